```python
import math
import jax, jax.numpy as jnp
from jax import lax
import numpy as np

D_MODEL = 1024
BATCH = 32
SEQ = 256
DEPTH = 4
DEC_BATCH = 8
DEC_SEQ = 1024
PAST_LEN = 512

GRID_W = 64
N_PAIRS = DEPTH // 2
EPS = 1e-6
A_WIDTH = D_MODEL // 2
A_HEAD = 64
A_HEADS = A_WIDTH // A_HEAD
LORA_W = 64
LORA_A = 64
GN_EPS = 64e-5
B_WIDTH = D_MODEL // 2
CONV_W = 3
C_HEAD = 64
C_Q_HEADS = D_MODEL // C_HEAD
C_KV_HEADS = 4
C_GROUP = C_Q_HEADS // C_KV_HEADS
C_WIDTH = C_Q_HEADS * C_HEAD
KV_WIDTH = C_KV_HEADS * C_HEAD
Q_BLOCK = 128
ROPE_THETA = 10000.0
ROPE_HALF = C_HEAD // 2
A_SHIFT = 3 * A_WIDTH + 2 * LORA_W + 2 * LORA_A
EVEN_IN = A_SHIFT + A_WIDTH + 4 * B_WIDTH
ODD_IN = C_WIDTH + 2 * KV_WIDTH + C_WIDTH

kernel_name = 'hybrid_rwkv7_shortconv_gqa_diffusion_step'

f32 = jnp.float32


def rmsnorm(x, g):
    xf = x.astype(f32)
    y = xf * lax.rsqrt(jnp.mean(xf * xf, axis=-1, keepdims=True) + EPS)
    return (y * g.astype(f32)).astype(x.dtype)


def adaln(cond, w, b):
    return jax.nn.silu(cond) @ w + b


def modulate(x, g, mod):
    shift, scale, gate = jnp.split(mod, 3, axis=-1)
    h = rmsnorm(x, g) * (1 + scale[:, None]) + shift[:, None]
    return h, gate[:, None]


def _neighbours(f):
    zero = jnp.zeros_like(f[:, :1])
    prev = jnp.concatenate([zero, f[:, :-1]], axis=1)
    nxt = jnp.concatenate([f[:, 1:], zero], axis=1)
    return prev, nxt


def centred_shift(f, mu):
    prev, nxt = _neighbours(f)
    return f + mu * (0.5 * (prev + nxt) - f)


def centred_conv3(u, w, b):
    prev, nxt = _neighbours(u)
    return prev * w[0] + u * w[1] + nxt * w[2] + b


def rwkv_scan(S0, r, w, k, v, kk, a, reverse):
    def step(S, inp):
        r_t, w_t, k_t, v_t, kk_t, a_t = inp
        sk = jnp.einsum('bhvk,bhk->bhv', S, kk_t)
        S = (S * w_t[:, :, None, :] - sk[..., None] * (kk_t * a_t)[:, :, None, :]
             + v_t[..., None] * k_t[:, :, None, :])
        return S, jnp.einsum('bhvk,bhk->bhv', S, r_t)
    xs = tuple(jnp.moveaxis(t, 1, 0) for t in (r, w, k, v, kk, a))
    S, ys = lax.scan(step, S0, xs, reverse=reverse)
    return S, jnp.moveaxis(ys, 0, 1)


def rwkv_conv_mixer(h, S0, w_in, mu, lw2, w0, la2, a0, k_k, k_a, r_k, lnx_g, lnx_b,
                    conv_w, conv_b, w_out):
    B, L, _ = h.shape
    proj = h @ w_in
    feat = centred_shift(proj[..., :A_SHIFT], mu)
    r = feat[..., :A_WIDTH]
    k = feat[..., A_WIDTH:2 * A_WIDTH]
    v = feat[..., 2 * A_WIDTH:3 * A_WIDTH]
    o = 3 * A_WIDTH
    lw = feat[..., o:o + 2 * LORA_W].reshape(B, L, 2, LORA_W)
    o += 2 * LORA_W
    la = feat[..., o:o + 2 * LORA_A].reshape(B, L, 2, LORA_A)
    o = A_SHIFT
    g_a = proj[..., o:o + A_WIDTH]
    o += A_WIDTH
    bg = proj[..., o:o + B_WIDTH]
    cg = proj[..., o + B_WIDTH:o + 2 * B_WIDTH]
    xin = proj[..., o + 2 * B_WIDTH:o + 3 * B_WIDTH]
    g_b = proj[..., o + 3 * B_WIDTH:o + 4 * B_WIDTH]

    w_log = -jax.nn.softplus(-(w0 + jnp.einsum('bldr,drc->bldc', jnp.tanh(lw), lw2))) - 0.5
    decay = jnp.exp(-jnp.exp(w_log.astype(f32)))
    a = jax.nn.sigmoid(a0 + jnp.einsum('bldr,drc->bldc', la, la2))
    k_dir = k[:, :, None, :] * (1 + (a - 1) * k_a)

    heads = lambda t: t.reshape(t.shape[:-1] + (A_HEADS, A_HEAD)).astype(f32)
    r_h, v_h = heads(r), heads(v)
    kk = heads(k * k_k)
    kk = kk / jnp.maximum(jnp.sqrt(jnp.sum(kk * kk, axis=-1, keepdims=True)), 1e-12)
    w_h, a_h, k_h = heads(decay), heads(a), heads(k_dir)
    S0 = S0.astype(f32)
    S_f, y_f = rwkv_scan(S0[:, 0], r_h, w_h[:, :, 0], k_h[:, :, 0], v_h, kk, a_h[:, :, 0], False)
    S_b, y_b = rwkv_scan(S0[:, 1], r_h, w_h[:, :, 1], k_h[:, :, 1], v_h, kk, a_h[:, :, 1], True)
    y = y_f + y_b
    mean = jnp.mean(y, axis=-1, keepdims=True)
    var = jnp.mean(jnp.square(y - mean), axis=-1, keepdims=True)
    y = ((y - mean) * lax.rsqrt(var + GN_EPS)).reshape(B, L, A_WIDTH) * lnx_g + lnx_b
    bonus = jnp.sum(r_h * jnp.mean(k_h, axis=2) * r_k, axis=-1, keepdims=True) * v_h
    out_a = (y + bonus.reshape(B, L, A_WIDTH)).astype(h.dtype) * jax.nn.silu(g_a)

    out_b = bg * centred_conv3(cg * xin, conv_w, conv_b) * jax.nn.silu(g_b)
    out = jnp.concatenate([out_a, out_b], axis=-1) @ w_out
    return out, jnp.stack([S_f, S_b], axis=1)


def gqa_project(h, w_in, q_g, k_g):
    B, L, _ = h.shape
    proj = h @ w_in
    q = proj[..., :C_WIDTH].reshape(B, L, C_Q_HEADS, C_HEAD)
    k = proj[..., C_WIDTH:C_WIDTH + KV_WIDTH].reshape(B, L, C_KV_HEADS, C_HEAD)
    v = proj[..., C_WIDTH + KV_WIDTH:C_WIDTH + 2 * KV_WIDTH].reshape(B, L, C_KV_HEADS, C_HEAD)
    g = proj[..., C_WIDTH + 2 * KV_WIDTH:]
    return rmsnorm(q, q_g), rmsnorm(k, k_g), v, g


def axial_rope_tables(L):
    rows = L // GRID_W
    row = jnp.repeat(jnp.arange(rows, dtype=f32), GRID_W)
    col = (jnp.arange(L) % GRID_W).astype(f32)
    inv = ROPE_THETA ** (-jnp.arange(0, ROPE_HALF, 2, dtype=f32) / ROPE_HALF)
    ang_r = row[:, None, None] * inv
    ang_c = col[:, None, None] * inv
    return jnp.cos(ang_r), jnp.sin(ang_r), jnp.cos(ang_c), jnp.sin(ang_c)


def _rope_half(x, cos, sin):
    x1, x2 = jnp.split(x, 2, axis=-1)
    return jnp.concatenate([x1 * cos - x2 * sin, x2 * cos + x1 * sin], axis=-1)


def apply_axial_rope(x, tables):
    cos_r, sin_r, cos_c, sin_c = tables
    xf = x.astype(f32)
    out = jnp.concatenate([_rope_half(xf[..., :ROPE_HALF], cos_r, sin_r),
                           _rope_half(xf[..., ROPE_HALF:], cos_c, sin_c)], axis=-1)
    return out.astype(x.dtype)


def block_attention(q, k, v):
    B, L = q.shape[:2]
    nb = L // Q_BLOCK
    qb = q.reshape(B, nb, Q_BLOCK, C_KV_HEADS, C_GROUP, C_HEAD).swapaxes(0, 1)
    scale = C_HEAD ** -0.5

    def one(qblk):
        s = jnp.einsum('bqhgd,bhkd->bhgqk', qblk, k).astype(f32) * scale
        p = jax.nn.softmax(s, axis=-1).astype(v.dtype)
        return jnp.einsum('bhgqk,bhkd->bqhgd', p, v)

    out = lax.map(one, qb)
    return out.swapaxes(0, 1).reshape(B, L, C_WIDTH)


def setup_inputs(seed: int = 0) -> dict:
    key = jax.random.key(seed)
    ks = iter(jax.random.split(key, 40))
    nrm = lambda shape, s=1.0: s * jax.random.normal(next(ks), shape, f32)
    P = N_PAIRS
    return {
        'x_prompt': nrm((BATCH, SEQ, D_MODEL)),
        'x_sample': nrm((DEC_BATCH, DEC_SEQ, D_MODEL)),
        'c': nrm((DEC_BATCH, D_MODEL)),
        'state_rwkv': nrm((DEC_BATCH, P, 2, A_HEADS, A_HEAD, A_HEAD), 0.3),
        'cache_k': nrm((DEC_BATCH, P, C_KV_HEADS, PAST_LEN, C_HEAD)),
        'cache_v': nrm((DEC_BATCH, P, C_KV_HEADS, PAST_LEN, C_HEAD)),
        'c_ctx': nrm((D_MODEL,)),
        'w_ada': nrm((DEPTH, D_MODEL, 3 * D_MODEL), 0.5 * D_MODEL ** -0.5),
        'b_ada': nrm((DEPTH, 3 * D_MODEL), 0.02),
        'norm_g': 1.0 + nrm((DEPTH, D_MODEL), 0.02),
        'final_g': 1.0 + nrm((D_MODEL,), 0.02),
        'w_in_e': nrm((P, D_MODEL, EVEN_IN), D_MODEL ** -0.5),
        'mu_shift': jax.random.uniform(next(ks), (P, A_SHIFT), f32),
        'lora_w2': nrm((P, 2, LORA_W, A_WIDTH), 0.1 * LORA_W ** -0.5),
        'w0': jax.random.uniform(next(ks), (P, 2, A_WIDTH), f32, -6.0, 1.0),
        'lora_a2': nrm((P, 2, LORA_A, A_WIDTH), 0.1 * LORA_A ** -0.5),
        'a0': nrm((P, 2, A_WIDTH), 0.1),
        'k_k': 0.85 + nrm((P, A_WIDTH), 0.02),
        'k_a': 1.0 + nrm((P, A_WIDTH), 0.02),
        'r_k': nrm((P, A_HEADS, A_HEAD), 0.1),
        'lnx_g': 1.0 + nrm((P, A_WIDTH), 0.02),
        'lnx_b': nrm((P, A_WIDTH), 0.01),
        'conv_w': nrm((P, CONV_W, B_WIDTH), CONV_W ** -0.5),
        'conv_b': nrm((P, B_WIDTH), 0.01),
        'w_out_e': nrm((P, A_WIDTH + B_WIDTH, D_MODEL), (A_WIDTH + B_WIDTH) ** -0.5),
        'w_in_o': nrm((P, D_MODEL, ODD_IN), D_MODEL ** -0.5),
        'q_norm_g': 1.0 + nrm((P, C_HEAD), 0.02),
        'k_norm_g': 1.0 + nrm((P, C_HEAD), 0.02),
        'w_out_o': nrm((P, C_WIDTH, D_MODEL), C_WIDTH ** -0.5),
    }


def reference(x_prompt, x_sample, c, state_rwkv, cache_k, cache_v, c_ctx, w_ada, b_ada,
              norm_g, final_g, w_in_e, mu_shift, lora_w2, w0, lora_a2, a0, k_k, k_a, r_k,
              lnx_g, lnx_b, conv_w, conv_b, w_out_e, w_in_o, q_norm_g, k_norm_g, w_out_o):
    xp, xs = x_prompt, x_sample
    rope = axial_rope_tables(xs.shape[1])
    zero_state = jnp.zeros((xp.shape[0], 2, A_HEADS, A_HEAD, A_HEAD), f32)
    new_rwkv, new_k, new_v = [], [], []
    for layer in range(DEPTH):
        p = layer // 2
        mod_p = adaln(c_ctx[None, :], w_ada[layer], b_ada[layer])
        mod_s = adaln(c, w_ada[layer], b_ada[layer])
        hp, gate_p = modulate(xp, norm_g[layer], mod_p)
        hs, gate_s = modulate(xs, norm_g[layer], mod_s)
        if layer % 2 == 0:
            ew = (w_in_e[p], mu_shift[p], lora_w2[p], w0[p], lora_a2[p], a0[p], k_k[p],
                  k_a[p], r_k[p], lnx_g[p], lnx_b[p], conv_w[p], conv_b[p], w_out_e[p])
            out_p, st_p = rwkv_conv_mixer(hp, zero_state, *ew)
            out_s, _ = rwkv_conv_mixer(hs, state_rwkv[:, p], *ew)
            new_rwkv.append(st_p)
        else:
            qp, kp, vp, gp = gqa_project(hp, w_in_o[p], q_norm_g[p], k_norm_g[p])
            kp_t, vp_t = kp.transpose(0, 2, 1, 3), vp.transpose(0, 2, 1, 3)
            out_p = (block_attention(qp, kp_t, vp_t) * jax.nn.silu(gp)) @ w_out_o[p]
            qs, ks_, vs, gs = gqa_project(hs, w_in_o[p], q_norm_g[p], k_norm_g[p])
            qs = apply_axial_rope(qs, rope)
            ks_ = apply_axial_rope(ks_, rope)
            keys = jnp.concatenate([cache_k[:, p], ks_.transpose(0, 2, 1, 3)], axis=2)
            vals = jnp.concatenate([cache_v[:, p], vs.transpose(0, 2, 1, 3)], axis=2)
            out_s = (block_attention(qs, keys, vals) * jax.nn.silu(gs)) @ w_out_o[p]
            new_k.append(kp_t)
            new_v.append(vp_t)
        xp = xp + gate_p * out_p
        xs = xs + gate_s * out_s
    y_prompt = rmsnorm(xp, final_g)
    y_sample = rmsnorm(xs, final_g)
    new_state_rwkv = jnp.stack(new_rwkv, axis=1)
    new_cache_k = jnp.stack(new_k, axis=1)
    new_cache_v = jnp.stack(new_v, axis=1)
    return (y_prompt, y_sample, new_state_rwkv, new_cache_k, new_cache_v)
```

```python
import functools

import numpy as np
import jax
import jax.numpy as jnp
from jax import lax
from jax.experimental import pallas as pl
from jax.experimental.pallas import tpu as pltpu

f32 = jnp.float32
bf16 = jnp.bfloat16
HI = lax.Precision.HIGHEST

D_MODEL = 1024
BATCH = 32
SEQ = 256
DEPTH = 4
DEC_BATCH = 8
DEC_SEQ = 1024
PAST_LEN = 512
GRID_W = 64
EPS = 1e-6
A_WIDTH = 512
A_HEAD = 64
A_HEADS = 8
LORA = 64
GN_EPS = 64e-5
B_WIDTH = 512
C_HEAD = 64
C_Q_HEADS = 16
C_KV_HEADS = 4
C_GROUP = 4
KV_WIDTH = 256
ROPE_THETA = 10000.0
ROPE_HALF = 32
A_SHIFT = 3 * A_WIDTH + 4 * LORA
EVEN_IN = A_SHIFT + A_WIDTH + 4 * B_WIDTH
ODD_IN = 2 * D_MODEL + 2 * KV_WIDTH

TOK_P = BATCH * SEQ
TOK_S = DEC_BATCH * DEC_SEQ
TOK = TOK_P + TOK_S
ROW_BLK = 256
N_MOD = 16
CHUNK = 64
LANE = 128
VMEM_LIMIT = 48 * 1024 * 1024


def _sigmoid(x):
    return 1.0 / (1.0 + jnp.exp(-x))


def _silu(x):
    return x * _sigmoid(x)


def _dot(a, b, precision=None):
    return jnp.dot(a, b, precision=precision, preferred_element_type=f32)


def _dot_nt(a, b, precision=None):
    return lax.dot_general(a, b, (((1,), (1,)), ((), ())), precision=precision,
                           preferred_element_type=f32)


def _dot_tn(a, b, precision=None):
    return lax.dot_general(a, b, (((0,), (0,)), ((), ())), precision=precision,
                           preferred_element_type=f32)


def _mod_row(i):
    return jnp.where(i < TOK_P // ROW_BLK, 0, 1 + (i - TOK_P // ROW_BLK) // (DEC_SEQ // ROW_BLK))


def _head_block_diag():
    ri = lax.broadcasted_iota(jnp.int32, (LANE, LANE), 0) >> 6
    ci = lax.broadcasted_iota(jnp.int32, (LANE, LANE), 1) >> 6
    return (ri == ci).astype(f32)


def _head_sum(x, bd):
    n = x.shape[1] // LANE
    cols = [_dot(x[:, i * LANE:(i + 1) * LANE], bd, HI) for i in range(n)]
    return cols[0] if n == 1 else jnp.concatenate(cols, axis=1)


def _adaln_kernel(cond_ref, w_ref, b_ref, o_ref):
    o_ref[...] = _dot(_silu(cond_ref[...]), w_ref[...], HI) + b_ref[...]


def _adaln(cond, w_ada, b_ada):
    tn = 768
    return pl.pallas_call(
        _adaln_kernel,
        grid=(DEPTH, 3 * D_MODEL // tn),
        in_specs=[
            pl.BlockSpec((N_MOD, D_MODEL), lambda l, n: (0, 0)),
            pl.BlockSpec((None, D_MODEL, tn), lambda l, n: (l, 0, n)),
            pl.BlockSpec((None, 1, tn), lambda l, n: (l, 0, n)),
        ],
        out_specs=pl.BlockSpec((None, N_MOD, tn), lambda l, n: (l, 0, n)),
        out_shape=jax.ShapeDtypeStruct((DEPTH, N_MOD, 3 * D_MODEL), f32),
        compiler_params=pltpu.CompilerParams(vmem_limit_bytes=VMEM_LIMIT),
        name="adaln",
    )(cond, w_ada, b_ada.reshape(DEPTH, 1, 3 * D_MODEL))


def _inproj_kernel(x_ref, mod_ref, g_ref, w_ref, o_ref):
    x = x_ref[...]
    y = x * lax.rsqrt(jnp.mean(x * x, axis=-1, keepdims=True) + EPS) * g_ref[...]
    m = mod_ref[...]
    h = y * (1.0 + m[:, D_MODEL:2 * D_MODEL]) + m[:, :D_MODEL]
    o_ref[...] = _dot(h.astype(bf16), w_ref[...])


def _inproj(x, mod_l, g, w):
    n = w.shape[1]
    return pl.pallas_call(
        _inproj_kernel,
        grid=(TOK // ROW_BLK,),
        in_specs=[
            pl.BlockSpec((ROW_BLK, D_MODEL), lambda i: (i, 0)),
            pl.BlockSpec((None, 1, 3 * D_MODEL), lambda i: (_mod_row(i), 0, 0)),
            pl.BlockSpec((1, D_MODEL), lambda i: (0, 0)),
            pl.BlockSpec((D_MODEL, n), lambda i: (0, 0)),
        ],
        out_specs=pl.BlockSpec((ROW_BLK, n), lambda i: (i, 0)),
        out_shape=jax.ShapeDtypeStruct((TOK, n), f32),
        compiler_params=pltpu.CompilerParams(vmem_limit_bytes=VMEM_LIMIT),
        name="inproj",
    )(x, mod_l, g, w)


def _outproj_kernel(a_ref, x_ref, mod_ref, w_ref, fg_ref, o_ref, *, final):
    o = _dot(a_ref[...].astype(bf16), w_ref[...])
    y = x_ref[...] + mod_ref[:, 2 * D_MODEL:] * o
    if final:
        y = y * lax.rsqrt(jnp.mean(y * y, axis=-1, keepdims=True) + EPS) * fg_ref[...]
    o_ref[...] = y


def _outproj(a, x, mod_l, w, fg, *, final, row0, rows):
    b0 = row0 // ROW_BLK
    return pl.pallas_call(
        functools.partial(_outproj_kernel, final=final),
        grid=(rows // ROW_BLK,),
        in_specs=[
            pl.BlockSpec((ROW_BLK, D_MODEL), lambda i: (b0 + i, 0)),
            pl.BlockSpec((ROW_BLK, D_MODEL), lambda i: (b0 + i, 0)),
            pl.BlockSpec((None, 1, 3 * D_MODEL), lambda i: (_mod_row(b0 + i), 0, 0)),
            pl.BlockSpec((D_MODEL, D_MODEL), lambda i: (0, 0)),
            pl.BlockSpec((1, D_MODEL), lambda i: (0, 0)),
        ],
        out_specs=pl.BlockSpec((ROW_BLK, D_MODEL), lambda i: (i, 0)),
        out_shape=jax.ShapeDtypeStruct((rows, D_MODEL), f32),
        compiler_params=pltpu.CompilerParams(vmem_limit_bytes=VMEM_LIMIT),
        name="outproj_final" if final else "outproj",
    )(a, x, mod_l, w, fg)


def _scan_masks():
    ri = lax.broadcasted_iota(jnp.int32, (CHUNK, CHUNK), 0)
    ci = lax.broadcasted_iota(jnp.int32, (CHUNK, CHUNK), 1)
    eye = ri == ci
    strict = (ri > ci, ri < ci)
    incl = (ri >= ci, ri <= ci)
    levels = []
    for sh in range(6):
        levels.append(((ri >> (sh + 1)) == (ci >> (sh + 1))) & ((ri >> sh) != (ci >> sh)))
    return eye, strict, incl, levels


def _unit_tri_inverse(a, eye_f, levels):
    t = eye_f - jnp.where(levels[0], a, 0.0)
    for m in levels[1:]:
        off = jnp.where(m, a, 0.0)
        t = t - _dot(t, _dot(off, t, HI), HI)
    return t


def _mixer_kernel(*refs, L, has_s0, write_s, unroll):
    it = iter(refs)
    pr, pk, pv, plo, ga, bg, cg, xi, gb = (next(it) for _ in range(9))
    mur, muk, muv, mulo = (next(it) for _ in range(4))
    lw2, w0, la2, a0, kkw, kaw, rkw, lng, lnb, cw, cb = (next(it) for _ in range(11))
    s0 = next(it) if has_s0 else None
    mix = next(it)
    sout = next(it) if write_s else None
    r_s, v_s, kap_s, kd_s, b_s, lw_s, y_s, rp_s, y0_s, mt_s, nt_s, st_s = it

    nch = L // CHUNK
    rows = lax.broadcasted_iota(jnp.int32, (L, 1), 0)
    first, last = rows == 0, rows == L - 1

    def prev(f):
        return jnp.where(first, 0.0, pltpu.roll(f, 1, 0))

    def nxt(f):
        return jnp.where(last, 0.0, pltpu.roll(f, L - 1, 0))

    def tshift(f, mu):
        return f + mu * (0.5 * (prev(f) + nxt(f)) - f)

    bd = _head_block_diag()

    r = tshift(pr[...], mur[...])
    k = tshift(pk[...], muk[...])
    v = tshift(pv[...], muv[...])
    lo = tshift(plo[...], mulo[...])
    kk = k * kkw[...]
    kap = kk / jnp.maximum(jnp.sqrt(_head_sum(kk * kk, bd)), 1e-12)
    r_s[...] = r
    v_s[...] = v
    kap_s[...] = kap
    for d in range(2):
        lwd = lo[:, d * LORA:(d + 1) * LORA]
        lad = lo[:, 2 * LORA + d * LORA:2 * LORA + (d + 1) * LORA]
        wl = w0[d] + _dot(jnp.tanh(lwd).astype(bf16), lw2[d].astype(bf16))
        softplus = jnp.maximum(-wl, 0.0) + jnp.log(1.0 + jnp.exp(-jnp.abs(wl)))
        lw_s[d] = -jnp.exp(-softplus - 0.5)
        a = _sigmoid(a0[d] + _dot(lad.astype(bf16), la2[d].astype(bf16)))
        kd_s[d] = k * (1.0 + (a - 1.0) * kaw[...])
        b_s[d] = a * kap

    eye, strict, incl, levels = _scan_masks()
    eye_f = eye.astype(f32)
    cum = (incl[0].astype(f32), incl[1].astype(f32))

    def chunk_prepass(c):
        rs = pl.ds(pl.multiple_of(c * CHUNK, CHUNK), CHUNK)
        r2, v2, kap2 = r_s[rs, :], v_s[rs, :], kap_s[rs, :]
        for d in range(2):
            lw = lw_s[d, rs, :]
            g = _dot(cum[d], lw, HI)
            tot = g[CHUNK - 1:CHUNK, :] if d == 0 else g[0:1, :]
            e_ng = jnp.exp(-g)
            e_rem = jnp.exp(tot - g)
            e_tot = jnp.exp(tot)
            kd2, b2 = kd_s[d, rs, :], b_s[d, rs, :]
            kt2 = kap2 * jnp.exp(g - lw)
            rt2 = r2 * jnp.exp(g)
            kh2, bh2 = kd2 * e_ng, b2 * e_ng
            kb2, bb2 = kd2 * e_rem, b2 * e_rem
            for hh in range(2):
                sl = slice(hh * A_HEAD, (hh + 1) * A_HEAD)
                kt, rt, kh, bh, kb, bb, vv = (x[:, sl] for x in (kt2, rt2, kh2, bh2, kb2, bb2, v2))
                lhs = jnp.concatenate([kt, rt], axis=0)
                a1 = _dot_nt(lhs, kh, HI)
                a2 = _dot_nt(lhs, bh, HI)
                akk = jnp.where(strict[d], a1[:CHUNK], 0.0)
                ark = jnp.where(incl[d], a1[CHUNK:], 0.0)
                akb = jnp.where(strict[d], a2[:CHUNK], 0.0)
                arb = jnp.where(incl[d], a2[CHUNK:], 0.0)
                t = _unit_tri_inverse(akb, eye_f, levels)
                w0v = _dot(akk, vv, HI)
                kp = _dot(t, kt, HI)
                w1 = _dot(t, w0v, HI)
                idx = d * 2 + hh
                rp_s[idx, rs, :] = rt - _dot(arb, kp, HI)
                y0_s[idx, rs, :] = _dot(ark, vv, HI) - _dot(arb, w1, HI)
                mt_s[idx, rs, :] = jnp.where(eye, e_tot[:, sl], 0.0) - _dot_tn(kp, bb, HI)
                nt_s[idx, rs, :] = _dot_tn(vv, kb, HI) - _dot_tn(w1, bb, HI)

    def prepass_body(i, carry):
        for u in range(unroll):
            chunk_prepass(i * unroll + u)
        return carry

    lax.fori_loop(0, nch // unroll, prepass_body, 0)

    for d in range(2):
        for hh in range(2):
            st_s[d * 2 + hh] = s0[d, hh] if has_s0 else jnp.zeros((A_HEAD, A_HEAD), f32)

    def seq_body(i, carry):
        for d in range(2):
            c = i if d == 0 else nch - 1 - i
            rs = pl.ds(pl.multiple_of(c * CHUNK, CHUNK), CHUNK)
            for hh in range(2):
                idx = d * 2 + hh
                s = st_s[idx]
                y_s[d, rs, hh * A_HEAD:(hh + 1) * A_HEAD] = y0_s[idx, rs, :] + _dot_nt(rp_s[idx, rs, :], s, HI)
                st_s[idx] = _dot(s, mt_s[idx, rs, :], HI) + nt_s[idx, rs, :]
        return carry

    lax.fori_loop(0, nch, seq_body, 0)

    if write_s:
        for d in range(2):
            for hh in range(2):
                sout[d, hh] = st_s[d * 2 + hh]

    y = y_s[0] + y_s[1]
    mean = _head_sum(y, bd) * (1.0 / A_HEAD)
    yc = y - mean
    var = _head_sum(yc * yc, bd) * (1.0 / A_HEAD)
    yn = yc * lax.rsqrt(var + GN_EPS) * lng[...] + lnb[...]
    kmean = 0.5 * (kd_s[0] + kd_s[1])
    bonus = _head_sum(r_s[...] * kmean * rkw[...], bd) * v_s[...]
    mix[:, :LANE] = (yn + bonus) * _silu(ga[...])
    u = cg[...] * xi[...]
    conv = prev(u) * cw[0:1, :] + u * cw[1:2, :] + nxt(u) * cw[2:3, :] + cb[...]
    mix[:, LANE:] = bg[...] * conv * _silu(gb[...])


def _mixer(proj, prm, s0, *, L, n_seq, row0, write_s):
    rb0 = row0 // L
    has_s0 = s0 is not None
    nhp = A_WIDTH // LANE
    col = lambda base: (lambda s, j: (rb0 + s, base + j))
    vec = lambda base: (lambda s, j: (0, base + j))
    in_specs = [
        pl.BlockSpec((L, LANE), col(0)),
        pl.BlockSpec((L, LANE), col(nhp)),
        pl.BlockSpec((L, LANE), col(2 * nhp)),
        pl.BlockSpec((L, 2 * LANE), lambda s, j: (rb0 + s, 3 * A_WIDTH // (2 * LANE))),
        pl.BlockSpec((L, LANE), col(A_SHIFT // LANE)),
        pl.BlockSpec((L, LANE), col((A_SHIFT + A_WIDTH) // LANE)),
        pl.BlockSpec((L, LANE), col((A_SHIFT + A_WIDTH + B_WIDTH) // LANE)),
        pl.BlockSpec((L, LANE), col((A_SHIFT + A_WIDTH + 2 * B_WIDTH) // LANE)),
        pl.BlockSpec((L, LANE), col((A_SHIFT + A_WIDTH + 3 * B_WIDTH) // LANE)),
        pl.BlockSpec((1, LANE), vec(0)),
        pl.BlockSpec((1, LANE), vec(nhp)),
        pl.BlockSpec((1, LANE), vec(2 * nhp)),
        pl.BlockSpec((1, 2 * LANE), lambda s, j: (0, 3 * A_WIDTH // (2 * LANE))),
        pl.BlockSpec((2, LORA, LANE), lambda s, j: (0, 0, j)),
        pl.BlockSpec((2, 1, LANE), lambda s, j: (0, 0, j)),
        pl.BlockSpec((2, LORA, LANE), lambda s, j: (0, 0, j)),
        pl.BlockSpec((2, 1, LANE), lambda s, j: (0, 0, j)),
        pl.BlockSpec((1, LANE), vec(0)),
        pl.BlockSpec((1, LANE), vec(0)),
        pl.BlockSpec((1, LANE), vec(0)),
        pl.BlockSpec((1, LANE), vec(0)),
        pl.BlockSpec((1, LANE), vec(0)),
        pl.BlockSpec((3, LANE), vec(0)),
        pl.BlockSpec((1, LANE), vec(0)),
    ]
    args = [proj] * 9 + [prm["mu"]] * 4 + [prm["lw2"], prm["w0"], prm["la2"], prm["a0"], prm["k_k"],
                                            prm["k_a"], prm["r_k"], prm["lnx_g"], prm["lnx_b"],
                                            prm["conv_w"], prm["conv_b"]]
    state_spec = pl.BlockSpec((None, 2, 2, A_HEAD, A_HEAD), lambda s, j: (s, 0, j, 0, 0))
    if has_s0:
        in_specs.append(pl.BlockSpec((None, None, 2, 2, A_HEAD, A_HEAD),
                                     lambda s, j: (s, prm["pair"], 0, j, 0, 0)))
        args.append(s0)
    out_specs = [pl.BlockSpec((L, 2 * LANE), lambda s, j: (s, j))]
    out_shape = [jax.ShapeDtypeStruct((n_seq * L, D_MODEL), f32)]
    if write_s:
        out_specs.append(state_spec)
        out_shape.append(jax.ShapeDtypeStruct((n_seq, 2, A_HEADS, A_HEAD, A_HEAD), f32))
    tok = lambda w=LANE: pltpu.VMEM((L, w), f32)
    scratch = [tok(), tok(), tok(),
               pltpu.VMEM((2, L, LANE), f32), pltpu.VMEM((2, L, LANE), f32),
               pltpu.VMEM((2, L, LANE), f32), pltpu.VMEM((2, L, LANE), f32),
               pltpu.VMEM((4, L, A_HEAD), f32), pltpu.VMEM((4, L, A_HEAD), f32),
               pltpu.VMEM((4, L, A_HEAD), f32), pltpu.VMEM((4, L, A_HEAD), f32),
               pltpu.VMEM((4, A_HEAD, A_HEAD), f32)]
    out = pl.pallas_call(
        functools.partial(_mixer_kernel, L=L, has_s0=has_s0, write_s=write_s,
                          unroll=min(4, L // CHUNK)),
        grid=(n_seq, nhp),
        in_specs=in_specs,
        out_specs=out_specs,
        out_shape=out_shape,
        scratch_shapes=scratch,
        compiler_params=pltpu.CompilerParams(vmem_limit_bytes=VMEM_LIMIT),
        name="rwkv_conv_mixer_L%d" % L,
    )(*args)
    return out


def _attn_kernel(*refs, LQ, LN, PAST, rope, write_cache):
    it = iter(refs)
    q_ref, k_ref, v_ref, g_ref, qg_ref, kg_ref = (next(it) for _ in range(6))
    if rope:
        cosq, sinq, cosk, sink = (next(it) for _ in range(4))
    if PAST:
        ck, cv = next(it), next(it)
    out_ref = next(it)
    if write_cache:
        nk_ref, nv_ref = next(it), next(it)
    kbuf, vbuf, obuf = it

    bd = _head_block_diag()
    lane = lax.broadcasted_iota(jnp.int32, (1, LANE), 1)
    first_half = (lane & 31) < 16

    def rms(x, g):
        return x * lax.rsqrt(_head_sum(x * x, bd) * (1.0 / C_HEAD) + EPS) * g

    def rotate(x, cos, sin):
        outs = []
        for i in range(x.shape[1] // LANE):
            xb = x[:, i * LANE:(i + 1) * LANE]
            sw = jnp.where(first_half, pltpu.roll(xb, LANE - 16, 1), pltpu.roll(xb, 16, 1))
            outs.append(xb * cos + sw * sin)
        return jnp.concatenate(outs, axis=1)

    @pl.when(pl.program_id(1) == 0)
    def _():
        kn = rms(k_ref[...], kg_ref[...])
        vx = v_ref[...]
        if write_cache:
            for h in range(C_KV_HEADS):
                nk_ref[h] = kn[:, h * C_HEAD:(h + 1) * C_HEAD]
                nv_ref[h] = vx[:, h * C_HEAD:(h + 1) * C_HEAD]
        if rope:
            kn = rotate(kn, cosk[...], sink[...])
        for h in range(C_KV_HEADS):
            if PAST:
                kbuf[h, :PAST, :] = ck[h].astype(bf16)
                vbuf[h, :PAST, :] = cv[h].astype(bf16)
            kbuf[h, PAST:, :] = kn[:, h * C_HEAD:(h + 1) * C_HEAD].astype(bf16)
            vbuf[h, PAST:, :] = vx[:, h * C_HEAD:(h + 1) * C_HEAD].astype(bf16)

    qn = rms(q_ref[...], qg_ref[...])
    if rope:
        qn = rotate(qn, cosq[...], sinq[...])
    qn = (qn * (C_HEAD ** -0.5)).astype(bf16)
    for h in range(C_KV_HEADS):
        kh, vh = kbuf[h], vbuf[h]
        for g in range(C_GROUP):
            sl = slice((h * C_GROUP + g) * C_HEAD, (h * C_GROUP + g + 1) * C_HEAD)
            s = _dot_nt(qn[:, sl], kh)
            p = jnp.exp(s - jnp.max(s, axis=-1, keepdims=True))
            o = _dot(p.astype(bf16), vh) / jnp.sum(p, axis=-1, keepdims=True)
            obuf[:, sl] = o
    out_ref[...] = obuf[...] * _silu(g_ref[...])


def _rope_tables():
    pos = np.arange(DEC_SEQ)
    inv = ROPE_THETA ** (-np.arange(0, ROPE_HALF, 2, dtype=np.float32) / ROPE_HALF)
    ang_r = (pos // GRID_W).astype(np.float32)[:, None] * inv
    ang_c = (pos % GRID_W).astype(np.float32)[:, None] * inv
    return ang_r, ang_c


def _attention(proj, prm, cache_k, cache_v, tables, *, n_seq, L, row0, sample):
    LQ = ROW_BLK
    nq = L // LQ
    qb0, kb0 = row0 // LQ, row0 // L
    past = PAST_LEN if sample else 0
    in_specs = [
        pl.BlockSpec((LQ, D_MODEL), lambda b, i: (qb0 + b * nq + i, 0)),
        pl.BlockSpec((L, KV_WIDTH), lambda b, i: (kb0 + b, 2 * D_MODEL // KV_WIDTH)),
        pl.BlockSpec((L, KV_WIDTH), lambda b, i: (kb0 + b, 2 * D_MODEL // KV_WIDTH + 1)),
        pl.BlockSpec((LQ, D_MODEL), lambda b, i: (qb0 + b * nq + i, 1)),
        pl.BlockSpec((1, D_MODEL), lambda b, i: (0, 0)),
        pl.BlockSpec((1, KV_WIDTH), lambda b, i: (0, 0)),
    ]
    args = [proj, proj, proj, proj, prm["q_g"], prm["k_g"]]
    if sample:
        cos, sin = tables
        in_specs += [pl.BlockSpec((LQ, LANE), lambda b, i: (i, 0)),
                     pl.BlockSpec((LQ, LANE), lambda b, i: (i, 0)),
                     pl.BlockSpec((L, LANE), lambda b, i: (0, 0)),
                     pl.BlockSpec((L, LANE), lambda b, i: (0, 0))]
        args += [cos, sin, cos, sin]
        cspec = pl.BlockSpec((None, None, C_KV_HEADS, PAST_LEN, C_HEAD),
                             lambda b, i: (b, prm["pair"], 0, 0, 0))
        in_specs += [cspec, cspec]
        args += [cache_k, cache_v]
    out_specs = [pl.BlockSpec((LQ, D_MODEL), lambda b, i: (b * nq + i, 0))]
    out_shape = [jax.ShapeDtypeStruct((n_seq * L, D_MODEL), f32)]
    if not sample:
        nspec = pl.BlockSpec((None, C_KV_HEADS, L, C_HEAD), lambda b, i: (b, 0, 0, 0))
        out_specs += [nspec, nspec]
        out_shape += [jax.ShapeDtypeStruct((n_seq, C_KV_HEADS, L, C_HEAD), f32)] * 2
    return pl.pallas_call(
        functools.partial(_attn_kernel, LQ=LQ, LN=L, PAST=past, rope=sample, write_cache=not sample),
        grid=(n_seq, nq),
        in_specs=in_specs,
        out_specs=out_specs,
        out_shape=out_shape,
        scratch_shapes=[pltpu.VMEM((C_KV_HEADS, past + L, C_HEAD), bf16),
                        pltpu.VMEM((C_KV_HEADS, past + L, C_HEAD), bf16),
                        pltpu.VMEM((LQ, D_MODEL), f32)],
        compiler_params=pltpu.CompilerParams(vmem_limit_bytes=VMEM_LIMIT),
        name="gqa_sample" if sample else "gqa_prompt",
    )(*args)


def kernel(x_prompt, x_sample, c, state_rwkv, cache_k, cache_v, c_ctx, w_ada, b_ada, norm_g, final_g,
           w_in_e, mu_shift, lora_w2, w0, lora_a2, a0, k_k, k_a, r_k, lnx_g, lnx_b, conv_w, conv_b,
           w_out_e, w_in_o, q_norm_g, k_norm_g, w_out_o):
    x = jnp.concatenate([x_prompt.reshape(TOK_P, D_MODEL), x_sample.reshape(TOK_S, D_MODEL)], axis=0)
    cond = jnp.concatenate([c_ctx[None, :], c, jnp.zeros((N_MOD - 1 - DEC_BATCH, D_MODEL), f32)], axis=0)
    mod = _adaln(cond, w_ada, b_ada).reshape(DEPTH, N_MOD, 1, 3 * D_MODEL)

    j = np.arange(D_MODEL) // (2 * LANE)
    w = np.arange(D_MODEL) % (2 * LANE)
    mix_rows = np.where(w < LANE, j * LANE + w, A_WIDTH + j * LANE + (w - LANE))

    ang_r, ang_c = _rope_tables()
    cos64 = np.concatenate([np.cos(ang_r), np.cos(ang_r), np.cos(ang_c), np.cos(ang_c)], axis=1)
    sin64 = np.concatenate([-np.sin(ang_r), np.sin(ang_r), -np.sin(ang_c), np.sin(ang_c)], axis=1)
    tables = (jnp.asarray(np.tile(cos64, (1, 2)), f32), jnp.asarray(np.tile(sin64, (1, 2)), f32))

    final_g2 = final_g.reshape(1, D_MODEL)
    new_rwkv, new_k, new_v = [], [], []
    y_prompt = y_sample = None
    for layer in range(DEPTH):
        p = layer // 2
        mod_l = mod[layer]
        g = norm_g[layer].reshape(1, D_MODEL)
        if layer % 2 == 0:
            proj = _inproj(x, mod_l, g, w_in_e[p].astype(bf16))
            prm = dict(pair=p, mu=mu_shift[p].reshape(1, A_SHIFT), lw2=lora_w2[p],
                       w0=w0[p].reshape(2, 1, A_WIDTH), la2=lora_a2[p], a0=a0[p].reshape(2, 1, A_WIDTH),
                       k_k=k_k[p].reshape(1, A_WIDTH), k_a=k_a[p].reshape(1, A_WIDTH),
                       r_k=r_k[p].reshape(1, A_WIDTH), lnx_g=lnx_g[p].reshape(1, A_WIDTH),
                       lnx_b=lnx_b[p].reshape(1, A_WIDTH), conv_w=conv_w[p],
                       conv_b=conv_b[p].reshape(1, B_WIDTH))
            mix_p, st_p = _mixer(proj, prm, None, L=SEQ, n_seq=BATCH, row0=0, write_s=True)
            (mix_s,) = _mixer(proj, prm, state_rwkv, L=DEC_SEQ, n_seq=DEC_BATCH, row0=TOK_P,
                              write_s=False)
            new_rwkv.append(st_p)
            act = jnp.concatenate([mix_p, mix_s], axis=0)
            w_out = w_out_e[p][mix_rows].astype(bf16)
        else:
            wi = w_in_o[p]
            wi = jnp.concatenate([wi[:, :D_MODEL], wi[:, D_MODEL + 2 * KV_WIDTH:],
                                  wi[:, D_MODEL:D_MODEL + 2 * KV_WIDTH]], axis=1).astype(bf16)
            proj = _inproj(x, mod_l, g, wi)
            prm = dict(pair=p, q_g=jnp.tile(q_norm_g[p], C_Q_HEADS).reshape(1, D_MODEL),
                       k_g=jnp.tile(k_norm_g[p], C_KV_HEADS).reshape(1, KV_WIDTH))
            att_p, nk, nv = _attention(proj, prm, None, None, None, n_seq=BATCH, L=SEQ, row0=0,
                                       sample=False)
            (att_s,) = _attention(proj, prm, cache_k, cache_v, tables, n_seq=DEC_BATCH, L=DEC_SEQ,
                                  row0=TOK_P, sample=True)
            new_k.append(nk)
            new_v.append(nv)
            act = jnp.concatenate([att_p, att_s], axis=0)
            w_out = w_out_o[p].astype(bf16)
        if layer < DEPTH - 1:
            x = _outproj(act, x, mod_l, w_out, final_g2, final=False, row0=0, rows=TOK)
        else:
            y_prompt = _outproj(act, x, mod_l, w_out, final_g2, final=True, row0=0, rows=TOK_P)
            y_sample = _outproj(act, x, mod_l, w_out, final_g2, final=True, row0=TOK_P, rows=TOK_S)
    return (y_prompt.reshape(BATCH, SEQ, D_MODEL), y_sample.reshape(DEC_BATCH, DEC_SEQ, D_MODEL),
            jnp.stack(new_rwkv, axis=1), jnp.stack(new_k, axis=1), jnp.stack(new_v, axis=1))
```

```python
import functools

import numpy as np
import jax
import jax.numpy as jnp
from jax import lax
from jax.experimental import pallas as pl
from jax.experimental.pallas import tpu as pltpu

f32 = jnp.float32
bf16 = jnp.bfloat16
HI = lax.Precision.HIGHEST

D_MODEL = 1024
BATCH = 32
SEQ = 256
DEPTH = 4
DEC_BATCH = 8
DEC_SEQ = 1024
PAST_LEN = 512
GRID_W = 64
EPS = 1e-6
A_WIDTH = 512
A_HEAD = 64
A_HEADS = 8
LORA = 64
GN_EPS = 64e-5
B_WIDTH = 512
C_HEAD = 64
C_Q_HEADS = 16
C_KV_HEADS = 4
C_GROUP = 4
KV_WIDTH = 256
ROPE_THETA = 10000.0
ROPE_HALF = 32
A_SHIFT = 3 * A_WIDTH + 4 * LORA
EVEN_IN = A_SHIFT + A_WIDTH + 4 * B_WIDTH
ODD_IN = 2 * D_MODEL + 2 * KV_WIDTH

TOK_P = BATCH * SEQ
TOK_S = DEC_BATCH * DEC_SEQ
TOK = TOK_P + TOK_S
ROW_BLK = 256
N_MOD = 16
CHUNK = 64
LANE = 128
VMEM_LIMIT = 48 * 1024 * 1024


def _sigmoid(x):
    return 1.0 / (1.0 + jnp.exp(-x))


def _silu(x):
    return x * _sigmoid(x)


def _dot(a, b, precision=None):
    return jnp.dot(a, b, precision=precision, preferred_element_type=f32)


def _dot_nt(a, b, precision=None):
    return lax.dot_general(a, b, (((1,), (1,)), ((), ())), precision=precision,
                           preferred_element_type=f32)


def _dot_tn(a, b, precision=None):
    return lax.dot_general(a, b, (((0,), (0,)), ((), ())), precision=precision,
                           preferred_element_type=f32)


def _mod_row(i):
    return jnp.where(i < TOK_P // ROW_BLK, 0, 1 + (i - TOK_P // ROW_BLK) // (DEC_SEQ // ROW_BLK))


def _head_block_diag():
    ri = lax.broadcasted_iota(jnp.int32, (LANE, LANE), 0) >> 6
    ci = lax.broadcasted_iota(jnp.int32, (LANE, LANE), 1) >> 6
    return (ri == ci).astype(f32)


def _head_sum(x, bd):
    n = x.shape[1] // LANE
    cols = [_dot(x[:, i * LANE:(i + 1) * LANE], bd, HI) for i in range(n)]
    return cols[0] if n == 1 else jnp.concatenate(cols, axis=1)


def _adaln_kernel(cond_ref, w_ref, b_ref, o_ref):
    o_ref[...] = _dot(_silu(cond_ref[...]), w_ref[...], HI) + b_ref[...]


def _adaln(cond, w_ada, b_ada):
    tn = 768
    return pl.pallas_call(
        _adaln_kernel,
        grid=(DEPTH, 3 * D_MODEL // tn),
        in_specs=[
            pl.BlockSpec((N_MOD, D_MODEL), lambda l, n: (0, 0)),
            pl.BlockSpec((None, D_MODEL, tn), lambda l, n: (l, 0, n)),
            pl.BlockSpec((None, 1, tn), lambda l, n: (l, 0, n)),
        ],
        out_specs=pl.BlockSpec((None, N_MOD, tn), lambda l, n: (l, 0, n)),
        out_shape=jax.ShapeDtypeStruct((DEPTH, N_MOD, 3 * D_MODEL), f32),
        compiler_params=pltpu.CompilerParams(vmem_limit_bytes=VMEM_LIMIT),
        name="adaln",
    )(cond, w_ada, b_ada.reshape(DEPTH, 1, 3 * D_MODEL))


def _inproj_kernel(x_ref, mod_ref, g_ref, w_ref, o_ref):
    x = x_ref[...]
    y = x * lax.rsqrt(jnp.mean(x * x, axis=-1, keepdims=True) + EPS) * g_ref[...]
    m = mod_ref[...]
    h = y * (1.0 + m[:, D_MODEL:2 * D_MODEL]) + m[:, :D_MODEL]
    o_ref[...] = _dot(h.astype(bf16), w_ref[...])


def _inproj(x, mod_l, g, w):
    n = w.shape[1]
    return pl.pallas_call(
        _inproj_kernel,
        grid=(TOK // ROW_BLK,),
        in_specs=[
            pl.BlockSpec((ROW_BLK, D_MODEL), lambda i: (i, 0)),
            pl.BlockSpec((None, 1, 3 * D_MODEL), lambda i: (_mod_row(i), 0, 0)),
            pl.BlockSpec((1, D_MODEL), lambda i: (0, 0)),
            pl.BlockSpec((D_MODEL, n), lambda i: (0, 0)),
        ],
        out_specs=pl.BlockSpec((ROW_BLK, n), lambda i: (i, 0)),
        out_shape=jax.ShapeDtypeStruct((TOK, n), f32),
        compiler_params=pltpu.CompilerParams(vmem_limit_bytes=VMEM_LIMIT),
        name="inproj",
    )(x, mod_l, g, w)


def _outproj_kernel(a_ref, x_ref, mod_ref, w_ref, fg_ref, o_ref, *, final):
    o = _dot(a_ref[...].astype(bf16), w_ref[...])
    y = x_ref[...] + mod_ref[:, 2 * D_MODEL:] * o
    if final:
        y = y * lax.rsqrt(jnp.mean(y * y, axis=-1, keepdims=True) + EPS) * fg_ref[...]
    o_ref[...] = y


def _outproj(a, x, mod_l, w, fg, *, final, row0, rows):
    b0 = row0 // ROW_BLK
    return pl.pallas_call(
        functools.partial(_outproj_kernel, final=final),
        grid=(rows // ROW_BLK,),
        in_specs=[
            pl.BlockSpec((ROW_BLK, D_MODEL), lambda i: (b0 + i, 0)),
            pl.BlockSpec((ROW_BLK, D_MODEL), lambda i: (b0 + i, 0)),
            pl.BlockSpec((None, 1, 3 * D_MODEL), lambda i: (_mod_row(b0 + i), 0, 0)),
            pl.BlockSpec((D_MODEL, D_MODEL), lambda i: (0, 0)),
            pl.BlockSpec((1, D_MODEL), lambda i: (0, 0)),
        ],
        out_specs=pl.BlockSpec((ROW_BLK, D_MODEL), lambda i: (i, 0)),
        out_shape=jax.ShapeDtypeStruct((rows, D_MODEL), f32),
        compiler_params=pltpu.CompilerParams(vmem_limit_bytes=VMEM_LIMIT),
        name="outproj_final" if final else "outproj",
    )(a, x, mod_l, w, fg)


def _scan_masks():
    ri = lax.broadcasted_iota(jnp.int32, (CHUNK, CHUNK), 0)
    ci = lax.broadcasted_iota(jnp.int32, (CHUNK, CHUNK), 1)
    eye = ri == ci
    ri2 = lax.broadcasted_iota(jnp.int32, (CHUNK, 2 * CHUNK), 0)
    ci2 = lax.broadcasted_iota(jnp.int32, (CHUNK, 2 * CHUNK), 1) & (CHUNK - 1)
    strict2 = (ri2 > ci2, ri2 < ci2)
    incl2 = (ri2 >= ci2, ri2 <= ci2)
    levels = []
    for sh in range(6):
        levels.append(((ri >> (sh + 1)) == (ci >> (sh + 1))) & ((ri >> sh) != (ci >> sh)))
    return eye, strict2, incl2, levels


def _mixer_kernel(*refs, L, has_s0, write_s, unroll):
    it = iter(refs)
    pr, pk, pv, plo, ga, bg, cg, xi, gb = (next(it) for _ in range(9))
    mur, muk, muv, mulo = (next(it) for _ in range(4))
    lw2, w0, la2, a0, kkw, kaw, rkw, lng, lnb, cw, cb = (next(it) for _ in range(11))
    s0 = next(it) if has_s0 else None
    mix = next(it)
    sout = next(it) if write_s else None
    r_s, v_s, kap_s, kd_s, b_s, lw_s, g_s, y_s, rp_s, y0_s, mt_s, nt_s, st_s = it

    nch = L // CHUNK
    rows = lax.broadcasted_iota(jnp.int32, (L, 1), 0)
    first, last = rows == 0, rows == L - 1

    def prev(f):
        return jnp.where(first, 0.0, pltpu.roll(f, 1, 0))

    def nxt(f):
        return jnp.where(last, 0.0, pltpu.roll(f, L - 1, 0))

    def tshift(f, mu):
        return f + mu * (0.5 * (prev(f) + nxt(f)) - f)

    bd = _head_block_diag()

    r = tshift(pr[...], mur[...])
    k = tshift(pk[...], muk[...])
    v = tshift(pv[...], muv[...])
    lo = tshift(plo[...], mulo[...])
    kk = k * kkw[...]
    kap = kk / jnp.maximum(jnp.sqrt(_head_sum(kk * kk, bd)), 1e-12)
    r_s[...] = r
    v_s[...] = v
    kap_s[...] = kap
    crow = rows & (CHUNK - 1)
    for d in range(2):
        lwd = lo[:, d * LORA:(d + 1) * LORA]
        lad = lo[:, 2 * LORA + d * LORA:2 * LORA + (d + 1) * LORA]
        wl = w0[d] + _dot(jnp.tanh(lwd).astype(bf16), lw2[d].astype(bf16))
        softplus = jnp.maximum(-wl, 0.0) + jnp.log(1.0 + jnp.exp(-jnp.abs(wl)))
        lw = -jnp.exp(-softplus - 0.5)
        lw_s[d] = lw
        g = lw
        for sh in range(6):
            st = 1 << sh
            if d == 0:
                g = g + jnp.where(crow >= st, pltpu.roll(g, st, 0), 0.0)
            else:
                g = g + jnp.where(crow < CHUNK - st, pltpu.roll(g, L - st, 0), 0.0)
        g_s[d] = g
        a = _sigmoid(a0[d] + _dot(lad.astype(bf16), la2[d].astype(bf16)))
        kd_s[d] = k * (1.0 + (a - 1.0) * kaw[...])
        b_s[d] = a * kap

    eye, strict2, incl2, levels = _scan_masks()
    eye_f = eye.astype(f32)
    zblk = jnp.zeros((CHUNK, A_HEAD), f32)

    def prepass_group(cs):
        chains = []
        for c in cs:
            rs = pl.ds(pl.multiple_of(c * CHUNK, CHUNK), CHUNK)
            r2, v2, kap2 = r_s[rs, :], v_s[rs, :], kap_s[rs, :]
            for d in range(2):
                lw, g = lw_s[d, rs, :], g_s[d, rs, :]
                tot = g[CHUNK - 1:CHUNK, :] if d == 0 else g[0:1, :]
                e_ng = jnp.exp(-g)
                e_rem = jnp.exp(tot - g)
                e_tot = jnp.exp(tot)
                kd2, b2 = kd_s[d, rs, :], b_s[d, rs, :]
                kt2 = kap2 * jnp.exp(g - lw)
                rt2 = r2 * jnp.exp(g)
                kh2, bh2 = kd2 * e_ng, b2 * e_ng
                kb2, bb2 = kd2 * e_rem, b2 * e_rem
                for hh in range(2):
                    sl = slice(hh * A_HEAD, (hh + 1) * A_HEAD)
                    kt, rt, kh, bh, kb, bb, vv = (x[:, sl] for x in (kt2, rt2, kh2, bh2, kb2, bb2, v2))
                    chains.append(dict(
                        d=d, idx=d * 2 + hh, rs=rs, kt=kt, rt=rt, vv=vv, e_tot=e_tot[:, sl],
                        lhs=jnp.concatenate([kt, rt], axis=0).astype(bf16),
                        rhs=jnp.concatenate([kh, bh], axis=0).astype(bf16),
                        kbbb=jnp.concatenate([kb, bb], axis=0).astype(bf16)))
        for ch in chains:
            ch["a"] = _dot_nt(ch["lhs"], ch["rhs"])
        for ch in chains:
            top = jnp.where(strict2[ch["d"]], ch["a"][:CHUNK], 0.0)
            ch["bot"] = jnp.where(incl2[ch["d"]], ch["a"][CHUNK:], 0.0).astype(bf16)
            ch["akk"] = top[:, :CHUNK].astype(bf16)
            akb = top[:, CHUNK:]
            ch["ab"] = akb.astype(bf16)
            ch["t"] = eye_f - jnp.where(levels[0], akb, 0.0)
        for m in levels[1:]:
            for ch in chains:
                ch["tb"] = ch["t"].astype(bf16)
                off = jnp.where(m, ch["ab"], jnp.zeros_like(ch["ab"]))
                ch["p"] = _dot(off, ch["tb"]).astype(bf16)
            for ch in chains:
                ch["t"] = ch["t"] - _dot(ch["tb"], ch["p"])
        for ch in chains:
            ch["w0v"] = _dot(ch["akk"], ch["vv"].astype(bf16))
        for ch in chains:
            ch["x"] = _dot(ch["t"].astype(bf16),
                           jnp.concatenate([ch["w0v"], ch["kt"]], axis=1).astype(bf16))
        for ch in chains:
            ch["z"] = jnp.concatenate([jnp.concatenate([ch["vv"], zblk], axis=1), -ch["x"]],
                                      axis=0).astype(bf16)
            ch["o"] = _dot(ch["bot"], ch["z"])
        for ch in chains:
            ch["tn"] = _dot_tn(ch["z"], ch["kbbb"])
        for ch in chains:
            idx, rs = ch["idx"], ch["rs"]
            y0_s[idx, rs, :] = ch["o"][:, :A_HEAD]
            rp_s[idx, rs, :] = ch["rt"] + ch["o"][:, A_HEAD:]
            nt_s[idx, rs, :] = ch["tn"][:A_HEAD]
            mt_s[idx, rs, :] = jnp.where(eye, ch["e_tot"], 0.0) + ch["tn"][A_HEAD:]

    def prepass_body(i, carry):
        prepass_group([i * unroll + u for u in range(unroll)])
        return carry

    lax.fori_loop(0, nch // unroll, prepass_body, 0)

    for d in range(2):
        for hh in range(2):
            st_s[d * 2 + hh] = s0[d, hh] if has_s0 else jnp.zeros((A_HEAD, A_HEAD), f32)

    def seq_body(i, carry):
        steps = []
        for d in range(2):
            c = i if d == 0 else nch - 1 - i
            rs = pl.ds(pl.multiple_of(c * CHUNK, CHUNK), CHUNK)
            for hh in range(2):
                steps.append((d, hh, d * 2 + hh, rs, st_s[d * 2 + hh].astype(bf16)))
        for d, hh, idx, rs, sb in steps:
            st_s[idx] = _dot(sb, mt_s[idx, rs, :].astype(bf16)) + nt_s[idx, rs, :]
        for d, hh, idx, rs, sb in steps:
            y_s[d, rs, hh * A_HEAD:(hh + 1) * A_HEAD] = (
                y0_s[idx, rs, :] + _dot_nt(rp_s[idx, rs, :].astype(bf16), sb))
        return carry

    lax.fori_loop(0, nch, seq_body, 0)

    if write_s:
        for d in range(2):
            for hh in range(2):
                sout[d, hh] = st_s[d * 2 + hh]

    y = y_s[0] + y_s[1]
    mean = _head_sum(y, bd) * (1.0 / A_HEAD)
    yc = y - mean
    var = _head_sum(yc * yc, bd) * (1.0 / A_HEAD)
    yn = yc * lax.rsqrt(var + GN_EPS) * lng[...] + lnb[...]
    kmean = 0.5 * (kd_s[0] + kd_s[1])
    bonus = _head_sum(r_s[...] * kmean * rkw[...], bd) * v_s[...]
    mix[:, :LANE] = (yn + bonus) * _silu(ga[...])
    u = cg[...] * xi[...]
    conv = prev(u) * cw[0:1, :] + u * cw[1:2, :] + nxt(u) * cw[2:3, :] + cb[...]
    mix[:, LANE:] = bg[...] * conv * _silu(gb[...])


def _mixer(proj, prm, s0, *, L, n_seq, row0, write_s):
    rb0 = row0 // L
    has_s0 = s0 is not None
    nhp = A_WIDTH // LANE
    col = lambda base: (lambda s, j: (rb0 + s, base + j))
    vec = lambda base: (lambda s, j: (0, base + j))
    in_specs = [
        pl.BlockSpec((L, LANE), col(0)),
        pl.BlockSpec((L, LANE), col(nhp)),
        pl.BlockSpec((L, LANE), col(2 * nhp)),
        pl.BlockSpec((L, 2 * LANE), lambda s, j: (rb0 + s, 3 * A_WIDTH // (2 * LANE))),
        pl.BlockSpec((L, LANE), col(A_SHIFT // LANE)),
        pl.BlockSpec((L, LANE), col((A_SHIFT + A_WIDTH) // LANE)),
        pl.BlockSpec((L, LANE), col((A_SHIFT + A_WIDTH + B_WIDTH) // LANE)),
        pl.BlockSpec((L, LANE), col((A_SHIFT + A_WIDTH + 2 * B_WIDTH) // LANE)),
        pl.BlockSpec((L, LANE), col((A_SHIFT + A_WIDTH + 3 * B_WIDTH) // LANE)),
        pl.BlockSpec((1, LANE), vec(0)),
        pl.BlockSpec((1, LANE), vec(nhp)),
        pl.BlockSpec((1, LANE), vec(2 * nhp)),
        pl.BlockSpec((1, 2 * LANE), lambda s, j: (0, 3 * A_WIDTH // (2 * LANE))),
        pl.BlockSpec((2, LORA, LANE), lambda s, j: (0, 0, j)),
        pl.BlockSpec((2, 1, LANE), lambda s, j: (0, 0, j)),
        pl.BlockSpec((2, LORA, LANE), lambda s, j: (0, 0, j)),
        pl.BlockSpec((2, 1, LANE), lambda s, j: (0, 0, j)),
        pl.BlockSpec((1, LANE), vec(0)),
        pl.BlockSpec((1, LANE), vec(0)),
        pl.BlockSpec((1, LANE), vec(0)),
        pl.BlockSpec((1, LANE), vec(0)),
        pl.BlockSpec((1, LANE), vec(0)),
        pl.BlockSpec((3, LANE), vec(0)),
        pl.BlockSpec((1, LANE), vec(0)),
    ]
    args = [proj] * 9 + [prm["mu"]] * 4 + [prm["lw2"], prm["w0"], prm["la2"], prm["a0"], prm["k_k"],
                                            prm["k_a"], prm["r_k"], prm["lnx_g"], prm["lnx_b"],
                                            prm["conv_w"], prm["conv_b"]]
    state_spec = pl.BlockSpec((None, 2, 2, A_HEAD, A_HEAD), lambda s, j: (s, 0, j, 0, 0))
    if has_s0:
        in_specs.append(pl.BlockSpec((None, None, 2, 2, A_HEAD, A_HEAD),
                                     lambda s, j: (s, prm["pair"], 0, j, 0, 0)))
        args.append(s0)
    out_specs = [pl.BlockSpec((L, 2 * LANE), lambda s, j: (s, j))]
    out_shape = [jax.ShapeDtypeStruct((n_seq * L, D_MODEL), f32)]
    if write_s:
        out_specs.append(state_spec)
        out_shape.append(jax.ShapeDtypeStruct((n_seq, 2, A_HEADS, A_HEAD, A_HEAD), f32))
    tok = lambda w=LANE: pltpu.VMEM((L, w), f32)
    scratch = [tok(), tok(), tok(),
               pltpu.VMEM((2, L, LANE), f32), pltpu.VMEM((2, L, LANE), f32),
               pltpu.VMEM((2, L, LANE), f32), pltpu.VMEM((2, L, LANE), f32),
               pltpu.VMEM((2, L, LANE), f32),
               pltpu.VMEM((4, L, A_HEAD), f32), pltpu.VMEM((4, L, A_HEAD), f32),
               pltpu.VMEM((4, L, A_HEAD), f32), pltpu.VMEM((4, L, A_HEAD), f32),
               pltpu.VMEM((4, A_HEAD, A_HEAD), f32)]
    out = pl.pallas_call(
        functools.partial(_mixer_kernel, L=L, has_s0=has_s0, write_s=write_s,
                          unroll=min(4, L // CHUNK)),
        grid=(n_seq, nhp),
        in_specs=in_specs,
        out_specs=out_specs,
        out_shape=out_shape,
        scratch_shapes=scratch,
        compiler_params=pltpu.CompilerParams(vmem_limit_bytes=VMEM_LIMIT),
        name="rwkv_conv_mixer_L%d" % L,
    )(*args)
    return out


def _attn_kernel(*refs, LQ, LN, PAST, rope, write_cache):
    it = iter(refs)
    q_ref, k_ref, v_ref, g_ref, qg_ref, kg_ref = (next(it) for _ in range(6))
    if rope:
        cosq, sinq, cosk, sink = (next(it) for _ in range(4))
    if PAST:
        ck, cv = next(it), next(it)
    out_ref = next(it)
    if write_cache:
        nk_ref, nv_ref = next(it), next(it)
    kbuf, vbuf, obuf = it

    bd = _head_block_diag()
    lane = lax.broadcasted_iota(jnp.int32, (1, LANE), 1)
    first_half = (lane & 31) < 16

    def rms(x, g):
        return x * lax.rsqrt(_head_sum(x * x, bd) * (1.0 / C_HEAD) + EPS) * g

    def rotate(x, cos, sin):
        outs = []
        for i in range(x.shape[1] // LANE):
            xb = x[:, i * LANE:(i + 1) * LANE]
            sw = jnp.where(first_half, pltpu.roll(xb, LANE - 16, 1), pltpu.roll(xb, 16, 1))
            outs.append(xb * cos + sw * sin)
        return jnp.concatenate(outs, axis=1)

    @pl.when(pl.program_id(1) == 0)
    def _():
        kn = rms(k_ref[...], kg_ref[...])
        vx = v_ref[...]
        if write_cache:
            for h in range(C_KV_HEADS):
                nk_ref[h] = kn[:, h * C_HEAD:(h + 1) * C_HEAD]
                nv_ref[h] = vx[:, h * C_HEAD:(h + 1) * C_HEAD]
        if rope:
            kn = rotate(kn, cosk[...], sink[...])
        for h in range(C_KV_HEADS):
            if PAST:
                kbuf[h, :PAST, :] = ck[h].astype(bf16)
                vbuf[h, :PAST, :] = cv[h].astype(bf16)
            kbuf[h, PAST:, :] = kn[:, h * C_HEAD:(h + 1) * C_HEAD].astype(bf16)
            vbuf[h, PAST:, :] = vx[:, h * C_HEAD:(h + 1) * C_HEAD].astype(bf16)

    qn = rms(q_ref[...], qg_ref[...])
    if rope:
        qn = rotate(qn, cosq[...], sinq[...])
    qn = (qn * (C_HEAD ** -0.5)).astype(bf16)
    for h in range(C_KV_HEADS):
        kh, vh = kbuf[h], vbuf[h]
        for g in range(C_GROUP):
            sl = slice((h * C_GROUP + g) * C_HEAD, (h * C_GROUP + g + 1) * C_HEAD)
            s = _dot_nt(qn[:, sl], kh)
            p = jnp.exp(s - jnp.max(s, axis=-1, keepdims=True))
            o = _dot(p.astype(bf16), vh) / jnp.sum(p, axis=-1, keepdims=True)
            obuf[:, sl] = o
    out_ref[...] = obuf[...] * _silu(g_ref[...])


def _rope_tables():
    pos = np.arange(DEC_SEQ)
    inv = ROPE_THETA ** (-np.arange(0, ROPE_HALF, 2, dtype=np.float32) / ROPE_HALF)
    ang_r = (pos // GRID_W).astype(np.float32)[:, None] * inv
    ang_c = (pos % GRID_W).astype(np.float32)[:, None] * inv
    return ang_r, ang_c


def _attention(proj, prm, cache_k, cache_v, tables, *, n_seq, L, row0, sample):
    LQ = ROW_BLK
    nq = L // LQ
    qb0, kb0 = row0 // LQ, row0 // L
    past = PAST_LEN if sample else 0
    in_specs = [
        pl.BlockSpec((LQ, D_MODEL), lambda b, i: (qb0 + b * nq + i, 0)),
        pl.BlockSpec((L, KV_WIDTH), lambda b, i: (kb0 + b, 2 * D_MODEL // KV_WIDTH)),
        pl.BlockSpec((L, KV_WIDTH), lambda b, i: (kb0 + b, 2 * D_MODEL // KV_WIDTH + 1)),
        pl.BlockSpec((LQ, D_MODEL), lambda b, i: (qb0 + b * nq + i, 1)),
        pl.BlockSpec((1, D_MODEL), lambda b, i: (0, 0)),
        pl.BlockSpec((1, KV_WIDTH), lambda b, i: (0, 0)),
    ]
    args = [proj, proj, proj, proj, prm["q_g"], prm["k_g"]]
    if sample:
        cos, sin = tables
        in_specs += [pl.BlockSpec((LQ, LANE), lambda b, i: (i, 0)),
                     pl.BlockSpec((LQ, LANE), lambda b, i: (i, 0)),
                     pl.BlockSpec((L, LANE), lambda b, i: (0, 0)),
                     pl.BlockSpec((L, LANE), lambda b, i: (0, 0))]
        args += [cos, sin, cos, sin]
        cspec = pl.BlockSpec((None, None, C_KV_HEADS, PAST_LEN, C_HEAD),
                             lambda b, i: (b, prm["pair"], 0, 0, 0))
        in_specs += [cspec, cspec]
        args += [cache_k, cache_v]
    out_specs = [pl.BlockSpec((LQ, D_MODEL), lambda b, i: (b * nq + i, 0))]
    out_shape = [jax.ShapeDtypeStruct((n_seq * L, D_MODEL), f32)]
    if not sample:
        nspec = pl.BlockSpec((None, C_KV_HEADS, L, C_HEAD), lambda b, i: (b, 0, 0, 0))
        out_specs += [nspec, nspec]
        out_shape += [jax.ShapeDtypeStruct((n_seq, C_KV_HEADS, L, C_HEAD), f32)] * 2
    return pl.pallas_call(
        functools.partial(_attn_kernel, LQ=LQ, LN=L, PAST=past, rope=sample, write_cache=not sample),
        grid=(n_seq, nq),
        in_specs=in_specs,
        out_specs=out_specs,
        out_shape=out_shape,
        scratch_shapes=[pltpu.VMEM((C_KV_HEADS, past + L, C_HEAD), bf16),
                        pltpu.VMEM((C_KV_HEADS, past + L, C_HEAD), bf16),
                        pltpu.VMEM((LQ, D_MODEL), f32)],
        compiler_params=pltpu.CompilerParams(vmem_limit_bytes=VMEM_LIMIT),
        name="gqa_sample" if sample else "gqa_prompt",
    )(*args)


def kernel(x_prompt, x_sample, c, state_rwkv, cache_k, cache_v, c_ctx, w_ada, b_ada, norm_g, final_g,
           w_in_e, mu_shift, lora_w2, w0, lora_a2, a0, k_k, k_a, r_k, lnx_g, lnx_b, conv_w, conv_b,
           w_out_e, w_in_o, q_norm_g, k_norm_g, w_out_o):
    x = jnp.concatenate([x_prompt.reshape(TOK_P, D_MODEL), x_sample.reshape(TOK_S, D_MODEL)], axis=0)
    cond = jnp.concatenate([c_ctx[None, :], c, jnp.zeros((N_MOD - 1 - DEC_BATCH, D_MODEL), f32)], axis=0)
    mod = _adaln(cond, w_ada, b_ada).reshape(DEPTH, N_MOD, 1, 3 * D_MODEL)

    j = np.arange(D_MODEL) // (2 * LANE)
    w = np.arange(D_MODEL) % (2 * LANE)
    mix_rows = np.where(w < LANE, j * LANE + w, A_WIDTH + j * LANE + (w - LANE))

    ang_r, ang_c = _rope_tables()
    cos64 = np.concatenate([np.cos(ang_r), np.cos(ang_r), np.cos(ang_c), np.cos(ang_c)], axis=1)
    sin64 = np.concatenate([-np.sin(ang_r), np.sin(ang_r), -np.sin(ang_c), np.sin(ang_c)], axis=1)
    tables = (jnp.asarray(np.tile(cos64, (1, 2)), f32), jnp.asarray(np.tile(sin64, (1, 2)), f32))

    final_g2 = final_g.reshape(1, D_MODEL)
    new_rwkv, new_k, new_v = [], [], []
    y_prompt = y_sample = None
    for layer in range(DEPTH):
        p = layer // 2
        mod_l = mod[layer]
        g = norm_g[layer].reshape(1, D_MODEL)
        if layer % 2 == 0:
            proj = _inproj(x, mod_l, g, w_in_e[p].astype(bf16))
            prm = dict(pair=p, mu=mu_shift[p].reshape(1, A_SHIFT), lw2=lora_w2[p],
                       w0=w0[p].reshape(2, 1, A_WIDTH), la2=lora_a2[p], a0=a0[p].reshape(2, 1, A_WIDTH),
                       k_k=k_k[p].reshape(1, A_WIDTH), k_a=k_a[p].reshape(1, A_WIDTH),
                       r_k=r_k[p].reshape(1, A_WIDTH), lnx_g=lnx_g[p].reshape(1, A_WIDTH),
                       lnx_b=lnx_b[p].reshape(1, A_WIDTH), conv_w=conv_w[p],
                       conv_b=conv_b[p].reshape(1, B_WIDTH))
            mix_p, st_p = _mixer(proj, prm, None, L=SEQ, n_seq=BATCH, row0=0, write_s=True)
            (mix_s,) = _mixer(proj, prm, state_rwkv, L=DEC_SEQ, n_seq=DEC_BATCH, row0=TOK_P,
                              write_s=False)
            new_rwkv.append(st_p)
            act = jnp.concatenate([mix_p, mix_s], axis=0)
            w_out = w_out_e[p][mix_rows].astype(bf16)
        else:
            wi = w_in_o[p]
            wi = jnp.concatenate([wi[:, :D_MODEL], wi[:, D_MODEL + 2 * KV_WIDTH:],
                                  wi[:, D_MODEL:D_MODEL + 2 * KV_WIDTH]], axis=1).astype(bf16)
            proj = _inproj(x, mod_l, g, wi)
            prm = dict(pair=p, q_g=jnp.tile(q_norm_g[p], C_Q_HEADS).reshape(1, D_MODEL),
                       k_g=jnp.tile(k_norm_g[p], C_KV_HEADS).reshape(1, KV_WIDTH))
            att_p, nk, nv = _attention(proj, prm, None, None, None, n_seq=BATCH, L=SEQ, row0=0,
                                       sample=False)
            (att_s,) = _attention(proj, prm, cache_k, cache_v, tables, n_seq=DEC_BATCH, L=DEC_SEQ,
                                  row0=TOK_P, sample=True)
            new_k.append(nk)
            new_v.append(nv)
            act = jnp.concatenate([att_p, att_s], axis=0)
            w_out = w_out_o[p].astype(bf16)
        if layer < DEPTH - 1:
            x = _outproj(act, x, mod_l, w_out, final_g2, final=False, row0=0, rows=TOK)
        else:
            y_prompt = _outproj(act, x, mod_l, w_out, final_g2, final=True, row0=0, rows=TOK_P)
            y_sample = _outproj(act, x, mod_l, w_out, final_g2, final=True, row0=TOK_P, rows=TOK_S)
    return (y_prompt.reshape(BATCH, SEQ, D_MODEL), y_sample.reshape(DEC_BATCH, DEC_SEQ, D_MODEL),
            jnp.stack(new_rwkv, axis=1), jnp.stack(new_k, axis=1), jnp.stack(new_v, axis=1))
```

```python
import functools

import numpy as np
import jax
import jax.numpy as jnp
from jax import lax
from jax.experimental import pallas as pl
from jax.experimental.pallas import tpu as pltpu

f32 = jnp.float32
bf16 = jnp.bfloat16
HI = lax.Precision.HIGHEST

D_MODEL = 1024
BATCH = 32
SEQ = 256
DEPTH = 4
DEC_BATCH = 8
DEC_SEQ = 1024
PAST_LEN = 512
GRID_W = 64
EPS = 1e-6
A_WIDTH = 512
A_HEAD = 64
A_HEADS = 8
LORA = 64
GN_EPS = 64e-5
B_WIDTH = 512
C_HEAD = 64
C_Q_HEADS = 16
C_KV_HEADS = 4
C_GROUP = 4
KV_WIDTH = 256
ROPE_THETA = 10000.0
ROPE_HALF = 32
A_SHIFT = 3 * A_WIDTH + 4 * LORA
EVEN_IN = A_SHIFT + A_WIDTH + 4 * B_WIDTH
ODD_IN = 2 * D_MODEL + 2 * KV_WIDTH

STREAM_TOK = BATCH * SEQ
assert STREAM_TOK == DEC_BATCH * DEC_SEQ
ROW_BLK = 256
N_MOD = 16
CHUNK = 64
LANE = 128
VMEM_LIMIT = 48 * 1024 * 1024
N_PAIR = A_WIDTH // LANE
PAIR_COLS = 8 * LANE
N_VEC = 16


def _sigmoid(x):
    return 1.0 / (1.0 + jnp.exp(-x))


def _silu(x):
    return x * _sigmoid(x)


def _dot(a, b, precision=None):
    return jnp.dot(a, b, precision=precision, preferred_element_type=f32)


def _dot_nt(a, b, precision=None):
    return lax.dot_general(a, b, (((1,), (1,)), ((), ())), precision=precision,
                           preferred_element_type=f32)


def _dot_tn(a, b, precision=None):
    return lax.dot_general(a, b, (((0,), (0,)), ((), ())), precision=precision,
                           preferred_element_type=f32)


def _mod_row(sample):
    if sample:
        return lambda i: 1 + i // (DEC_SEQ // ROW_BLK)
    return lambda i: 0


def _head_block_diag():
    ri = lax.broadcasted_iota(jnp.int32, (LANE, LANE), 0) >> 6
    ci = lax.broadcasted_iota(jnp.int32, (LANE, LANE), 1) >> 6
    return (ri == ci).astype(bf16)


def _head_sum(x, bd):
    hi = x.astype(bf16)
    lo = (x - hi.astype(f32)).astype(bf16)
    n = x.shape[1] // LANE
    cols = [_dot(hi[:, i * LANE:(i + 1) * LANE], bd) + _dot(lo[:, i * LANE:(i + 1) * LANE], bd)
            for i in range(n)]
    return cols[0] if n == 1 else jnp.concatenate(cols, axis=1)


def _adaln_kernel(cond_ref, w_ref, b_ref, o_ref):
    o_ref[...] = _dot(_silu(cond_ref[...]), w_ref[...], HI) + b_ref[...]


def _adaln(cond, w_ada, b_ada):
    tn = 768
    return pl.pallas_call(
        _adaln_kernel,
        grid=(DEPTH, 3 * D_MODEL // tn),
        in_specs=[
            pl.BlockSpec((N_MOD, D_MODEL), lambda l, n: (0, 0)),
            pl.BlockSpec((None, D_MODEL, tn), lambda l, n: (l, 0, n)),
            pl.BlockSpec((None, 1, tn), lambda l, n: (l, 0, n)),
        ],
        out_specs=pl.BlockSpec((None, N_MOD, tn), lambda l, n: (l, 0, n)),
        out_shape=jax.ShapeDtypeStruct((DEPTH, N_MOD, 3 * D_MODEL), f32),
        compiler_params=pltpu.CompilerParams(vmem_limit_bytes=VMEM_LIMIT),
        name="adaln",
    )(cond, w_ada, b_ada.reshape(DEPTH, 1, 3 * D_MODEL))


def _inproj_kernel(x_ref, mod_ref, g_ref, w_ref, o_ref):
    x = x_ref[...]
    y = x * lax.rsqrt(jnp.mean(x * x, axis=-1, keepdims=True) + EPS) * g_ref[...]
    m = mod_ref[...]
    h = y * (1.0 + m[:, D_MODEL:2 * D_MODEL]) + m[:, :D_MODEL]
    o_ref[...] = _dot(h.astype(bf16), w_ref[...])


def _inproj(x, mod_l, g, w, *, sample):
    n = w.shape[1]
    mrow = _mod_row(sample)
    return pl.pallas_call(
        _inproj_kernel,
        grid=(STREAM_TOK // ROW_BLK,),
        in_specs=[
            pl.BlockSpec((ROW_BLK, D_MODEL), lambda i: (i, 0)),
            pl.BlockSpec((None, 1, 3 * D_MODEL), lambda i: (mrow(i), 0, 0)),
            pl.BlockSpec((1, D_MODEL), lambda i: (0, 0)),
            pl.BlockSpec((D_MODEL, n), lambda i: (0, 0)),
        ],
        out_specs=pl.BlockSpec((ROW_BLK, n), lambda i: (i, 0)),
        out_shape=jax.ShapeDtypeStruct((STREAM_TOK, n), f32),
        compiler_params=pltpu.CompilerParams(vmem_limit_bytes=VMEM_LIMIT),
        name="inproj",
    )(x, mod_l, g, w)


def _outproj_kernel(a_ref, x_ref, mod_ref, w_ref, fg_ref, o_ref, *, final):
    o = _dot(a_ref[...].astype(bf16), w_ref[...])
    y = x_ref[...] + mod_ref[:, 2 * D_MODEL:] * o
    if final:
        y = y * lax.rsqrt(jnp.mean(y * y, axis=-1, keepdims=True) + EPS) * fg_ref[...]
    o_ref[...] = y


def _outproj(a, x, mod_l, w, fg, *, final, sample):
    mrow = _mod_row(sample)
    return pl.pallas_call(
        functools.partial(_outproj_kernel, final=final),
        grid=(STREAM_TOK // ROW_BLK,),
        in_specs=[
            pl.BlockSpec((ROW_BLK, D_MODEL), lambda i: (i, 0)),
            pl.BlockSpec((ROW_BLK, D_MODEL), lambda i: (i, 0)),
            pl.BlockSpec((None, 1, 3 * D_MODEL), lambda i: (mrow(i), 0, 0)),
            pl.BlockSpec((D_MODEL, D_MODEL), lambda i: (0, 0)),
            pl.BlockSpec((1, D_MODEL), lambda i: (0, 0)),
        ],
        out_specs=pl.BlockSpec((ROW_BLK, D_MODEL), lambda i: (i, 0)),
        out_shape=jax.ShapeDtypeStruct((STREAM_TOK, D_MODEL), f32),
        compiler_params=pltpu.CompilerParams(vmem_limit_bytes=VMEM_LIMIT),
        name="outproj_final" if final else "outproj",
    )(a, x, mod_l, w, fg)


def _even_column_order():
    lane = np.arange(LANE)
    bases = [0, A_WIDTH, 2 * A_WIDTH, A_SHIFT, A_SHIFT + A_WIDTH, A_SHIFT + A_WIDTH + B_WIDTH,
             A_SHIFT + A_WIDTH + 2 * B_WIDTH, A_SHIFT + A_WIDTH + 3 * B_WIDTH]
    cols = [b + j * LANE + lane for j in range(N_PAIR) for b in bases]
    cols.append(3 * A_WIDTH + np.arange(4 * LORA))
    return np.concatenate(cols)


def _mixer_tables(p, mu_shift, lora_w2, w0, lora_a2, a0, k_k, k_a, r_k, lnx_g, lnx_b, conv_w, conv_b):
    pair = lambda v: v.reshape(-1, N_PAIR, LANE).transpose(1, 0, 2)
    mu = mu_shift[p]
    rows = [k_k[p][None], k_a[p][None], r_k[p].reshape(1, A_WIDTH), lnx_g[p][None], lnx_b[p][None],
            conv_b[p][None], conv_w[p], w0[p], a0[p],
            mu[None, :A_WIDTH], mu[None, A_WIDTH:2 * A_WIDTH], mu[None, 2 * A_WIDTH:3 * A_WIDTH]]
    vecs = pair(jnp.concatenate(rows, axis=0))
    assert vecs.shape == (N_PAIR, N_VEC, LANE)
    mats = pair(jnp.concatenate([lora_w2[p], lora_a2[p]], axis=0).reshape(4 * LORA, A_WIDTH))
    mats = mats.reshape(N_PAIR, 4, LORA, LANE)
    return vecs, mats, mu[None, 3 * A_WIDTH:]


V_KK, V_KA, V_RK, V_LNG, V_LNB, V_CB, V_CW, V_W0, V_A0, V_MU = 0, 1, 2, 3, 4, 5, 6, 9, 11, 13


def _scan_masks():
    ri = lax.broadcasted_iota(jnp.int32, (CHUNK, CHUNK), 0)
    ci = lax.broadcasted_iota(jnp.int32, (CHUNK, CHUNK), 1)
    eye = ri == ci
    ri2 = lax.broadcasted_iota(jnp.int32, (CHUNK, 2 * CHUNK), 0)
    ci2 = lax.broadcasted_iota(jnp.int32, (CHUNK, 2 * CHUNK), 1) & (CHUNK - 1)
    strict2 = (ri2 > ci2, ri2 < ci2)
    incl2 = (ri2 >= ci2, ri2 <= ci2)
    levels = []
    for sh in range(6):
        levels.append(((ri >> (sh + 1)) == (ci >> (sh + 1))) & ((ri >> sh) != (ci >> sh)))
    return eye, strict2, incl2, levels


def _mixer_kernel(*refs, L, has_s0, write_s, unroll):
    it = iter(refs)
    x_ref, lo_ref, vec_ref, mat_ref, mulo = (next(it) for _ in range(5))
    s0 = next(it) if has_s0 else None
    mix = next(it)
    sout = next(it) if write_s else None
    r_s, v_s, kap_s, kd_s, b_s, lw_s, g_s, y_s, rp_s, y0_s, mt_s, nt_s, st_s = it

    j = pl.program_id(1)
    vec = vec_ref[j]
    row = lambda i: vec[i:i + 1, :]
    col = lambda i: x_ref[:, i * LANE:(i + 1) * LANE]

    nch = L // CHUNK
    rows = lax.broadcasted_iota(jnp.int32, (L, 1), 0)
    first, last = rows == 0, rows == L - 1

    def prev(f):
        return jnp.where(first, 0.0, pltpu.roll(f, 1, 0))

    def nxt(f):
        return jnp.where(last, 0.0, pltpu.roll(f, L - 1, 0))

    def tshift(f, mu):
        return f + mu * (0.5 * (prev(f) + nxt(f)) - f)

    bd = _head_block_diag()

    r = tshift(col(0), row(V_MU))
    k = tshift(col(1), row(V_MU + 1))
    v = tshift(col(2), row(V_MU + 2))
    lo = tshift(lo_ref[...], mulo[...])
    kk = k * row(V_KK)
    kap = kk / jnp.maximum(jnp.sqrt(_head_sum(kk * kk, bd)), 1e-12)
    r_s[...] = r
    v_s[...] = v
    kap_s[...] = kap
    crow = rows & (CHUNK - 1)
    for d in range(2):
        lwd = lo[:, d * LORA:(d + 1) * LORA]
        lad = lo[:, 2 * LORA + d * LORA:2 * LORA + (d + 1) * LORA]
        wl = row(V_W0 + d) + _dot(jnp.tanh(lwd).astype(bf16), mat_ref[j, d].astype(bf16))
        softplus = jnp.maximum(-wl, 0.0) + jnp.log(1.0 + jnp.exp(-jnp.abs(wl)))
        lw = -jnp.exp(-softplus - 0.5)
        lw_s[d] = lw
        g = lw
        for sh in range(6):
            st = 1 << sh
            if d == 0:
                g = g + jnp.where(crow >= st, pltpu.roll(g, st, 0), 0.0)
            else:
                g = g + jnp.where(crow < CHUNK - st, pltpu.roll(g, L - st, 0), 0.0)
        g_s[d] = g
        a = _sigmoid(row(V_A0 + d) + _dot(lad.astype(bf16), mat_ref[j, 2 + d].astype(bf16)))
        kd_s[d] = k * (1.0 + (a - 1.0) * row(V_KA))
        b_s[d] = a * kap

    eye, strict2, incl2, levels = _scan_masks()
    eye_f = eye.astype(f32)
    zblk = jnp.zeros((CHUNK, A_HEAD), f32)

    def prepass_group(cs):
        chains = []
        for c in cs:
            rs = pl.ds(pl.multiple_of(c * CHUNK, CHUNK), CHUNK)
            r2, v2, kap2 = r_s[rs, :], v_s[rs, :], kap_s[rs, :]
            for d in range(2):
                lw, g = lw_s[d, rs, :], g_s[d, rs, :]
                tot = g[CHUNK - 1:CHUNK, :] if d == 0 else g[0:1, :]
                e_ng = jnp.exp(-g)
                e_rem = jnp.exp(tot - g)
                e_tot = jnp.exp(tot)
                kd2, b2 = kd_s[d, rs, :], b_s[d, rs, :]
                kt2 = kap2 * jnp.exp(g - lw)
                rt2 = r2 * jnp.exp(g)
                kh2, bh2 = kd2 * e_ng, b2 * e_ng
                kb2, bb2 = kd2 * e_rem, b2 * e_rem
                for hh in range(2):
                    sl = slice(hh * A_HEAD, (hh + 1) * A_HEAD)
                    kt, rt, kh, bh, kb, bb, vv = (x[:, sl] for x in (kt2, rt2, kh2, bh2, kb2, bb2, v2))
                    chains.append(dict(
                        d=d, idx=d * 2 + hh, rs=rs, kt=kt, rt=rt, vv=vv, e_tot=e_tot[:, sl],
                        lhs=jnp.concatenate([kt, rt], axis=0).astype(bf16),
                        rhs=jnp.concatenate([kh, bh], axis=0).astype(bf16),
                        kbbb=jnp.concatenate([kb, bb], axis=0).astype(bf16)))
        for ch in chains:
            ch["a"] = _dot_nt(ch["lhs"], ch["rhs"])
        for ch in chains:
            top = jnp.where(strict2[ch["d"]], ch["a"][:CHUNK], 0.0)
            ch["bot"] = jnp.where(incl2[ch["d"]], ch["a"][CHUNK:], 0.0).astype(bf16)
            ch["akk"] = top[:, :CHUNK].astype(bf16)
            akb = top[:, CHUNK:]
            ch["ab"] = akb.astype(bf16)
            ch["t"] = eye_f - jnp.where(levels[0], akb, 0.0)
        for m in levels[1:]:
            for ch in chains:
                ch["tb"] = ch["t"].astype(bf16)
                off = jnp.where(m, ch["ab"], jnp.zeros_like(ch["ab"]))
                ch["p"] = _dot(off, ch["tb"]).astype(bf16)
            for ch in chains:
                ch["t"] = ch["t"] - _dot(ch["tb"], ch["p"])
        for ch in chains:
            ch["w0v"] = _dot(ch["akk"], ch["vv"].astype(bf16))
        for ch in chains:
            ch["x"] = _dot(ch["t"].astype(bf16),
                           jnp.concatenate([ch["w0v"], ch["kt"]], axis=1).astype(bf16))
        for ch in chains:
            ch["z"] = jnp.concatenate([jnp.concatenate([ch["vv"], zblk], axis=1), -ch["x"]],
                                      axis=0).astype(bf16)
            ch["o"] = _dot(ch["bot"], ch["z"])
        for ch in chains:
            ch["tn"] = _dot_tn(ch["z"], ch["kbbb"])
        for ch in chains:
            idx, rs = ch["idx"], ch["rs"]
            y0_s[idx, rs, :] = ch["o"][:, :A_HEAD]
            rp_s[idx, rs, :] = ch["rt"] + ch["o"][:, A_HEAD:]
            nt_s[idx, rs, :] = ch["tn"][:A_HEAD]
            mt_s[idx, rs, :] = jnp.where(eye, ch["e_tot"], 0.0) + ch["tn"][A_HEAD:]

    def prepass_body(i, carry):
        prepass_group([i * unroll + u for u in range(unroll)])
        return carry

    lax.fori_loop(0, nch // unroll, prepass_body, 0)

    for d in range(2):
        for hh in range(2):
            st_s[d * 2 + hh] = s0[d, hh] if has_s0 else jnp.zeros((A_HEAD, A_HEAD), f32)

    def seq_body(i, carry):
        steps = []
        for d in range(2):
            c = i if d == 0 else nch - 1 - i
            rs = pl.ds(pl.multiple_of(c * CHUNK, CHUNK), CHUNK)
            for hh in range(2):
                steps.append((d, hh, d * 2 + hh, rs, st_s[d * 2 + hh].astype(bf16)))
        for d, hh, idx, rs, sb in steps:
            st_s[idx] = _dot(sb, mt_s[idx, rs, :].astype(bf16)) + nt_s[idx, rs, :]
        for d, hh, idx, rs, sb in steps:
            y_s[d, rs, hh * A_HEAD:(hh + 1) * A_HEAD] = (
                y0_s[idx, rs, :] + _dot_nt(rp_s[idx, rs, :].astype(bf16), sb))
        return carry

    lax.fori_loop(0, nch, seq_body, 0)

    if write_s:
        for d in range(2):
            for hh in range(2):
                sout[d, hh] = st_s[d * 2 + hh]

    y = y_s[0] + y_s[1]
    mean = _head_sum(y, bd) * (1.0 / A_HEAD)
    yc = y - mean
    var = _head_sum(yc * yc, bd) * (1.0 / A_HEAD)
    yn = yc * lax.rsqrt(var + GN_EPS) * row(V_LNG) + row(V_LNB)
    kmean = 0.5 * (kd_s[0] + kd_s[1])
    bonus = _head_sum(r_s[...] * kmean * row(V_RK), bd) * v_s[...]
    mix[:, :LANE] = (yn + bonus) * _silu(col(3))
    u = col(5) * col(6)
    conv = prev(u) * row(V_CW) + u * row(V_CW + 1) + nxt(u) * row(V_CW + 2) + row(V_CB)
    mix[:, LANE:] = col(4) * conv * _silu(col(7))


def _mixer(proj, tables, s0, pair, *, L, n_seq, write_s):
    vecs, mats, mulo = tables
    has_s0 = s0 is not None
    in_specs = [
        pl.BlockSpec((L, PAIR_COLS), lambda s, j: (s, j)),
        pl.BlockSpec((L, 4 * LORA), lambda s, j: (s, N_PAIR * PAIR_COLS // (4 * LORA))),
        pl.BlockSpec((N_PAIR, N_VEC, LANE), lambda s, j: (0, 0, 0)),
        pl.BlockSpec((N_PAIR, 4, LORA, LANE), lambda s, j: (0, 0, 0, 0)),
        pl.BlockSpec((1, 4 * LORA), lambda s, j: (0, 0)),
    ]
    args = [proj, proj, vecs, mats, mulo]
    state_spec = pl.BlockSpec((None, 2, 2, A_HEAD, A_HEAD), lambda s, j: (s, 0, j, 0, 0))
    if has_s0:
        in_specs.append(pl.BlockSpec((None, None, 2, 2, A_HEAD, A_HEAD), lambda s, j: (s, pair, 0, j, 0, 0)))
        args.append(s0)
    out_specs = [pl.BlockSpec((L, 2 * LANE), lambda s, j: (s, j))]
    out_shape = [jax.ShapeDtypeStruct((n_seq * L, D_MODEL), f32)]
    if write_s:
        out_specs.append(state_spec)
        out_shape.append(jax.ShapeDtypeStruct((n_seq, 2, A_HEADS, A_HEAD, A_HEAD), f32))
    tok = lambda: pltpu.VMEM((L, LANE), f32)
    tok2 = lambda: pltpu.VMEM((2, L, LANE), f32)
    chain = lambda: pltpu.VMEM((4, L, A_HEAD), f32)
    scratch = [tok(), tok(), tok(), tok2(), tok2(), tok2(), tok2(), tok2(),
               chain(), chain(), chain(), chain(), pltpu.VMEM((4, A_HEAD, A_HEAD), f32)]
    return pl.pallas_call(
        functools.partial(_mixer_kernel, L=L, has_s0=has_s0, write_s=write_s, unroll=min(4, L // CHUNK)),
        grid=(n_seq, N_PAIR),
        in_specs=in_specs,
        out_specs=out_specs,
        out_shape=out_shape,
        scratch_shapes=scratch,
        compiler_params=pltpu.CompilerParams(vmem_limit_bytes=VMEM_LIMIT),
        name="rwkv_conv_mixer_L%d" % L,
    )(*args)


def _attn_kernel(*refs, LQ, LN, PAST, rope, write_cache):
    it = iter(refs)
    q_ref, k_ref, v_ref, g_ref, qg_ref, kg_ref = (next(it) for _ in range(6))
    if rope:
        cosq, sinq, cosk, sink = (next(it) for _ in range(4))
    if PAST:
        ck, cv = next(it), next(it)
    out_ref = next(it)
    if write_cache:
        nk_ref, nv_ref = next(it), next(it)
    kbuf, vbuf, obuf = it

    bd = _head_block_diag()
    lane = lax.broadcasted_iota(jnp.int32, (1, LANE), 1)
    first_half = (lane & 31) < 16

    def rms(x, g):
        return x * lax.rsqrt(_head_sum(x * x, bd) * (1.0 / C_HEAD) + EPS) * g

    def rotate(x, cos, sin):
        outs = []
        for i in range(x.shape[1] // LANE):
            xb = x[:, i * LANE:(i + 1) * LANE]
            sw = jnp.where(first_half, pltpu.roll(xb, LANE - 16, 1), pltpu.roll(xb, 16, 1))
            outs.append(xb * cos + sw * sin)
        return jnp.concatenate(outs, axis=1)

    @pl.when(pl.program_id(1) == 0)
    def _():
        kn = rms(k_ref[...], kg_ref[...])
        vx = v_ref[...]
        if write_cache:
            for h in range(C_KV_HEADS):
                nk_ref[h] = kn[:, h * C_HEAD:(h + 1) * C_HEAD]
                nv_ref[h] = vx[:, h * C_HEAD:(h + 1) * C_HEAD]
        if rope:
            kn = rotate(kn, cosk[...], sink[...])
        for h in range(C_KV_HEADS):
            if PAST:
                kbuf[h, :PAST, :] = ck[h].astype(bf16)
                vbuf[h, :PAST, :] = cv[h].astype(bf16)
            kbuf[h, PAST:, :] = kn[:, h * C_HEAD:(h + 1) * C_HEAD].astype(bf16)
            vbuf[h, PAST:, :] = vx[:, h * C_HEAD:(h + 1) * C_HEAD].astype(bf16)

    qn = rms(q_ref[...], qg_ref[...])
    if rope:
        qn = rotate(qn, cosq[...], sinq[...])
    qn = (qn * (C_HEAD ** -0.5)).astype(bf16)
    for h in range(C_KV_HEADS):
        heads = [slice((h * C_GROUP + g) * C_HEAD, (h * C_GROUP + g + 1) * C_HEAD) for g in range(C_GROUP)]
        s = _dot_nt(jnp.concatenate([qn[:, sl] for sl in heads], axis=0), kbuf[h])
        p = jnp.exp(s - jnp.max(s, axis=-1, keepdims=True))
        o = _dot(p.astype(bf16), vbuf[h]) / jnp.sum(p, axis=-1, keepdims=True)
        for g, sl in enumerate(heads):
            obuf[:, sl] = o[g * LQ:(g + 1) * LQ]
    out_ref[...] = obuf[...] * _silu(g_ref[...])


def _rope_tables():
    pos = np.arange(DEC_SEQ)
    inv = ROPE_THETA ** (-np.arange(0, ROPE_HALF, 2, dtype=np.float32) / ROPE_HALF)
    ang_r = (pos // GRID_W).astype(np.float32)[:, None] * inv
    ang_c = (pos % GRID_W).astype(np.float32)[:, None] * inv
    cos64 = np.concatenate([np.cos(ang_r), np.cos(ang_r), np.cos(ang_c), np.cos(ang_c)], axis=1)
    sin64 = np.concatenate([-np.sin(ang_r), np.sin(ang_r), -np.sin(ang_c), np.sin(ang_c)], axis=1)
    return jnp.asarray(np.tile(cos64, (1, 2)), f32), jnp.asarray(np.tile(sin64, (1, 2)), f32)


def _attention(proj, q_g, k_g, cache_k, cache_v, pair, tables, *, n_seq, L, sample):
    LQ = ROW_BLK
    nq = L // LQ
    past = PAST_LEN if sample else 0
    in_specs = [
        pl.BlockSpec((LQ, D_MODEL), lambda b, i: (b * nq + i, 0)),
        pl.BlockSpec((L, KV_WIDTH), lambda b, i: (b, 2 * D_MODEL // KV_WIDTH)),
        pl.BlockSpec((L, KV_WIDTH), lambda b, i: (b, 2 * D_MODEL // KV_WIDTH + 1)),
        pl.BlockSpec((LQ, D_MODEL), lambda b, i: (b * nq + i, 1)),
        pl.BlockSpec((1, D_MODEL), lambda b, i: (0, 0)),
        pl.BlockSpec((1, KV_WIDTH), lambda b, i: (0, 0)),
    ]
    args = [proj, proj, proj, proj, q_g, k_g]
    if sample:
        cos, sin = tables
        in_specs += [pl.BlockSpec((LQ, LANE), lambda b, i: (i, 0)),
                     pl.BlockSpec((LQ, LANE), lambda b, i: (i, 0)),
                     pl.BlockSpec((L, LANE), lambda b, i: (0, 0)),
                     pl.BlockSpec((L, LANE), lambda b, i: (0, 0))]
        args += [cos, sin, cos, sin]
        cspec = pl.BlockSpec((None, None, C_KV_HEADS, PAST_LEN, C_HEAD), lambda b, i: (b, pair, 0, 0, 0))
        in_specs += [cspec, cspec]
        args += [cache_k, cache_v]
    out_specs = [pl.BlockSpec((LQ, D_MODEL), lambda b, i: (b * nq + i, 0))]
    out_shape = [jax.ShapeDtypeStruct((n_seq * L, D_MODEL), f32)]
    if not sample:
        nspec = pl.BlockSpec((None, C_KV_HEADS, L, C_HEAD), lambda b, i: (b, 0, 0, 0))
        out_specs += [nspec, nspec]
        out_shape += [jax.ShapeDtypeStruct((n_seq, C_KV_HEADS, L, C_HEAD), f32)] * 2
    return pl.pallas_call(
        functools.partial(_attn_kernel, LQ=LQ, LN=L, PAST=past, rope=sample, write_cache=not sample),
        grid=(n_seq, nq),
        in_specs=in_specs,
        out_specs=out_specs,
        out_shape=out_shape,
        scratch_shapes=[pltpu.VMEM((C_KV_HEADS, past + L, C_HEAD), bf16),
                        pltpu.VMEM((C_KV_HEADS, past + L, C_HEAD), bf16),
                        pltpu.VMEM((LQ, D_MODEL), f32)],
        compiler_params=pltpu.CompilerParams(vmem_limit_bytes=VMEM_LIMIT),
        name="gqa_sample" if sample else "gqa_prompt",
    )(*args)


def kernel(x_prompt, x_sample, c, state_rwkv, cache_k, cache_v, c_ctx, w_ada, b_ada, norm_g, final_g,
           w_in_e, mu_shift, lora_w2, w0, lora_a2, a0, k_k, k_a, r_k, lnx_g, lnx_b, conv_w, conv_b,
           w_out_e, w_in_o, q_norm_g, k_norm_g, w_out_o):
    xs = [x_prompt.reshape(STREAM_TOK, D_MODEL), x_sample.reshape(STREAM_TOK, D_MODEL)]
    cond = jnp.concatenate([c_ctx[None, :], c, jnp.zeros((N_MOD - 1 - DEC_BATCH, D_MODEL), f32)], axis=0)
    mod = _adaln(cond, w_ada, b_ada).reshape(DEPTH, N_MOD, 1, 3 * D_MODEL)

    even_cols = _even_column_order()
    jj = np.arange(D_MODEL) // (2 * LANE)
    ww = np.arange(D_MODEL) % (2 * LANE)
    mix_rows = np.where(ww < LANE, jj * LANE + ww, A_WIDTH + jj * LANE + (ww - LANE))
    rope = _rope_tables()
    final_g2 = final_g.reshape(1, D_MODEL)

    new_rwkv, new_k, new_v = [], [], []
    for layer in range(DEPTH):
        p = layer // 2
        mod_l = mod[layer]
        g = norm_g[layer].reshape(1, D_MODEL)
        if layer % 2 == 0:
            w_in = w_in_e[p][:, even_cols].astype(bf16)
            w_out = w_out_e[p][mix_rows].astype(bf16)
            tables = _mixer_tables(p, mu_shift, lora_w2, w0, lora_a2, a0, k_k, k_a, r_k, lnx_g, lnx_b,
                                   conv_w, conv_b)
            proj_p = _inproj(xs[0], mod_l, g, w_in, sample=False)
            proj_s = _inproj(xs[1], mod_l, g, w_in, sample=True)
            act_p, st_p = _mixer(proj_p, tables, None, p, L=SEQ, n_seq=BATCH, write_s=True)
            (act_s,) = _mixer(proj_s, tables, state_rwkv, p, L=DEC_SEQ, n_seq=DEC_BATCH, write_s=False)
            new_rwkv.append(st_p)
        else:
            wi = w_in_o[p]
            w_in = jnp.concatenate([wi[:, :D_MODEL], wi[:, D_MODEL + 2 * KV_WIDTH:],
                                    wi[:, D_MODEL:D_MODEL + 2 * KV_WIDTH]], axis=1).astype(bf16)
            w_out = w_out_o[p].astype(bf16)
            q_g = jnp.tile(q_norm_g[p], C_Q_HEADS).reshape(1, D_MODEL)
            k_g = jnp.tile(k_norm_g[p], C_KV_HEADS).reshape(1, KV_WIDTH)
            proj_p = _inproj(xs[0], mod_l, g, w_in, sample=False)
            proj_s = _inproj(xs[1], mod_l, g, w_in, sample=True)
            act_p, nk, nv = _attention(proj_p, q_g, k_g, None, None, p, None, n_seq=BATCH, L=SEQ,
                                       sample=False)
            (act_s,) = _attention(proj_s, q_g, k_g, cache_k, cache_v, p, rope, n_seq=DEC_BATCH,
                                  L=DEC_SEQ, sample=True)
            new_k.append(nk)
            new_v.append(nv)
        final = layer == DEPTH - 1
        xs = [_outproj(act_p, xs[0], mod_l, w_out, final_g2, final=final, sample=False),
              _outproj(act_s, xs[1], mod_l, w_out, final_g2, final=final, sample=True)]
    return (xs[0].reshape(BATCH, SEQ, D_MODEL), xs[1].reshape(DEC_BATCH, DEC_SEQ, D_MODEL),
            jnp.stack(new_rwkv, axis=1), jnp.stack(new_k, axis=1), jnp.stack(new_v, axis=1))
```

```python
import functools

import numpy as np
import jax
import jax.numpy as jnp
from jax import lax
from jax.experimental import pallas as pl
from jax.experimental.pallas import tpu as pltpu

f32 = jnp.float32
bf16 = jnp.bfloat16
HI = lax.Precision.HIGHEST

D_MODEL = 1024
BATCH = 32
SEQ = 256
DEPTH = 4
DEC_BATCH = 8
DEC_SEQ = 1024
PAST_LEN = 512
GRID_W = 64
EPS = 1e-6
A_WIDTH = 512
A_HEAD = 64
A_HEADS = 8
LORA = 64
GN_EPS = 64e-5
B_WIDTH = 512
C_HEAD = 64
C_Q_HEADS = 16
C_KV_HEADS = 4
C_GROUP = 4
KV_WIDTH = 256
ROPE_THETA = 10000.0
ROPE_HALF = 32
A_SHIFT = 3 * A_WIDTH + 4 * LORA
EVEN_IN = A_SHIFT + A_WIDTH + 4 * B_WIDTH
ODD_IN = 2 * D_MODEL + 2 * KV_WIDTH

STREAM_TOK = BATCH * SEQ
assert STREAM_TOK == DEC_BATCH * DEC_SEQ
ROW_BLK = 256
N_MOD = 16
CHUNK = 64
LANE = 128
VMEM_LIMIT = 48 * 1024 * 1024
N_PAIR = A_WIDTH // LANE
PAIR_COLS = 8 * LANE
N_VEC = 16
MIXER_ROWS = 1024
MIXER_UNROLL = 8


def _sigmoid(x):
    return 1.0 / (1.0 + jnp.exp(-x))


def _silu(x):
    return x * _sigmoid(x)


def _dot(a, b, precision=None):
    return jnp.dot(a, b, precision=precision, preferred_element_type=f32)


def _dot_nt(a, b, precision=None):
    return lax.dot_general(a, b, (((1,), (1,)), ((), ())), precision=precision,
                           preferred_element_type=f32)


def _dot_tn(a, b, precision=None):
    return lax.dot_general(a, b, (((0,), (0,)), ((), ())), precision=precision,
                           preferred_element_type=f32)


def _mod_row(sample):
    if sample:
        return lambda i: 1 + i // (DEC_SEQ // ROW_BLK)
    return lambda i: 0


def _head_block_diag():
    ri = lax.broadcasted_iota(jnp.int32, (LANE, LANE), 0) >> 6
    ci = lax.broadcasted_iota(jnp.int32, (LANE, LANE), 1) >> 6
    return (ri == ci).astype(bf16)


def _head_sum(x, bd, pieces=2):
    n = x.shape[1] // LANE
    hi = x.astype(bf16)
    cols = [_dot(hi[:, i * LANE:(i + 1) * LANE], bd) for i in range(n)]
    if pieces == 2:
        lo = (x - hi.astype(f32)).astype(bf16)
        cols = [c + _dot(lo[:, i * LANE:(i + 1) * LANE], bd) for i, c in enumerate(cols)]
    return cols[0] if n == 1 else jnp.concatenate(cols, axis=1)


def _adaln_kernel(cond_ref, w_ref, b_ref, o_ref):
    o_ref[...] = _dot(_silu(cond_ref[...]), w_ref[...], HI) + b_ref[...]


def _adaln(cond, w_ada, b_ada):
    tn = 768
    return pl.pallas_call(
        _adaln_kernel,
        grid=(DEPTH, 3 * D_MODEL // tn),
        in_specs=[
            pl.BlockSpec((N_MOD, D_MODEL), lambda l, n: (0, 0)),
            pl.BlockSpec((None, D_MODEL, tn), lambda l, n: (l, 0, n)),
            pl.BlockSpec((None, 1, tn), lambda l, n: (l, 0, n)),
        ],
        out_specs=pl.BlockSpec((None, N_MOD, tn), lambda l, n: (l, 0, n)),
        out_shape=jax.ShapeDtypeStruct((DEPTH, N_MOD, 3 * D_MODEL), f32),
        compiler_params=pltpu.CompilerParams(vmem_limit_bytes=VMEM_LIMIT),
        name="adaln",
    )(cond, w_ada, b_ada.reshape(DEPTH, 1, 3 * D_MODEL))


def _inproj_kernel(x_ref, mod_ref, g_ref, w_ref, o_ref):
    x = x_ref[...]
    y = x * lax.rsqrt(jnp.mean(x * x, axis=-1, keepdims=True) + EPS) * g_ref[...]
    m = mod_ref[...]
    h = y * (1.0 + m[:, D_MODEL:2 * D_MODEL]) + m[:, :D_MODEL]
    o_ref[...] = _dot(h.astype(bf16), w_ref[...])


def _inproj(x, mod_l, g, w, *, sample):
    n = w.shape[1]
    mrow = _mod_row(sample)
    return pl.pallas_call(
        _inproj_kernel,
        grid=(STREAM_TOK // ROW_BLK,),
        in_specs=[
            pl.BlockSpec((ROW_BLK, D_MODEL), lambda i: (i, 0)),
            pl.BlockSpec((None, 1, 3 * D_MODEL), lambda i: (mrow(i), 0, 0)),
            pl.BlockSpec((1, D_MODEL), lambda i: (0, 0)),
            pl.BlockSpec((D_MODEL, n), lambda i: (0, 0)),
        ],
        out_specs=pl.BlockSpec((ROW_BLK, n), lambda i: (i, 0)),
        out_shape=jax.ShapeDtypeStruct((STREAM_TOK, n), f32),
        compiler_params=pltpu.CompilerParams(vmem_limit_bytes=VMEM_LIMIT),
        name="inproj",
    )(x, mod_l, g, w)


def _outproj_kernel(a_ref, x_ref, mod_ref, w_ref, fg_ref, o_ref, *, final):
    o = _dot(a_ref[...].astype(bf16), w_ref[...])
    y = x_ref[...] + mod_ref[:, 2 * D_MODEL:] * o
    if final:
        y = y * lax.rsqrt(jnp.mean(y * y, axis=-1, keepdims=True) + EPS) * fg_ref[...]
    o_ref[...] = y


def _outproj(a, x, mod_l, w, fg, *, final, sample):
    mrow = _mod_row(sample)
    return pl.pallas_call(
        functools.partial(_outproj_kernel, final=final),
        grid=(STREAM_TOK // ROW_BLK,),
        in_specs=[
            pl.BlockSpec((ROW_BLK, D_MODEL), lambda i: (i, 0)),
            pl.BlockSpec((ROW_BLK, D_MODEL), lambda i: (i, 0)),
            pl.BlockSpec((None, 1, 3 * D_MODEL), lambda i: (mrow(i), 0, 0)),
            pl.BlockSpec((D_MODEL, D_MODEL), lambda i: (0, 0)),
            pl.BlockSpec((1, D_MODEL), lambda i: (0, 0)),
        ],
        out_specs=pl.BlockSpec((ROW_BLK, D_MODEL), lambda i: (i, 0)),
        out_shape=jax.ShapeDtypeStruct((STREAM_TOK, D_MODEL), f32),
        compiler_params=pltpu.CompilerParams(vmem_limit_bytes=VMEM_LIMIT),
        name="outproj_final" if final else "outproj",
    )(a, x, mod_l, w, fg)


def _even_column_order():
    lane = np.arange(LANE)
    bases = [0, A_WIDTH, 2 * A_WIDTH, A_SHIFT, A_SHIFT + A_WIDTH, A_SHIFT + A_WIDTH + B_WIDTH,
             A_SHIFT + A_WIDTH + 2 * B_WIDTH, A_SHIFT + A_WIDTH + 3 * B_WIDTH]
    cols = [b + j * LANE + lane for j in range(N_PAIR) for b in bases]
    cols.append(3 * A_WIDTH + np.arange(4 * LORA))
    return np.concatenate(cols)


def _mixer_tables(p, mu_shift, lora_w2, w0, lora_a2, a0, k_k, k_a, r_k, lnx_g, lnx_b, conv_w, conv_b):
    pair = lambda v: v.reshape(-1, N_PAIR, LANE).transpose(1, 0, 2)
    mu = mu_shift[p]
    rows = [k_k[p][None], k_a[p][None], r_k[p].reshape(1, A_WIDTH), lnx_g[p][None], lnx_b[p][None],
            conv_b[p][None], conv_w[p], w0[p], a0[p],
            mu[None, :A_WIDTH], mu[None, A_WIDTH:2 * A_WIDTH], mu[None, 2 * A_WIDTH:3 * A_WIDTH]]
    vecs = pair(jnp.concatenate(rows, axis=0))
    assert vecs.shape == (N_PAIR, N_VEC, LANE)
    mats = pair(jnp.concatenate([lora_w2[p], lora_a2[p]], axis=0).reshape(4 * LORA, A_WIDTH))
    mats = mats.reshape(N_PAIR, 4, LORA, LANE)
    return vecs, mats, mu[None, 3 * A_WIDTH:]


V_KK, V_KA, V_RK, V_LNG, V_LNB, V_CB, V_CW, V_W0, V_A0, V_MU = 0, 1, 2, 3, 4, 5, 6, 9, 11, 13


def _scan_masks():
    ri = lax.broadcasted_iota(jnp.int32, (CHUNK, CHUNK), 0)
    ci = lax.broadcasted_iota(jnp.int32, (CHUNK, CHUNK), 1)
    eye = ri == ci
    ri2 = lax.broadcasted_iota(jnp.int32, (CHUNK, 2 * CHUNK), 0)
    ci2 = lax.broadcasted_iota(jnp.int32, (CHUNK, 2 * CHUNK), 1) & (CHUNK - 1)
    strict2 = (ri2 > ci2, ri2 < ci2)
    incl2 = (ri2 >= ci2, ri2 <= ci2)
    levels = []
    for sh in range(6):
        levels.append(((ri >> (sh + 1)) == (ci >> (sh + 1))) & ((ri >> sh) != (ci >> sh)))
    return eye, strict2, incl2, levels


def _mixer_kernel(*refs, L, NB, has_s0, write_s, unroll):
    it = iter(refs)
    x_ref, lo_ref, vec_ref, mat_ref, mulo = (next(it) for _ in range(5))
    s0 = next(it) if has_s0 else None
    mix = next(it)
    sout = next(it) if write_s else None
    r_s, v_s, kap_s, kd_s, b_s, lw_s, g_s, y_s, rp_s, y0_s, mt_s, nt_s, st_s = it

    j = pl.program_id(1)
    vec = vec_ref[j]
    row = lambda i: vec[i:i + 1, :]
    col = lambda i: x_ref[:, i * LANE:(i + 1) * LANE]

    R = NB * L
    nch, nch_seq = R // CHUNK, L // CHUNK
    rows = lax.broadcasted_iota(jnp.int32, (R, 1), 0)
    first, last = (rows & (L - 1)) == 0, (rows & (L - 1)) == L - 1

    def prev(f):
        return jnp.where(first, 0.0, pltpu.roll(f, 1, 0))

    def nxt(f):
        return jnp.where(last, 0.0, pltpu.roll(f, R - 1, 0))

    def tshift(f, mu):
        return f + mu * (0.5 * (prev(f) + nxt(f)) - f)

    bd = _head_block_diag()

    r = tshift(col(0), row(V_MU))
    k = tshift(col(1), row(V_MU + 1))
    v = tshift(col(2), row(V_MU + 2))
    lo = tshift(lo_ref[...], mulo[...])
    kk = k * row(V_KK)
    kap = kk / jnp.maximum(jnp.sqrt(_head_sum(kk * kk, bd)), 1e-12)
    r_s[...] = r
    v_s[...] = v
    kap_s[...] = kap
    crow = rows & (CHUNK - 1)
    for d in range(2):
        lwd = lo[:, d * LORA:(d + 1) * LORA]
        lad = lo[:, 2 * LORA + d * LORA:2 * LORA + (d + 1) * LORA]
        wl = row(V_W0 + d) + _dot(jnp.tanh(lwd).astype(bf16), mat_ref[j, d].astype(bf16))
        softplus = jnp.maximum(-wl, 0.0) + jnp.log(1.0 + jnp.exp(-jnp.abs(wl)))
        lw = -jnp.exp(-softplus - 0.5)
        lw_s[d] = lw
        g = lw
        for sh in range(6):
            st = 1 << sh
            if d == 0:
                g = g + jnp.where(crow >= st, pltpu.roll(g, st, 0), 0.0)
            else:
                g = g + jnp.where(crow < CHUNK - st, pltpu.roll(g, R - st, 0), 0.0)
        g_s[d] = g
        a = _sigmoid(row(V_A0 + d) + _dot(lad.astype(bf16), mat_ref[j, 2 + d].astype(bf16)))
        kd_s[d] = k * (1.0 + (a - 1.0) * row(V_KA))
        b_s[d] = a * kap

    eye, strict2, incl2, levels = _scan_masks()
    eye_f = eye.astype(f32)
    lane = lax.broadcasted_iota(jnp.int32, (1, LANE), 1)
    half = (lane < A_HEAD, lane >= A_HEAD)
    ri = lax.broadcasted_iota(jnp.int32, (LANE, LANE), 0)
    ci = lax.broadcasted_iota(jnp.int32, (LANE, LANE), 1)
    eye_pair = ri == ci

    def prepass_group(cs):
        groups, chains = [], []
        for c in cs:
            rs = pl.ds(pl.multiple_of(c * CHUNK, CHUNK), CHUNK)
            r2, v2, kap2 = r_s[rs, :], v_s[rs, :], kap_s[rs, :]
            for d in range(2):
                lw, g = lw_s[d, rs, :], g_s[d, rs, :]
                tot = g[CHUNK - 1:CHUNK, :] if d == 0 else g[0:1, :]
                e_ng = jnp.exp(-g)
                e_rem = jnp.exp(tot - g)
                kd2, b2 = kd_s[d, rs, :], b_s[d, rs, :]
                kt2 = kap2 * jnp.exp(g - lw)
                rt2 = r2 * jnp.exp(g)
                lhs2 = jnp.concatenate([kt2, rt2], axis=0).astype(bf16)
                grp = dict(
                    d=d, c=c, rs=rs, rt2=rt2, v2=v2, v2b=v2.astype(bf16), e_tot=jnp.exp(tot),
                    kt2sw=pltpu.roll(kt2, A_HEAD, 1),
                    rhs2=jnp.concatenate([kd2 * e_ng, b2 * e_ng], axis=0).astype(bf16),
                    kbbb2=jnp.concatenate([kd2 * e_rem, b2 * e_rem], axis=0).astype(bf16), chains=[])
                groups.append(grp)
                for hh in range(2):
                    ch = dict(grp=grp, d=d, hh=hh, lhs=jnp.where(half[hh], lhs2, jnp.zeros_like(lhs2)))
                    grp["chains"].append(ch)
                    chains.append(ch)
        for ch in chains:
            ch["a"] = _dot_nt(ch["lhs"], ch["grp"]["rhs2"])
        for ch in chains:
            top = jnp.where(strict2[ch["d"]], ch["a"][:CHUNK], 0.0)
            ch["bot"] = jnp.where(incl2[ch["d"]], ch["a"][CHUNK:], 0.0).astype(bf16)
            ch["akk"] = top[:, :CHUNK].astype(bf16)
            akb = top[:, CHUNK:]
            ch["ab"] = akb.astype(bf16)
            ch["t"] = eye_f - jnp.where(levels[0], akb, 0.0)
        for m in levels[1:]:
            for ch in chains:
                ch["tb"] = ch["t"].astype(bf16)
                off = jnp.where(m, ch["ab"], jnp.zeros_like(ch["ab"]))
                ch["p"] = _dot(off, ch["tb"]).astype(bf16)
            for ch in chains:
                ch["t"] = ch["t"] - _dot(ch["tb"], ch["p"])
        for ch in chains:
            ch["w0v"] = _dot(ch["akk"], ch["grp"]["v2b"])
        for ch in chains:
            rhs = jnp.where(half[ch["hh"]], ch["w0v"], ch["grp"]["kt2sw"])
            ch["x"] = _dot(ch["t"].astype(bf16), rhs.astype(bf16))
        for ch in chains:
            vtop = jnp.where(half[ch["hh"]], ch["grp"]["v2"], 0.0)
            ch["z"] = jnp.concatenate([vtop, -ch["x"]], axis=0).astype(bf16)
            ch["o"] = _dot(ch["bot"], ch["z"])
        for ch in chains:
            ch["tn"] = _dot_tn(ch["z"], ch["grp"]["kbbb2"])
        for grp in groups:
            d, rs = grp["d"], grp["rs"]
            c0, c1 = grp["chains"]
            y0_s[d, rs, :] = jnp.where(half[0], c0["o"], c1["o"])
            rp_s[d, rs, :] = grp["rt2"] + pltpu.roll(jnp.where(half[0], c1["o"], c0["o"]), A_HEAD, 1)
            ps = pl.ds(pl.multiple_of(grp["c"] * LANE, LANE), LANE)
            nt_s[d, ps, :] = jnp.concatenate([jnp.where(half[0], c0["tn"][:A_HEAD], 0.0),
                                              jnp.where(half[1], c1["tn"][A_HEAD:], 0.0)], axis=0)
            mt_s[d, ps, :] = jnp.where(eye_pair, grp["e_tot"], 0.0) + jnp.concatenate(
                [jnp.where(half[0], c0["tn"][A_HEAD:], 0.0), jnp.where(half[1], c1["tn"][:A_HEAD], 0.0)],
                axis=0)

    def prepass_body(i, carry):
        prepass_group([i * unroll + u for u in range(unroll)])
        return carry

    lax.fori_loop(0, nch // unroll, prepass_body, 0)

    for q in range(NB):
        for d in range(2):
            st_s[q, d] = jnp.zeros((LANE, LANE), f32)
            if has_s0:
                for hh in range(2):
                    st_s[q, d, hh * A_HEAD:(hh + 1) * A_HEAD, hh * A_HEAD:(hh + 1) * A_HEAD] = s0[q, d, hh]

    def seq_body(i, carry):
        steps = []
        for q in range(NB):
            for d in range(2):
                c = q * nch_seq + (i if d == 0 else nch_seq - 1 - i)
                rs = pl.ds(pl.multiple_of(c * CHUNK, CHUNK), CHUNK)
                ps = pl.ds(pl.multiple_of(c * LANE, LANE), LANE)
                steps.append((q, d, rs, ps, st_s[q, d].astype(bf16)))
        for q, d, rs, ps, sb in steps:
            st_s[q, d] = _dot(sb, mt_s[d, ps, :].astype(bf16)) + nt_s[d, ps, :]
        for q, d, rs, ps, sb in steps:
            y_s[d, rs, :] = y0_s[d, rs, :] + _dot_nt(rp_s[d, rs, :].astype(bf16), sb)
        return carry

    lax.fori_loop(0, nch_seq, seq_body, 0)

    if write_s:
        for q in range(NB):
            for d in range(2):
                for hh in range(2):
                    sout[q, d, hh] = st_s[q, d, hh * A_HEAD:(hh + 1) * A_HEAD, hh * A_HEAD:(hh + 1) * A_HEAD]

    y = y_s[0] + y_s[1]
    mean = _head_sum(y, bd) * (1.0 / A_HEAD)
    yc = y - mean
    var = _head_sum(yc * yc, bd) * (1.0 / A_HEAD)
    yn = yc * lax.rsqrt(var + GN_EPS) * row(V_LNG) + row(V_LNB)
    kmean = 0.5 * (kd_s[0] + kd_s[1])
    bonus = _head_sum(r_s[...] * kmean * row(V_RK), bd) * v_s[...]
    mix[:, :LANE] = (yn + bonus) * _silu(col(3))
    u = col(5) * col(6)
    conv = prev(u) * row(V_CW) + u * row(V_CW + 1) + nxt(u) * row(V_CW + 2) + row(V_CB)
    mix[:, LANE:] = col(4) * conv * _silu(col(7))


def _mixer(proj, tables, s0, pair, *, L, n_seq, write_s):
    vecs, mats, mulo = tables
    has_s0 = s0 is not None
    R = MIXER_ROWS
    nb = R // L
    in_specs = [
        pl.BlockSpec((R, PAIR_COLS), lambda s, j: (s, j)),
        pl.BlockSpec((R, 4 * LORA), lambda s, j: (s, N_PAIR * PAIR_COLS // (4 * LORA))),
        pl.BlockSpec((N_PAIR, N_VEC, LANE), lambda s, j: (0, 0, 0)),
        pl.BlockSpec((N_PAIR, 4, LORA, LANE), lambda s, j: (0, 0, 0, 0)),
        pl.BlockSpec((1, 4 * LORA), lambda s, j: (0, 0)),
    ]
    args = [proj, proj, vecs, mats, mulo]
    if has_s0:
        in_specs.append(pl.BlockSpec((nb, None, 2, 2, A_HEAD, A_HEAD), lambda s, j: (s, pair, 0, j, 0, 0)))
        args.append(s0)
    out_specs = [pl.BlockSpec((R, 2 * LANE), lambda s, j: (s, j))]
    out_shape = [jax.ShapeDtypeStruct((n_seq * L, D_MODEL), f32)]
    if write_s:
        out_specs.append(pl.BlockSpec((nb, 2, 2, A_HEAD, A_HEAD), lambda s, j: (s, 0, j, 0, 0)))
        out_shape.append(jax.ShapeDtypeStruct((n_seq, 2, A_HEADS, A_HEAD, A_HEAD), f32))
    tok = lambda: pltpu.VMEM((R, LANE), f32)
    tok2 = lambda: pltpu.VMEM((2, R, LANE), f32)
    maps = lambda: pltpu.VMEM((2, 2 * R, LANE), f32)
    scratch = [tok(), tok(), tok(), tok2(), tok2(), tok2(), tok2(), tok2(),
               tok2(), tok2(), maps(), maps(), pltpu.VMEM((nb, 2, LANE, LANE), f32)]
    return pl.pallas_call(
        functools.partial(_mixer_kernel, L=L, NB=nb, has_s0=has_s0, write_s=write_s, unroll=MIXER_UNROLL),
        grid=(n_seq // nb, N_PAIR),
        in_specs=in_specs,
        out_specs=out_specs,
        out_shape=out_shape,
        scratch_shapes=scratch,
        compiler_params=pltpu.CompilerParams(vmem_limit_bytes=VMEM_LIMIT),
        name="rwkv_conv_mixer_L%d" % L,
    )(*args)


def _attn_kernel(*refs, LQ, LN, PAST, rope, write_cache):
    it = iter(refs)
    q_ref, k_ref, v_ref, g_ref, qg_ref, kg_ref = (next(it) for _ in range(6))
    if rope:
        cosq, sinq, cosk, sink = (next(it) for _ in range(4))
    if PAST:
        ck, cv = next(it), next(it)
    out_ref = next(it)
    if write_cache:
        nk_ref, nv_ref = next(it), next(it)
    kbuf, vbuf, obuf = it

    bd = _head_block_diag()
    lane = lax.broadcasted_iota(jnp.int32, (1, LANE), 1)
    first_half = (lane & 31) < 16

    def rms(x, g):
        return x * lax.rsqrt(_head_sum(x * x, bd, pieces=1) * (1.0 / C_HEAD) + EPS) * g

    def rotate(x, cos, sin):
        outs = []
        for i in range(x.shape[1] // LANE):
            xb = x[:, i * LANE:(i + 1) * LANE]
            sw = jnp.where(first_half, pltpu.roll(xb, LANE - 16, 1), pltpu.roll(xb, 16, 1))
            outs.append(xb * cos + sw * sin)
        return jnp.concatenate(outs, axis=1)

    @pl.when(pl.program_id(1) == 0)
    def _():
        kn = rms(k_ref[...], kg_ref[...])
        vx = v_ref[...]
        if write_cache:
            for h in range(C_KV_HEADS):
                nk_ref[h] = kn[:, h * C_HEAD:(h + 1) * C_HEAD]
                nv_ref[h] = vx[:, h * C_HEAD:(h + 1) * C_HEAD]
        if rope:
            kn = rotate(kn, cosk[...], sink[...])
        for h in range(C_KV_HEADS):
            if PAST:
                kbuf[h, :PAST, :] = ck[h].astype(bf16)
                vbuf[h, :PAST, :] = cv[h].astype(bf16)
            kbuf[h, PAST:, :] = kn[:, h * C_HEAD:(h + 1) * C_HEAD].astype(bf16)
            vbuf[h, PAST:, :] = vx[:, h * C_HEAD:(h + 1) * C_HEAD].astype(bf16)

    qn = rms(q_ref[...], qg_ref[...])
    if rope:
        qn = rotate(qn, cosq[...], sinq[...])
    qn = (qn * (C_HEAD ** -0.5)).astype(bf16)
    for h in range(C_KV_HEADS):
        heads = [slice((h * C_GROUP + g) * C_HEAD, (h * C_GROUP + g + 1) * C_HEAD) for g in range(C_GROUP)]
        s = _dot_nt(jnp.concatenate([qn[:, sl] for sl in heads], axis=0), kbuf[h])
        p = jnp.exp(s - jnp.max(s, axis=-1, keepdims=True))
        o = _dot(p.astype(bf16), vbuf[h]) / jnp.sum(p, axis=-1, keepdims=True)
        for g, sl in enumerate(heads):
            obuf[:, sl] = o[g * LQ:(g + 1) * LQ]
    out_ref[...] = obuf[...] * _silu(g_ref[...])


def _rope_tables():
    pos = np.arange(DEC_SEQ)
    inv = ROPE_THETA ** (-np.arange(0, ROPE_HALF, 2, dtype=np.float32) / ROPE_HALF)
    ang_r = (pos // GRID_W).astype(np.float32)[:, None] * inv
    ang_c = (pos % GRID_W).astype(np.float32)[:, None] * inv
    cos64 = np.concatenate([np.cos(ang_r), np.cos(ang_r), np.cos(ang_c), np.cos(ang_c)], axis=1)
    sin64 = np.concatenate([-np.sin(ang_r), np.sin(ang_r), -np.sin(ang_c), np.sin(ang_c)], axis=1)
    return jnp.asarray(np.tile(cos64, (1, 2)), f32), jnp.asarray(np.tile(sin64, (1, 2)), f32)


def _attention(proj, q_g, k_g, cache_k, cache_v, pair, tables, *, n_seq, L, sample):
    LQ = ROW_BLK
    nq = L // LQ
    past = PAST_LEN if sample else 0
    in_specs = [
        pl.BlockSpec((LQ, D_MODEL), lambda b, i: (b * nq + i, 0)),
        pl.BlockSpec((L, KV_WIDTH), lambda b, i: (b, 2 * D_MODEL // KV_WIDTH)),
        pl.BlockSpec((L, KV_WIDTH), lambda b, i: (b, 2 * D_MODEL // KV_WIDTH + 1)),
        pl.BlockSpec((LQ, D_MODEL), lambda b, i: (b * nq + i, 1)),
        pl.BlockSpec((1, D_MODEL), lambda b, i: (0, 0)),
        pl.BlockSpec((1, KV_WIDTH), lambda b, i: (0, 0)),
    ]
    args = [proj, proj, proj, proj, q_g, k_g]
    if sample:
        cos, sin = tables
        in_specs += [pl.BlockSpec((LQ, LANE), lambda b, i: (i, 0)),
                     pl.BlockSpec((LQ, LANE), lambda b, i: (i, 0)),
                     pl.BlockSpec((L, LANE), lambda b, i: (0, 0)),
                     pl.BlockSpec((L, LANE), lambda b, i: (0, 0))]
        args += [cos, sin, cos, sin]
        cspec = pl.BlockSpec((None, None, C_KV_HEADS, PAST_LEN, C_HEAD), lambda b, i: (b, pair, 0, 0, 0))
        in_specs += [cspec, cspec]
        args += [cache_k, cache_v]
    out_specs = [pl.BlockSpec((LQ, D_MODEL), lambda b, i: (b * nq + i, 0))]
    out_shape = [jax.ShapeDtypeStruct((n_seq * L, D_MODEL), f32)]
    if not sample:
        nspec = pl.BlockSpec((None, C_KV_HEADS, L, C_HEAD), lambda b, i: (b, 0, 0, 0))
        out_specs += [nspec, nspec]
        out_shape += [jax.ShapeDtypeStruct((n_seq, C_KV_HEADS, L, C_HEAD), f32)] * 2
    return pl.pallas_call(
        functools.partial(_attn_kernel, LQ=LQ, LN=L, PAST=past, rope=sample, write_cache=not sample),
        grid=(n_seq, nq),
        in_specs=in_specs,
        out_specs=out_specs,
        out_shape=out_shape,
        scratch_shapes=[pltpu.VMEM((C_KV_HEADS, past + L, C_HEAD), bf16),
                        pltpu.VMEM((C_KV_HEADS, past + L, C_HEAD), bf16),
                        pltpu.VMEM((LQ, D_MODEL), f32)],
        compiler_params=pltpu.CompilerParams(vmem_limit_bytes=VMEM_LIMIT),
        name="gqa_sample" if sample else "gqa_prompt",
    )(*args)


def kernel(x_prompt, x_sample, c, state_rwkv, cache_k, cache_v, c_ctx, w_ada, b_ada, norm_g, final_g,
           w_in_e, mu_shift, lora_w2, w0, lora_a2, a0, k_k, k_a, r_k, lnx_g, lnx_b, conv_w, conv_b,
           w_out_e, w_in_o, q_norm_g, k_norm_g, w_out_o):
    xs = [x_prompt.reshape(STREAM_TOK, D_MODEL), x_sample.reshape(STREAM_TOK, D_MODEL)]
    cond = jnp.concatenate([c_ctx[None, :], c, jnp.zeros((N_MOD - 1 - DEC_BATCH, D_MODEL), f32)], axis=0)
    mod = _adaln(cond, w_ada, b_ada).reshape(DEPTH, N_MOD, 1, 3 * D_MODEL)

    even_cols = _even_column_order()
    jj = np.arange(D_MODEL) // (2 * LANE)
    ww = np.arange(D_MODEL) % (2 * LANE)
    mix_rows = np.where(ww < LANE, jj * LANE + ww, A_WIDTH + jj * LANE + (ww - LANE))
    rope = _rope_tables()
    final_g2 = final_g.reshape(1, D_MODEL)

    new_rwkv, new_k, new_v = [], [], []
    for layer in range(DEPTH):
        p = layer // 2
        mod_l = mod[layer]
        g = norm_g[layer].reshape(1, D_MODEL)
        if layer % 2 == 0:
            w_in = w_in_e[p][:, even_cols].astype(bf16)
            w_out = w_out_e[p][mix_rows].astype(bf16)
            tables = _mixer_tables(p, mu_shift, lora_w2, w0, lora_a2, a0, k_k, k_a, r_k, lnx_g, lnx_b,
                                   conv_w, conv_b)
            proj_p = _inproj(xs[0], mod_l, g, w_in, sample=False)
            proj_s = _inproj(xs[1], mod_l, g, w_in, sample=True)
            act_p, st_p = _mixer(proj_p, tables, None, p, L=SEQ, n_seq=BATCH, write_s=True)
            (act_s,) = _mixer(proj_s, tables, state_rwkv, p, L=DEC_SEQ, n_seq=DEC_BATCH, write_s=False)
            new_rwkv.append(st_p)
        else:
            wi = w_in_o[p]
            w_in = jnp.concatenate([wi[:, :D_MODEL], wi[:, D_MODEL + 2 * KV_WIDTH:],
                                    wi[:, D_MODEL:D_MODEL + 2 * KV_WIDTH]], axis=1).astype(bf16)
            w_out = w_out_o[p].astype(bf16)
            q_g = jnp.tile(q_norm_g[p], C_Q_HEADS).reshape(1, D_MODEL)
            k_g = jnp.tile(k_norm_g[p], C_KV_HEADS).reshape(1, KV_WIDTH)
            proj_p = _inproj(xs[0], mod_l, g, w_in, sample=False)
            proj_s = _inproj(xs[1], mod_l, g, w_in, sample=True)
            act_p, nk, nv = _attention(proj_p, q_g, k_g, None, None, p, None, n_seq=BATCH, L=SEQ,
                                       sample=False)
            (act_s,) = _attention(proj_s, q_g, k_g, cache_k, cache_v, p, rope, n_seq=DEC_BATCH,
                                  L=DEC_SEQ, sample=True)
            new_k.append(nk)
            new_v.append(nv)
        final = layer == DEPTH - 1
        xs = [_outproj(act_p, xs[0], mod_l, w_out, final_g2, final=final, sample=False),
              _outproj(act_s, xs[1], mod_l, w_out, final_g2, final=final, sample=True)]
    return (xs[0].reshape(BATCH, SEQ, D_MODEL), xs[1].reshape(DEC_BATCH, DEC_SEQ, D_MODEL),
            jnp.stack(new_rwkv, axis=1), jnp.stack(new_k, axis=1), jnp.stack(new_v, axis=1))
```

```python
import functools

import numpy as np
import jax
import jax.numpy as jnp
from jax import lax
from jax.experimental import pallas as pl
from jax.experimental.pallas import tpu as pltpu

f32 = jnp.float32
bf16 = jnp.bfloat16
HI = lax.Precision.HIGHEST

D_MODEL = 1024
BATCH = 32
SEQ = 256
DEPTH = 4
DEC_BATCH = 8
DEC_SEQ = 1024
PAST_LEN = 512
GRID_W = 64
EPS = 1e-6
A_WIDTH = 512
A_HEAD = 64
A_HEADS = 8
LORA = 64
GN_EPS = 64e-5
B_WIDTH = 512
C_HEAD = 64
C_Q_HEADS = 16
C_KV_HEADS = 4
C_GROUP = 4
KV_WIDTH = 256
ROPE_THETA = 10000.0
ROPE_HALF = 32
A_SHIFT = 3 * A_WIDTH + 4 * LORA
EVEN_IN = A_SHIFT + A_WIDTH + 4 * B_WIDTH
ODD_IN = 2 * D_MODEL + 2 * KV_WIDTH

STREAM_TOK = BATCH * SEQ
assert STREAM_TOK == DEC_BATCH * DEC_SEQ
PROJ_ROWS = 512
ATTN_ROWS = 256
N_MOD = 16
CHUNK = 64
LANE = 128
VMEM_LIMIT = 48 * 1024 * 1024
N_PAIR = A_WIDTH // LANE
PAIR_COLS = 8 * LANE
N_VEC = 16
MIXER_ROWS = 1024
MIXER_UNROLL = 8


def _sigmoid(x):
    return 1.0 / (1.0 + jnp.exp(-x))


def _silu(x):
    return x * _sigmoid(x)


def _dot(a, b, precision=None):
    return jnp.dot(a, b, precision=precision, preferred_element_type=f32)


def _dot_nt(a, b, precision=None):
    return lax.dot_general(a, b, (((1,), (1,)), ((), ())), precision=precision,
                           preferred_element_type=f32)


def _dot_tn(a, b, precision=None):
    return lax.dot_general(a, b, (((0,), (0,)), ((), ())), precision=precision,
                           preferred_element_type=f32)


def _mod_row(sample):
    if sample:
        return lambda i: 1 + i // (DEC_SEQ // PROJ_ROWS)
    return lambda i: 0


def _head_block_diag():
    ri = lax.broadcasted_iota(jnp.int32, (LANE, LANE), 0) >> 6
    ci = lax.broadcasted_iota(jnp.int32, (LANE, LANE), 1) >> 6
    return (ri == ci).astype(bf16)


def _head_sum(x, bd, pieces=2):
    n = x.shape[1] // LANE
    hi = x.astype(bf16)
    cols = [_dot(hi[:, i * LANE:(i + 1) * LANE], bd) for i in range(n)]
    if pieces == 2:
        lo = (x - hi.astype(f32)).astype(bf16)
        cols = [c + _dot(lo[:, i * LANE:(i + 1) * LANE], bd) for i, c in enumerate(cols)]
    return cols[0] if n == 1 else jnp.concatenate(cols, axis=1)


def _adaln_kernel(cond_ref, w_ref, b_ref, o_ref):
    o_ref[...] = _dot(_silu(cond_ref[...]), w_ref[...], HI) + b_ref[...]


def _adaln(cond, w_ada, b_ada):
    tn = 768
    return pl.pallas_call(
        _adaln_kernel,
        grid=(DEPTH, 3 * D_MODEL // tn),
        in_specs=[
            pl.BlockSpec((N_MOD, D_MODEL), lambda l, n: (0, 0)),
            pl.BlockSpec((None, D_MODEL, tn), lambda l, n: (l, 0, n)),
            pl.BlockSpec((None, 1, tn), lambda l, n: (l, 0, n)),
        ],
        out_specs=pl.BlockSpec((None, N_MOD, tn), lambda l, n: (l, 0, n)),
        out_shape=jax.ShapeDtypeStruct((DEPTH, N_MOD, 3 * D_MODEL), f32),
        compiler_params=pltpu.CompilerParams(vmem_limit_bytes=VMEM_LIMIT),
        name="adaln",
    )(cond, w_ada, b_ada.reshape(DEPTH, 1, 3 * D_MODEL))


def _proj_kernel(*refs, has_out, has_in, final):
    it = iter(refs)
    if has_out:
        a_ref, x_ref, modp_ref, wo_ref = (next(it) for _ in range(4))
        fg_ref = next(it) if final else None
    else:
        x_ref = next(it)
    if has_in:
        modn_ref, g_ref, wi_ref = (next(it) for _ in range(3))
    x = x_ref[...]
    if has_out:
        x = x + modp_ref[:, 2 * D_MODEL:] * _dot(a_ref[...].astype(bf16), wo_ref[...])
        xo_ref = next(it)
        if final:
            xo_ref[...] = x * lax.rsqrt(jnp.mean(x * x, axis=-1, keepdims=True) + EPS) * fg_ref[...]
        else:
            xo_ref[...] = x
    if has_in:
        p_ref = next(it)
        y = x * lax.rsqrt(jnp.mean(x * x, axis=-1, keepdims=True) + EPS) * g_ref[...]
        m = modn_ref[...]
        h = y * (1.0 + m[:, D_MODEL:2 * D_MODEL]) + m[:, :D_MODEL]
        p_ref[...] = _dot(h.astype(bf16), wi_ref[...])


def _proj(x, *, sample, out=None, inp=None, final_g=None):
    mrow = _mod_row(sample)
    tm = PROJ_ROWS
    rows = lambda w: pl.BlockSpec((tm, w), lambda i: (i, 0))
    const = lambda shape: pl.BlockSpec(shape, lambda i: (0,) * len(shape), pipeline_mode=pl.Buffered(1))
    modspec = pl.BlockSpec((None, 1, 3 * D_MODEL), lambda i: (mrow(i), 0, 0))
    in_specs, args, out_specs, out_shape = [], [], [], []
    if out is not None:
        act, mod_prev, w_out = out
        in_specs += [rows(D_MODEL), rows(D_MODEL), modspec, const((D_MODEL, D_MODEL))]
        args += [act, x, mod_prev, w_out]
        if final_g is not None:
            in_specs.append(const((1, D_MODEL)))
            args.append(final_g)
        out_specs.append(rows(D_MODEL))
        out_shape.append(jax.ShapeDtypeStruct((STREAM_TOK, D_MODEL), f32))
    else:
        in_specs.append(rows(D_MODEL))
        args.append(x)
    if inp is not None:
        mod_next, g, w_in = inp
        n = w_in.shape[1]
        in_specs += [modspec, const((1, D_MODEL)), const((D_MODEL, n))]
        args += [mod_next, g, w_in]
        out_specs.append(rows(n))
        out_shape.append(jax.ShapeDtypeStruct((STREAM_TOK, n), f32))
    return pl.pallas_call(
        functools.partial(_proj_kernel, has_out=out is not None, has_in=inp is not None,
                          final=final_g is not None),
        grid=(STREAM_TOK // tm,),
        in_specs=in_specs,
        out_specs=out_specs,
        out_shape=out_shape,
        compiler_params=pltpu.CompilerParams(vmem_limit_bytes=VMEM_LIMIT),
        name="proj_%s%s" % ("o" if out is not None else "", "i" if inp is not None else ""),
    )(*args)


def _even_column_starts():
    bases = [0, A_WIDTH, 2 * A_WIDTH, A_SHIFT, A_SHIFT + A_WIDTH, A_SHIFT + A_WIDTH + B_WIDTH,
             A_SHIFT + A_WIDTH + 2 * B_WIDTH, A_SHIFT + A_WIDTH + 3 * B_WIDTH]
    return [(b + j * LANE, LANE) for j in range(N_PAIR) for b in bases] + [(3 * A_WIDTH, 4 * LORA)]


def _even_column_order():
    return np.concatenate([s + np.arange(w) for s, w in _even_column_starts()])


def _even_in_weight(w):
    return jnp.concatenate([w[:, s:s + n] for s, n in _even_column_starts()], axis=1).astype(bf16)


def _mixer_tables(p, mu_shift, lora_w2, w0, lora_a2, a0, k_k, k_a, r_k, lnx_g, lnx_b, conv_w, conv_b):
    pair = lambda v: v.reshape(-1, N_PAIR, LANE).transpose(1, 0, 2)
    mu = mu_shift[p]
    rows = [k_k[p][None], k_a[p][None], r_k[p].reshape(1, A_WIDTH), lnx_g[p][None], lnx_b[p][None],
            conv_b[p][None], conv_w[p], w0[p], a0[p],
            mu[None, :A_WIDTH], mu[None, A_WIDTH:2 * A_WIDTH], mu[None, 2 * A_WIDTH:3 * A_WIDTH]]
    vecs = pair(jnp.concatenate(rows, axis=0))
    assert vecs.shape == (N_PAIR, N_VEC, LANE)
    mats = pair(jnp.concatenate([lora_w2[p], lora_a2[p]], axis=0).reshape(4 * LORA, A_WIDTH))
    mats = mats.reshape(N_PAIR, 4, LORA, LANE)
    return vecs, mats, mu[None, 3 * A_WIDTH:]


V_KK, V_KA, V_RK, V_LNG, V_LNB, V_CB, V_CW, V_W0, V_A0, V_MU = 0, 1, 2, 3, 4, 5, 6, 9, 11, 13


def _scan_masks():
    ri = lax.broadcasted_iota(jnp.int32, (CHUNK, CHUNK), 0)
    ci = lax.broadcasted_iota(jnp.int32, (CHUNK, CHUNK), 1)
    eye = ri == ci
    ri2 = lax.broadcasted_iota(jnp.int32, (CHUNK, 2 * CHUNK), 0)
    ci2 = lax.broadcasted_iota(jnp.int32, (CHUNK, 2 * CHUNK), 1) & (CHUNK - 1)
    strict2 = (ri2 > ci2, ri2 < ci2)
    incl2 = (ri2 >= ci2, ri2 <= ci2)
    levels = []
    for sh in range(6):
        levels.append(((ri >> (sh + 1)) == (ci >> (sh + 1))) & ((ri >> sh) != (ci >> sh)))
    return eye, strict2, incl2, levels


def _mixer_kernel(*refs, L, NB, has_s0, write_s, unroll):
    it = iter(refs)
    x_ref, lo_ref, vec_ref, mat_ref, mulo = (next(it) for _ in range(5))
    s0 = next(it) if has_s0 else None
    mix = next(it)
    sout = next(it) if write_s else None
    r_s, v_s, kap_s, kd_s, b_s, lw_s, g_s, y_s, rp_s, y0_s, mt_s, nt_s, st_s = it

    j = pl.program_id(1)
    vec = vec_ref[j]
    row = lambda i: vec[i:i + 1, :]
    col = lambda i: x_ref[:, i * LANE:(i + 1) * LANE]

    R = NB * L
    nch, nch_seq = R // CHUNK, L // CHUNK
    rows = lax.broadcasted_iota(jnp.int32, (R, 1), 0)
    first, last = (rows & (L - 1)) == 0, (rows & (L - 1)) == L - 1

    def prev(f):
        return jnp.where(first, 0.0, pltpu.roll(f, 1, 0))

    def nxt(f):
        return jnp.where(last, 0.0, pltpu.roll(f, R - 1, 0))

    def tshift(f, mu):
        return f + mu * (0.5 * (prev(f) + nxt(f)) - f)

    bd = _head_block_diag()

    r = tshift(col(0), row(V_MU))
    k = tshift(col(1), row(V_MU + 1))
    v = tshift(col(2), row(V_MU + 2))
    lo = tshift(lo_ref[...], mulo[...])
    kk = k * row(V_KK)
    kap = kk / jnp.maximum(jnp.sqrt(_head_sum(kk * kk, bd)), 1e-12)
    r_s[...] = r
    v_s[...] = v
    kap_s[...] = kap
    crow = rows & (CHUNK - 1)
    for d in range(2):
        lwd = lo[:, d * LORA:(d + 1) * LORA]
        lad = lo[:, 2 * LORA + d * LORA:2 * LORA + (d + 1) * LORA]
        wl = row(V_W0 + d) + _dot(jnp.tanh(lwd).astype(bf16), mat_ref[j, d].astype(bf16))
        softplus = jnp.maximum(-wl, 0.0) + jnp.log(1.0 + jnp.exp(-jnp.abs(wl)))
        lw = -jnp.exp(-softplus - 0.5)
        lw_s[d] = lw
        g = lw
        for sh in range(6):
            st = 1 << sh
            if d == 0:
                g = g + jnp.where(crow >= st, pltpu.roll(g, st, 0), 0.0)
            else:
                g = g + jnp.where(crow < CHUNK - st, pltpu.roll(g, R - st, 0), 0.0)
        g_s[d] = g
        a = _sigmoid(row(V_A0 + d) + _dot(lad.astype(bf16), mat_ref[j, 2 + d].astype(bf16)))
        kd_s[d] = k * (1.0 + (a - 1.0) * row(V_KA))
        b_s[d] = a * kap

    eye, strict2, incl2, levels = _scan_masks()
    eye_f = eye.astype(f32)
    lane = lax.broadcasted_iota(jnp.int32, (1, LANE), 1)
    half = (lane < A_HEAD, lane >= A_HEAD)
    ri = lax.broadcasted_iota(jnp.int32, (LANE, LANE), 0)
    ci = lax.broadcasted_iota(jnp.int32, (LANE, LANE), 1)
    eye_pair = ri == ci

    def prepass_group(cs):
        groups, chains = [], []
        for c in cs:
            rs = pl.ds(pl.multiple_of(c * CHUNK, CHUNK), CHUNK)
            r2, v2, kap2 = r_s[rs, :], v_s[rs, :], kap_s[rs, :]
            for d in range(2):
                lw, g = lw_s[d, rs, :], g_s[d, rs, :]
                tot = g[CHUNK - 1:CHUNK, :] if d == 0 else g[0:1, :]
                e_ng = jnp.exp(-g)
                e_rem = jnp.exp(tot - g)
                kd2, b2 = kd_s[d, rs, :], b_s[d, rs, :]
                kt2 = kap2 * jnp.exp(g - lw)
                rt2 = r2 * jnp.exp(g)
                lhs2 = jnp.concatenate([kt2, rt2], axis=0).astype(bf16)
                grp = dict(
                    d=d, c=c, rs=rs, rt2=rt2, v2=v2, v2b=v2.astype(bf16), e_tot=jnp.exp(tot),
                    kt2sw=pltpu.roll(kt2, A_HEAD, 1),
                    rhs2=jnp.concatenate([kd2 * e_ng, b2 * e_ng], axis=0).astype(bf16),
                    kbbb2=jnp.concatenate([kd2 * e_rem, b2 * e_rem], axis=0).astype(bf16), chains=[])
                groups.append(grp)
                for hh in range(2):
                    ch = dict(grp=grp, d=d, hh=hh, lhs=jnp.where(half[hh], lhs2, jnp.zeros_like(lhs2)))
                    grp["chains"].append(ch)
                    chains.append(ch)
        for ch in chains:
            ch["a"] = _dot_nt(ch["lhs"], ch["grp"]["rhs2"])
        for ch in chains:
            top = jnp.where(strict2[ch["d"]], ch["a"][:CHUNK], 0.0)
            ch["bot"] = jnp.where(incl2[ch["d"]], ch["a"][CHUNK:], 0.0).astype(bf16)
            ch["akk"] = top[:, :CHUNK].astype(bf16)
            akb = top[:, CHUNK:]
            ch["ab"] = akb.astype(bf16)
            ch["t"] = eye_f - jnp.where(levels[0], akb, 0.0)
        for m in levels[1:]:
            for ch in chains:
                ch["tb"] = ch["t"].astype(bf16)
                off = jnp.where(m, ch["ab"], jnp.zeros_like(ch["ab"]))
                ch["p"] = _dot(off, ch["tb"]).astype(bf16)
            for ch in chains:
                ch["t"] = ch["t"] - _dot(ch["tb"], ch["p"])
        for ch in chains:
            ch["w0v"] = _dot(ch["akk"], ch["grp"]["v2b"])
        for ch in chains:
            rhs = jnp.where(half[ch["hh"]], ch["w0v"], ch["grp"]["kt2sw"])
            ch["x"] = _dot(ch["t"].astype(bf16), rhs.astype(bf16))
        for ch in chains:
            vtop = jnp.where(half[ch["hh"]], ch["grp"]["v2"], 0.0)
            ch["z"] = jnp.concatenate([vtop, -ch["x"]], axis=0).astype(bf16)
            ch["o"] = _dot(ch["bot"], ch["z"])
        for ch in chains:
            ch["tn"] = _dot_tn(ch["z"], ch["grp"]["kbbb2"])
        for grp in groups:
            d, rs = grp["d"], grp["rs"]
            c0, c1 = grp["chains"]
            y0_s[d, rs, :] = jnp.where(half[0], c0["o"], c1["o"])
            rp_s[d, rs, :] = grp["rt2"] + pltpu.roll(jnp.where(half[0], c1["o"], c0["o"]), A_HEAD, 1)
            ps = pl.ds(pl.multiple_of(grp["c"] * LANE, LANE), LANE)
            nt_s[d, ps, :] = jnp.concatenate([jnp.where(half[0], c0["tn"][:A_HEAD], 0.0),
                                              jnp.where(half[1], c1["tn"][A_HEAD:], 0.0)], axis=0)
            mt_s[d, ps, :] = jnp.where(eye_pair, grp["e_tot"], 0.0) + jnp.concatenate(
                [jnp.where(half[0], c0["tn"][A_HEAD:], 0.0), jnp.where(half[1], c1["tn"][:A_HEAD], 0.0)],
                axis=0)

    def prepass_body(i, carry):
        prepass_group([i * unroll + u for u in range(unroll)])
        return carry

    lax.fori_loop(0, nch // unroll, prepass_body, 0)

    for q in range(NB):
        for d in range(2):
            st_s[q, d] = jnp.zeros((LANE, LANE), f32)
            if has_s0:
                for hh in range(2):
                    st_s[q, d, hh * A_HEAD:(hh + 1) * A_HEAD, hh * A_HEAD:(hh + 1) * A_HEAD] = s0[q, d, hh]

    def seq_body(i, carry):
        steps = []
        for q in range(NB):
            for d in range(2):
                c = q * nch_seq + (i if d == 0 else nch_seq - 1 - i)
                rs = pl.ds(pl.multiple_of(c * CHUNK, CHUNK), CHUNK)
                ps = pl.ds(pl.multiple_of(c * LANE, LANE), LANE)
                steps.append((q, d, rs, ps, st_s[q, d].astype(bf16)))
        for q, d, rs, ps, sb in steps:
            st_s[q, d] = _dot(sb, mt_s[d, ps, :].astype(bf16)) + nt_s[d, ps, :]
        for q, d, rs, ps, sb in steps:
            y_s[d, rs, :] = y0_s[d, rs, :] + _dot_nt(rp_s[d, rs, :].astype(bf16), sb)
        return carry

    lax.fori_loop(0, nch_seq, seq_body, 0)

    if write_s:
        for q in range(NB):
            for d in range(2):
                for hh in range(2):
                    sout[q, d, hh] = st_s[q, d, hh * A_HEAD:(hh + 1) * A_HEAD, hh * A_HEAD:(hh + 1) * A_HEAD]

    y = y_s[0] + y_s[1]
    mean = _head_sum(y, bd) * (1.0 / A_HEAD)
    yc = y - mean
    var = _head_sum(yc * yc, bd) * (1.0 / A_HEAD)
    yn = yc * lax.rsqrt(var + GN_EPS) * row(V_LNG) + row(V_LNB)
    kmean = 0.5 * (kd_s[0] + kd_s[1])
    bonus = _head_sum(r_s[...] * kmean * row(V_RK), bd) * v_s[...]
    mix[:, :LANE] = (yn + bonus) * _silu(col(3))
    u = col(5) * col(6)
    conv = prev(u) * row(V_CW) + u * row(V_CW + 1) + nxt(u) * row(V_CW + 2) + row(V_CB)
    mix[:, LANE:] = col(4) * conv * _silu(col(7))


def _mixer(proj, tables, s0, pair, *, L, n_seq, write_s):
    vecs, mats, mulo = tables
    has_s0 = s0 is not None
    R = MIXER_ROWS
    nb = R // L
    in_specs = [
        pl.BlockSpec((R, PAIR_COLS), lambda s, j: (s, j)),
        pl.BlockSpec((R, 4 * LORA), lambda s, j: (s, N_PAIR * PAIR_COLS // (4 * LORA))),
        pl.BlockSpec((N_PAIR, N_VEC, LANE), lambda s, j: (0, 0, 0)),
        pl.BlockSpec((N_PAIR, 4, LORA, LANE), lambda s, j: (0, 0, 0, 0)),
        pl.BlockSpec((1, 4 * LORA), lambda s, j: (0, 0)),
    ]
    args = [proj, proj, vecs, mats, mulo]
    if has_s0:
        in_specs.append(pl.BlockSpec((nb, None, 2, 2, A_HEAD, A_HEAD), lambda s, j: (s, pair, 0, j, 0, 0)))
        args.append(s0)
    out_specs = [pl.BlockSpec((R, 2 * LANE), lambda s, j: (s, j))]
    out_shape = [jax.ShapeDtypeStruct((n_seq * L, D_MODEL), f32)]
    if write_s:
        out_specs.append(pl.BlockSpec((nb, 2, 2, A_HEAD, A_HEAD), lambda s, j: (s, 0, j, 0, 0)))
        out_shape.append(jax.ShapeDtypeStruct((n_seq, 2, A_HEADS, A_HEAD, A_HEAD), f32))
    tok = lambda: pltpu.VMEM((R, LANE), f32)
    tok2 = lambda: pltpu.VMEM((2, R, LANE), f32)
    maps = lambda: pltpu.VMEM((2, 2 * R, LANE), f32)
    scratch = [tok(), tok(), tok(), tok2(), tok2(), tok2(), tok2(), tok2(),
               tok2(), tok2(), maps(), maps(), pltpu.VMEM((nb, 2, LANE, LANE), f32)]
    return pl.pallas_call(
        functools.partial(_mixer_kernel, L=L, NB=nb, has_s0=has_s0, write_s=write_s, unroll=MIXER_UNROLL),
        grid=(n_seq // nb, N_PAIR),
        in_specs=in_specs,
        out_specs=out_specs,
        out_shape=out_shape,
        scratch_shapes=scratch,
        compiler_params=pltpu.CompilerParams(vmem_limit_bytes=VMEM_LIMIT),
        name="rwkv_conv_mixer_L%d" % L,
    )(*args)


def _attn_kernel(*refs, LQ, LN, PAST, rope, write_cache):
    it = iter(refs)
    q_ref, k_ref, v_ref, g_ref, qg_ref, kg_ref = (next(it) for _ in range(6))
    if rope:
        cosq, sinq, cosk, sink = (next(it) for _ in range(4))
    if PAST:
        ck, cv = next(it), next(it)
    out_ref = next(it)
    if write_cache:
        nk_ref, nv_ref = next(it), next(it)
    kbuf, vbuf, obuf = it

    bd = _head_block_diag()
    lane = lax.broadcasted_iota(jnp.int32, (1, LANE), 1)
    first_half = (lane & 31) < 16

    def rms(x, g):
        return x * lax.rsqrt(_head_sum(x * x, bd, pieces=1) * (1.0 / C_HEAD) + EPS) * g

    def rotate(x, cos, sin):
        outs = []
        for i in range(x.shape[1] // LANE):
            xb = x[:, i * LANE:(i + 1) * LANE]
            sw = jnp.where(first_half, pltpu.roll(xb, LANE - 16, 1), pltpu.roll(xb, 16, 1))
            outs.append(xb * cos + sw * sin)
        return jnp.concatenate(outs, axis=1)

    @pl.when(pl.program_id(1) == 0)
    def _():
        kn = rms(k_ref[...], kg_ref[...])
        vx = v_ref[...]
        if write_cache:
            for h in range(C_KV_HEADS):
                nk_ref[h] = kn[:, h * C_HEAD:(h + 1) * C_HEAD]
                nv_ref[h] = vx[:, h * C_HEAD:(h + 1) * C_HEAD]
        if rope:
            kn = rotate(kn, cosk[...], sink[...])
        for h in range(C_KV_HEADS):
            if PAST:
                kbuf[h, :PAST, :] = ck[h].astype(bf16)
                vbuf[h, :PAST, :] = cv[h].astype(bf16)
            kbuf[h, PAST:, :] = kn[:, h * C_HEAD:(h + 1) * C_HEAD].astype(bf16)
            vbuf[h, PAST:, :] = vx[:, h * C_HEAD:(h + 1) * C_HEAD].astype(bf16)

    qn = rms(q_ref[...], qg_ref[...])
    if rope:
        qn = rotate(qn, cosq[...], sinq[...])
    qn = (qn * (C_HEAD ** -0.5)).astype(bf16)
    for h in range(C_KV_HEADS):
        heads = [slice((h * C_GROUP + g) * C_HEAD, (h * C_GROUP + g + 1) * C_HEAD) for g in range(C_GROUP)]
        s = _dot_nt(jnp.concatenate([qn[:, sl] for sl in heads], axis=0), kbuf[h])
        p = jnp.exp(s - jnp.max(s, axis=-1, keepdims=True))
        o = _dot(p.astype(bf16), vbuf[h]) / jnp.sum(p, axis=-1, keepdims=True)
        for g, sl in enumerate(heads):
            obuf[:, sl] = o[g * LQ:(g + 1) * LQ]
    out_ref[...] = obuf[...] * _silu(g_ref[...])


def _rope_tables():
    pos = np.arange(DEC_SEQ)
    inv = ROPE_THETA ** (-np.arange(0, ROPE_HALF, 2, dtype=np.float32) / ROPE_HALF)
    ang_r = (pos // GRID_W).astype(np.float32)[:, None] * inv
    ang_c = (pos % GRID_W).astype(np.float32)[:, None] * inv
    cos64 = np.concatenate([np.cos(ang_r), np.cos(ang_r), np.cos(ang_c), np.cos(ang_c)], axis=1)
    sin64 = np.concatenate([-np.sin(ang_r), np.sin(ang_r), -np.sin(ang_c), np.sin(ang_c)], axis=1)
    return jnp.asarray(np.tile(cos64, (1, 2)), f32), jnp.asarray(np.tile(sin64, (1, 2)), f32)


def _attention(proj, q_g, k_g, cache_k, cache_v, pair, tables, *, n_seq, L, sample):
    LQ = ATTN_ROWS
    nq = L // LQ
    past = PAST_LEN if sample else 0
    in_specs = [
        pl.BlockSpec((LQ, D_MODEL), lambda b, i: (b * nq + i, 0)),
        pl.BlockSpec((L, KV_WIDTH), lambda b, i: (b, 2 * D_MODEL // KV_WIDTH)),
        pl.BlockSpec((L, KV_WIDTH), lambda b, i: (b, 2 * D_MODEL // KV_WIDTH + 1)),
        pl.BlockSpec((LQ, D_MODEL), lambda b, i: (b * nq + i, 1)),
        pl.BlockSpec((1, D_MODEL), lambda b, i: (0, 0)),
        pl.BlockSpec((1, KV_WIDTH), lambda b, i: (0, 0)),
    ]
    args = [proj, proj, proj, proj, q_g, k_g]
    if sample:
        cos, sin = tables
        in_specs += [pl.BlockSpec((LQ, LANE), lambda b, i: (i, 0)),
                     pl.BlockSpec((LQ, LANE), lambda b, i: (i, 0)),
                     pl.BlockSpec((L, LANE), lambda b, i: (0, 0)),
                     pl.BlockSpec((L, LANE), lambda b, i: (0, 0))]
        args += [cos, sin, cos, sin]
        cspec = pl.BlockSpec((None, None, C_KV_HEADS, PAST_LEN, C_HEAD), lambda b, i: (b, pair, 0, 0, 0))
        in_specs += [cspec, cspec]
        args += [cache_k, cache_v]
    out_specs = [pl.BlockSpec((LQ, D_MODEL), lambda b, i: (b * nq + i, 0))]
    out_shape = [jax.ShapeDtypeStruct((n_seq * L, D_MODEL), f32)]
    if not sample:
        nspec = pl.BlockSpec((None, C_KV_HEADS, L, C_HEAD), lambda b, i: (b, 0, 0, 0))
        out_specs += [nspec, nspec]
        out_shape += [jax.ShapeDtypeStruct((n_seq, C_KV_HEADS, L, C_HEAD), f32)] * 2
    return pl.pallas_call(
        functools.partial(_attn_kernel, LQ=LQ, LN=L, PAST=past, rope=sample, write_cache=not sample),
        grid=(n_seq, nq),
        in_specs=in_specs,
        out_specs=out_specs,
        out_shape=out_shape,
        scratch_shapes=[pltpu.VMEM((C_KV_HEADS, past + L, C_HEAD), bf16),
                        pltpu.VMEM((C_KV_HEADS, past + L, C_HEAD), bf16),
                        pltpu.VMEM((LQ, D_MODEL), f32)],
        compiler_params=pltpu.CompilerParams(vmem_limit_bytes=VMEM_LIMIT),
        name="gqa_sample" if sample else "gqa_prompt",
    )(*args)


def kernel(x_prompt, x_sample, c, state_rwkv, cache_k, cache_v, c_ctx, w_ada, b_ada, norm_g, final_g,
           w_in_e, mu_shift, lora_w2, w0, lora_a2, a0, k_k, k_a, r_k, lnx_g, lnx_b, conv_w, conv_b,
           w_out_e, w_in_o, q_norm_g, k_norm_g, w_out_o):
    xs = [x_prompt.reshape(STREAM_TOK, D_MODEL), x_sample.reshape(STREAM_TOK, D_MODEL)]
    cond = jnp.concatenate([c_ctx[None, :], c, jnp.zeros((N_MOD - 1 - DEC_BATCH, D_MODEL), f32)], axis=0)
    mod = _adaln(cond, w_ada, b_ada).reshape(DEPTH, N_MOD, 1, 3 * D_MODEL)

    rope = _rope_tables()
    final_g2 = final_g.reshape(1, D_MODEL)

    w_ins, w_outs = [], []
    for layer in range(DEPTH):
        p = layer // 2
        if layer % 2 == 0:
            w_ins.append(_even_in_weight(w_in_e[p]))
            wo = w_out_e[p]
            w_outs.append(jnp.concatenate(
                [wo[h * A_WIDTH + j * LANE:h * A_WIDTH + (j + 1) * LANE] for j in range(N_PAIR) for h in range(2)],
                axis=0).astype(bf16))
        else:
            wi = w_in_o[p]
            w_ins.append(jnp.concatenate([wi[:, :D_MODEL], wi[:, D_MODEL + 2 * KV_WIDTH:],
                                          wi[:, D_MODEL:D_MODEL + 2 * KV_WIDTH]], axis=1).astype(bf16))
            w_outs.append(w_out_o[p].astype(bf16))
    inp = lambda layer: (mod[layer], norm_g[layer].reshape(1, D_MODEL), w_ins[layer])

    new_rwkv, new_k, new_v = [], [], []
    (proj_p,) = _proj(xs[0], sample=False, inp=inp(0))
    (proj_s,) = _proj(xs[1], sample=True, inp=inp(0))
    for layer in range(DEPTH):
        p = layer // 2
        if layer % 2 == 0:
            tables = _mixer_tables(p, mu_shift, lora_w2, w0, lora_a2, a0, k_k, k_a, r_k, lnx_g, lnx_b,
                                   conv_w, conv_b)
            act_p, st_p = _mixer(proj_p, tables, None, p, L=SEQ, n_seq=BATCH, write_s=True)
            (act_s,) = _mixer(proj_s, tables, state_rwkv, p, L=DEC_SEQ, n_seq=DEC_BATCH, write_s=False)
            new_rwkv.append(st_p)
        else:
            q_g = jnp.tile(q_norm_g[p], C_Q_HEADS).reshape(1, D_MODEL)
            k_g = jnp.tile(k_norm_g[p], C_KV_HEADS).reshape(1, KV_WIDTH)
            act_p, nk, nv = _attention(proj_p, q_g, k_g, None, None, p, None, n_seq=BATCH, L=SEQ,
                                       sample=False)
            (act_s,) = _attention(proj_s, q_g, k_g, cache_k, cache_v, p, rope, n_seq=DEC_BATCH,
                                  L=DEC_SEQ, sample=True)
            new_k.append(nk)
            new_v.append(nv)
        if layer < DEPTH - 1:
            xs[0], proj_p = _proj(xs[0], sample=False, out=(act_p, mod[layer], w_outs[layer]), inp=inp(layer + 1))
            xs[1], proj_s = _proj(xs[1], sample=True, out=(act_s, mod[layer], w_outs[layer]), inp=inp(layer + 1))
        else:
            (xs[0],) = _proj(xs[0], sample=False, out=(act_p, mod[layer], w_outs[layer]), final_g=final_g2)
            (xs[1],) = _proj(xs[1], sample=True, out=(act_s, mod[layer], w_outs[layer]), final_g=final_g2)
    return (xs[0].reshape(BATCH, SEQ, D_MODEL), xs[1].reshape(DEC_BATCH, DEC_SEQ, D_MODEL),
            jnp.stack(new_rwkv, axis=1), jnp.stack(new_k, axis=1), jnp.stack(new_v, axis=1))
```

```python
import functools

import numpy as np
import jax
import jax.numpy as jnp
from jax import lax
from jax.experimental import pallas as pl
from jax.experimental.pallas import tpu as pltpu

f32 = jnp.float32
bf16 = jnp.bfloat16
HI = lax.Precision.HIGHEST

D_MODEL = 1024
BATCH = 32
SEQ = 256
DEPTH = 4
DEC_BATCH = 8
DEC_SEQ = 1024
PAST_LEN = 512
GRID_W = 64
EPS = 1e-6
A_WIDTH = 512
A_HEAD = 64
A_HEADS = 8
LORA = 64
GN_EPS = 64e-5
B_WIDTH = 512
C_HEAD = 64
C_Q_HEADS = 16
C_KV_HEADS = 4
C_GROUP = 4
KV_WIDTH = 256
ROPE_THETA = 10000.0
ROPE_HALF = 32
A_SHIFT = 3 * A_WIDTH + 4 * LORA
EVEN_IN = A_SHIFT + A_WIDTH + 4 * B_WIDTH
ODD_IN = 2 * D_MODEL + 2 * KV_WIDTH

STREAM_TOK = BATCH * SEQ
assert STREAM_TOK == DEC_BATCH * DEC_SEQ
PROJ_ROWS = 512
ATTN_ROWS = 256
N_MOD = 16
CHUNK = 64
LANE = 128
VMEM_LIMIT = 48 * 1024 * 1024
N_PAIR = A_WIDTH // LANE
PAIR_COLS = 8 * LANE
N_VEC = 16
MIXER_ROWS = 1024


def _sigmoid(x):
    return 1.0 / (1.0 + jnp.exp(-x))


def _silu(x):
    return x * _sigmoid(x)


def _dot(a, b, precision=None):
    return jnp.dot(a, b, precision=precision, preferred_element_type=f32)


def _dot_nt(a, b, precision=None):
    return lax.dot_general(a, b, (((1,), (1,)), ((), ())), precision=precision,
                           preferred_element_type=f32)


def _dot_tn(a, b, precision=None):
    return lax.dot_general(a, b, (((0,), (0,)), ((), ())), precision=precision,
                           preferred_element_type=f32)


def _mod_row(sample):
    if sample:
        return lambda i: 1 + i // (DEC_SEQ // PROJ_ROWS)
    return lambda i: 0


def _head_block_diag():
    ri = lax.broadcasted_iota(jnp.int32, (LANE, LANE), 0) >> 6
    ci = lax.broadcasted_iota(jnp.int32, (LANE, LANE), 1) >> 6
    return (ri == ci).astype(bf16)


def _head_sum(x, bd, pieces=2):
    n = x.shape[1] // LANE
    hi = x.astype(bf16)
    cols = [_dot(hi[:, i * LANE:(i + 1) * LANE], bd) for i in range(n)]
    if pieces == 2:
        lo = (x - hi.astype(f32)).astype(bf16)
        cols = [c + _dot(lo[:, i * LANE:(i + 1) * LANE], bd) for i, c in enumerate(cols)]
    return cols[0] if n == 1 else jnp.concatenate(cols, axis=1)


def _adaln_kernel(cond_ref, w_ref, b_ref, o_ref):
    o_ref[...] = _dot(_silu(cond_ref[...]), w_ref[...], HI) + b_ref[...]


def _adaln(cond, w_ada, b_ada):
    tn = 768
    return pl.pallas_call(
        _adaln_kernel,
        grid=(DEPTH, 3 * D_MODEL // tn),
        in_specs=[
            pl.BlockSpec((N_MOD, D_MODEL), lambda l, n: (0, 0)),
            pl.BlockSpec((None, D_MODEL, tn), lambda l, n: (l, 0, n)),
            pl.BlockSpec((None, 1, tn), lambda l, n: (l, 0, n)),
        ],
        out_specs=pl.BlockSpec((None, N_MOD, tn), lambda l, n: (l, 0, n)),
        out_shape=jax.ShapeDtypeStruct((DEPTH, N_MOD, 3 * D_MODEL), f32),
        compiler_params=pltpu.CompilerParams(vmem_limit_bytes=VMEM_LIMIT),
        name="adaln",
    )(cond, w_ada, b_ada.reshape(DEPTH, 1, 3 * D_MODEL))


def _proj_kernel(*refs, has_out, has_in, final):
    it = iter(refs)
    if has_out:
        a_ref, x_ref, modp_ref, wo_ref = (next(it) for _ in range(4))
        fg_ref = next(it) if final else None
    else:
        x_ref = next(it)
    if has_in:
        modn_ref, g_ref, wi_ref = (next(it) for _ in range(3))
    x = x_ref[...]
    if has_out:
        x = x + modp_ref[:, 2 * D_MODEL:] * _dot(a_ref[...].astype(bf16), wo_ref[...])
        xo_ref = next(it)
        if final:
            xo_ref[...] = x * lax.rsqrt(jnp.mean(x * x, axis=-1, keepdims=True) + EPS) * fg_ref[...]
        else:
            xo_ref[...] = x
    if has_in:
        p_ref = next(it)
        y = x * lax.rsqrt(jnp.mean(x * x, axis=-1, keepdims=True) + EPS) * g_ref[...]
        m = modn_ref[...]
        h = y * (1.0 + m[:, D_MODEL:2 * D_MODEL]) + m[:, :D_MODEL]
        p_ref[...] = _dot(h.astype(bf16), wi_ref[...])


def _proj(x, *, sample, out=None, inp=None, final_g=None):
    mrow = _mod_row(sample)
    tm = PROJ_ROWS
    rows = lambda w: pl.BlockSpec((tm, w), lambda i: (i, 0))
    const = lambda shape: pl.BlockSpec(shape, lambda i: (0,) * len(shape), pipeline_mode=pl.Buffered(1))
    modspec = pl.BlockSpec((None, 1, 3 * D_MODEL), lambda i: (mrow(i), 0, 0))
    in_specs, args, out_specs, out_shape = [], [], [], []
    if out is not None:
        act, mod_prev, w_out = out
        in_specs += [rows(D_MODEL), rows(D_MODEL), modspec, const((D_MODEL, D_MODEL))]
        args += [act, x, mod_prev, w_out]
        if final_g is not None:
            in_specs.append(const((1, D_MODEL)))
            args.append(final_g)
        out_specs.append(rows(D_MODEL))
        out_shape.append(jax.ShapeDtypeStruct((STREAM_TOK, D_MODEL), f32))
    else:
        in_specs.append(rows(D_MODEL))
        args.append(x)
    if inp is not None:
        mod_next, g, w_in = inp
        n = w_in.shape[1]
        in_specs += [modspec, const((1, D_MODEL)), const((D_MODEL, n))]
        args += [mod_next, g, w_in]
        out_specs.append(rows(n))
        out_shape.append(jax.ShapeDtypeStruct((STREAM_TOK, n), f32))
    return pl.pallas_call(
        functools.partial(_proj_kernel, has_out=out is not None, has_in=inp is not None,
                          final=final_g is not None),
        grid=(STREAM_TOK // tm,),
        in_specs=in_specs,
        out_specs=out_specs,
        out_shape=out_shape,
        compiler_params=pltpu.CompilerParams(vmem_limit_bytes=VMEM_LIMIT),
        name="proj_%s%s" % ("o" if out is not None else "", "i" if inp is not None else ""),
    )(*args)


def _even_column_starts():
    bases = [0, A_WIDTH, 2 * A_WIDTH, A_SHIFT, A_SHIFT + A_WIDTH, A_SHIFT + A_WIDTH + B_WIDTH,
             A_SHIFT + A_WIDTH + 2 * B_WIDTH, A_SHIFT + A_WIDTH + 3 * B_WIDTH]
    return [(b + j * LANE, LANE) for j in range(N_PAIR) for b in bases] + [(3 * A_WIDTH, 4 * LORA)]


def _even_column_order():
    return np.concatenate([s + np.arange(w) for s, w in _even_column_starts()])


def _even_in_weight(w):
    return jnp.concatenate([w[:, s:s + n] for s, n in _even_column_starts()], axis=1).astype(bf16)


def _mixer_tables(p, mu_shift, lora_w2, w0, lora_a2, a0, k_k, k_a, r_k, lnx_g, lnx_b, conv_w, conv_b):
    pair = lambda v: v.reshape(-1, N_PAIR, LANE).transpose(1, 0, 2)
    mu = mu_shift[p]
    rows = [k_k[p][None], k_a[p][None], r_k[p].reshape(1, A_WIDTH), lnx_g[p][None], lnx_b[p][None],
            conv_b[p][None], conv_w[p], w0[p], a0[p],
            mu[None, :A_WIDTH], mu[None, A_WIDTH:2 * A_WIDTH], mu[None, 2 * A_WIDTH:3 * A_WIDTH]]
    vecs = pair(jnp.concatenate(rows, axis=0))
    assert vecs.shape == (N_PAIR, N_VEC, LANE)
    mats = pair(jnp.concatenate([lora_w2[p], lora_a2[p]], axis=0).reshape(4 * LORA, A_WIDTH))
    mats = mats.reshape(N_PAIR, 4, LORA, LANE)
    return vecs, mats, mu[None, 3 * A_WIDTH:]


V_KK, V_KA, V_RK, V_LNG, V_LNB, V_CB, V_CW, V_W0, V_A0, V_MU = 0, 1, 2, 3, 4, 5, 6, 9, 11, 13


def _scan_masks():
    ri = lax.broadcasted_iota(jnp.int32, (CHUNK, CHUNK), 0)
    ci = lax.broadcasted_iota(jnp.int32, (CHUNK, CHUNK), 1)
    eye = ri == ci
    ri2 = lax.broadcasted_iota(jnp.int32, (CHUNK, 2 * CHUNK), 0)
    ci2 = lax.broadcasted_iota(jnp.int32, (CHUNK, 2 * CHUNK), 1) & (CHUNK - 1)
    strict2 = (ri2 > ci2, ri2 < ci2)
    incl2 = (ri2 >= ci2, ri2 <= ci2)
    levels = []
    for sh in range(6):
        levels.append(((ri >> (sh + 1)) == (ci >> (sh + 1))) & ((ri >> sh) != (ci >> sh)))
    return eye, strict2, incl2, levels


def _mixer_kernel(*refs, L, NB, has_s0, write_s):
    it = iter(refs)
    x_ref, lo_ref, vec_ref, mat_ref, mulo = (next(it) for _ in range(5))
    s0 = next(it) if has_s0 else None
    mix = next(it)
    sout = next(it) if write_s else None
    r_s, k_s, v_s, lo_s, kd_s, y_s, rp_s, y0_s, mt_s, nt_s, st_s = it

    j = pl.program_id(1)
    vec = vec_ref[j]
    row = lambda i: vec[i:i + 1, :]
    col = lambda i: x_ref[:, i * LANE:(i + 1) * LANE]
    lora_w = [mat_ref[j, i].astype(bf16) for i in range(4)]

    R = NB * L
    nch, nch_seq = R // CHUNK, L // CHUNK
    rows = lax.broadcasted_iota(jnp.int32, (R, 1), 0)
    first, last = (rows & (L - 1)) == 0, (rows & (L - 1)) == L - 1

    def prev(f):
        return jnp.where(first, 0.0, pltpu.roll(f, 1, 0))

    def nxt(f):
        return jnp.where(last, 0.0, pltpu.roll(f, R - 1, 0))

    def tshift(f, mu):
        return f + mu * (0.5 * (prev(f) + nxt(f)) - f)

    bd = _head_block_diag()

    r_s[...] = tshift(col(0), row(V_MU))
    k_s[...] = tshift(col(1), row(V_MU + 1))
    v_s[...] = tshift(col(2), row(V_MU + 2))
    lo_s[...] = tshift(lo_ref[...], mulo[...])
    for q in range(NB):
        for d in range(2):
            st_s[q, d] = jnp.zeros((LANE, LANE), f32)
            if has_s0:
                for hh in range(2):
                    st_s[q, d, hh * A_HEAD:(hh + 1) * A_HEAD, hh * A_HEAD:(hh + 1) * A_HEAD] = s0[q, d, hh]

    eye, strict2, incl2, levels = _scan_masks()
    eye_f = eye.astype(f32)
    lane = lax.broadcasted_iota(jnp.int32, (1, LANE), 1)
    half = (lane < A_HEAD, lane >= A_HEAD)
    ri = lax.broadcasted_iota(jnp.int32, (LANE, LANE), 0)
    ci = lax.broadcasted_iota(jnp.int32, (LANE, LANE), 1)
    eye_pair = ri == ci
    crow = lax.broadcasted_iota(jnp.int32, (CHUNK, 1), 0)

    def setup_dots(c):
        rs = slice(c * CHUNK, (c + 1) * CHUNK)
        k2, lo = k_s[rs, :], lo_s[rs, :]
        kk = k2 * row(V_KK)
        st = dict(c=c, rs=rs, k2=k2, kk=kk, ssq=_head_sum(kk * kk, bd), wl=[], al=[])
        for d in range(2):
            lwd = lo[:, d * LORA:(d + 1) * LORA]
            lad = lo[:, 2 * LORA + d * LORA:2 * LORA + (d + 1) * LORA]
            st["wl"].append(_dot(jnp.tanh(lwd).astype(bf16), lora_w[d]))
            st["al"].append(_dot(lad.astype(bf16), lora_w[2 + d]))
        return st

    def setup_chains(st):
        c, rs, k2 = st["c"], st["rs"], st["k2"]
        r2, v2 = r_s[rs, :], v_s[rs, :]
        kap2 = st["kk"] / jnp.maximum(jnp.sqrt(st["ssq"]), 1e-12)
        groups = []
        for d in range(2):
            wl = row(V_W0 + d) + st["wl"][d]
            softplus = jnp.maximum(-wl, 0.0) + jnp.log(1.0 + jnp.exp(-jnp.abs(wl)))
            lw = -jnp.exp(-softplus - 0.5)
            g = lw
            for sh in range(6):
                stp = 1 << sh
                if d == 0:
                    g = g + jnp.where(crow >= stp, pltpu.roll(g, stp, 0), 0.0)
                else:
                    g = g + jnp.where(crow < CHUNK - stp, pltpu.roll(g, CHUNK - stp, 0), 0.0)
            a = _sigmoid(row(V_A0 + d) + st["al"][d])
            kd2 = k2 * (1.0 + (a - 1.0) * row(V_KA))
            kd_s[d, rs, :] = kd2
            b2 = a * kap2
            tot = g[CHUNK - 1:CHUNK, :] if d == 0 else g[0:1, :]
            e_ng = jnp.exp(-g)
            e_rem = jnp.exp(tot - g)
            kt2 = kap2 * jnp.exp(g - lw)
            rt2 = r2 * jnp.exp(g)
            lhs2 = jnp.concatenate([kt2, rt2], axis=0).astype(bf16)
            grp = dict(
                d=d, c=c, rs=rs, rt2=rt2, v2=v2, v2b=v2.astype(bf16), e_tot=jnp.exp(tot),
                kt2sw=pltpu.roll(kt2, A_HEAD, 1),
                rhs2=jnp.concatenate([kd2 * e_ng, b2 * e_ng], axis=0).astype(bf16),
                kbbb2=jnp.concatenate([kd2 * e_rem, b2 * e_rem], axis=0).astype(bf16), chains=[])
            for hh in range(2):
                ch = dict(grp=grp, d=d, hh=hh)
                lhs = jnp.where(half[hh], lhs2, jnp.zeros_like(lhs2))
                ch["a"] = _dot_nt(lhs, grp["rhs2"])
                grp["chains"].append(ch)
            groups.append(grp)
        return groups

    def chain_stages(groups):
        chains = [ch for grp in groups for ch in grp["chains"]]
        for ch in chains:
            top = jnp.where(strict2[ch["d"]], ch["a"][:CHUNK], 0.0)
            ch["bot"] = jnp.where(incl2[ch["d"]], ch["a"][CHUNK:], 0.0).astype(bf16)
            ch["akk"] = top[:, :CHUNK].astype(bf16)
            akb = top[:, CHUNK:]
            ch["ab"] = akb.astype(bf16)
            ch["t"] = eye_f - jnp.where(levels[0], akb, 0.0)
        yield
        for m in levels[1:]:
            for ch in chains:
                ch["tb"] = ch["t"].astype(bf16)
                off = jnp.where(m, ch["ab"], jnp.zeros_like(ch["ab"]))
                ch["p"] = _dot(off, ch["tb"]).astype(bf16)
            yield
            for ch in chains:
                ch["t"] = ch["t"] - _dot(ch["tb"], ch["p"])
            yield
        for ch in chains:
            ch["w0v"] = _dot(ch["akk"], ch["grp"]["v2b"])
        yield
        for ch in chains:
            rhs = jnp.where(half[ch["hh"]], ch["w0v"], ch["grp"]["kt2sw"])
            ch["x"] = _dot(ch["t"].astype(bf16), rhs.astype(bf16))
        yield
        for ch in chains:
            vtop = jnp.where(half[ch["hh"]], ch["grp"]["v2"], 0.0)
            ch["z"] = jnp.concatenate([vtop, -ch["x"]], axis=0).astype(bf16)
            ch["o"] = _dot(ch["bot"], ch["z"])
        yield
        for ch in chains:
            ch["tn"] = _dot_tn(ch["z"], ch["grp"]["kbbb2"])
        yield
        for grp in groups:
            d, rs = grp["d"], grp["rs"]
            c0, c1 = grp["chains"]
            y0_s[d, rs, :] = jnp.where(half[0], c0["o"], c1["o"])
            rp_s[d, rs, :] = grp["rt2"] + pltpu.roll(jnp.where(half[0], c1["o"], c0["o"]), A_HEAD, 1)
            ps = slice(grp["c"] * LANE, (grp["c"] + 1) * LANE)
            nt_s[d, ps, :] = jnp.concatenate([jnp.where(half[0], c0["tn"][:A_HEAD], 0.0),
                                              jnp.where(half[1], c1["tn"][A_HEAD:], 0.0)], axis=0)
            mt_s[d, ps, :] = jnp.where(eye_pair, grp["e_tot"], 0.0) + jnp.concatenate(
                [jnp.where(half[0], c0["tn"][A_HEAD:], 0.0), jnp.where(half[1], c1["tn"][:A_HEAD], 0.0)],
                axis=0)

    def seq_step(seqs, i):
        steps = []
        for q in seqs:
            for d in range(2):
                c = q * nch_seq + (i if d == 0 else nch_seq - 1 - i)
                steps.append((q, d, slice(c * CHUNK, (c + 1) * CHUNK), slice(c * LANE, (c + 1) * LANE),
                              st_s[q, d].astype(bf16)))
        for q, d, rs, ps, sb in steps:
            st_s[q, d] = _dot(sb, mt_s[d, ps, :].astype(bf16)) + nt_s[d, ps, :]
        for q, d, rs, ps, sb in steps:
            y_s[d, rs, :] = y0_s[d, rs, :] + _dot_nt(rp_s[d, rs, :].astype(bf16), sb)

    def finish_a(c, fin):
        rs = slice(c * CHUNK, (c + 1) * CHUNK)
        y = y_s[0, rs, :] + y_s[1, rs, :]
        kmean = 0.5 * (kd_s[0, rs, :] + kd_s[1, rs, :])
        fin[c] = dict(rs=rs, y=y, ysum=_head_sum(y, bd), bsum=_head_sum(r_s[rs, :] * kmean * row(V_RK), bd))

    def finish_b(c, fin):
        f = fin[c]
        f["yc"] = f["y"] - f["ysum"] * (1.0 / A_HEAD)
        f["vsum"] = _head_sum(f["yc"] * f["yc"], bd)

    def finish_c(c, fin):
        f = fin[c]
        rs = f["rs"]
        yn = f["yc"] * lax.rsqrt(f["vsum"] * (1.0 / A_HEAD) + GN_EPS) * row(V_LNG) + row(V_LNB)
        mix[rs, :LANE] = (yn + f["bsum"] * v_s[rs, :]) * _silu(x_ref[rs, 3 * LANE:4 * LANE])

    def finish_fillers(chunks, fin):
        chunks = list(chunks)
        out = []
        for t in range(len(chunks) + 2):
            slot = []
            if t < len(chunks):
                slot.append(functools.partial(finish_a, chunks[t], fin))
            if 0 <= t - 1 < len(chunks):
                slot.append(functools.partial(finish_b, chunks[t - 1], fin))
            if 0 <= t - 2 < len(chunks):
                slot.append(functools.partial(finish_c, chunks[t - 2], fin))
            out.append(slot)
        return out

    def conv_branch():
        u = col(5) * col(6)
        conv = prev(u) * row(V_CW) + u * row(V_CW + 1) + nxt(u) * row(V_CW + 2) + row(V_CB)
        mix[:, LANE:] = col(4) * conv * _silu(col(7))

    def run(stages, slots):
        slots = list(slots)
        for _ in stages:
            if slots:
                for f in slots.pop(0):
                    f()
        for slot in slots:
            for f in slot:
                f()

    hc = nch // 2
    dots = [setup_dots(c) for c in range(hc)]
    first_half = [grp for st in dots for grp in setup_chains(st)]
    second_half, pending = [], {}
    slots = []
    for t in range(hc + 1):
        slot = []
        if t < hc:
            slot.append(functools.partial(lambda c: pending.__setitem__(c, setup_dots(c)), hc + t))
        if t >= 1:
            slot.append(functools.partial(lambda c: second_half.extend(setup_chains(pending[c])), hc + t - 1))
        slots.append(slot)
    run(chain_stages(first_half), slots)
    fin = {}
    if NB >= 2:
        early = list(range(NB // 2))
        late = list(range(NB // 2, NB))
        slots = [[conv_branch]] + [[functools.partial(seq_step, early, i)] for i in range(nch_seq)]
        slots += finish_fillers(range(hc), fin)
        run(chain_stages(second_half), slots)
        for i in range(nch_seq):
            seq_step(late, i)
        run(iter(()), finish_fillers(range(hc, nch), fin))
    else:
        run(chain_stages(second_half), [[conv_branch]])
        ready = []
        for i in range(nch_seq):
            seq_step([0], i)
            if i >= nch_seq // 2:
                ready += [i, nch_seq - 1 - i]
            for c in list(ready):
                stage = fin.get(c, {}).get("stage", 0)
                (finish_a, finish_b, finish_c)[stage](c, fin)
                fin[c]["stage"] = stage + 1
                if stage == 2:
                    ready.remove(c)
        while ready:
            for c in list(ready):
                stage = fin[c]["stage"]
                (finish_a, finish_b, finish_c)[stage](c, fin)
                fin[c]["stage"] = stage + 1
                if stage == 2:
                    ready.remove(c)

    if write_s:
        for q in range(NB):
            for d in range(2):
                for hh in range(2):
                    sout[q, d, hh] = st_s[q, d, hh * A_HEAD:(hh + 1) * A_HEAD, hh * A_HEAD:(hh + 1) * A_HEAD]


def _mixer(proj, tables, s0, pair, *, L, n_seq, write_s):
    vecs, mats, mulo = tables
    has_s0 = s0 is not None
    R = MIXER_ROWS
    nb = R // L
    in_specs = [
        pl.BlockSpec((R, PAIR_COLS), lambda s, j: (s, j)),
        pl.BlockSpec((R, 4 * LORA), lambda s, j: (s, N_PAIR * PAIR_COLS // (4 * LORA))),
        pl.BlockSpec((N_PAIR, N_VEC, LANE), lambda s, j: (0, 0, 0)),
        pl.BlockSpec((N_PAIR, 4, LORA, LANE), lambda s, j: (0, 0, 0, 0)),
        pl.BlockSpec((1, 4 * LORA), lambda s, j: (0, 0)),
    ]
    args = [proj, proj, vecs, mats, mulo]
    if has_s0:
        in_specs.append(pl.BlockSpec((nb, None, 2, 2, A_HEAD, A_HEAD), lambda s, j: (s, pair, 0, j, 0, 0)))
        args.append(s0)
    out_specs = [pl.BlockSpec((R, 2 * LANE), lambda s, j: (s, j))]
    out_shape = [jax.ShapeDtypeStruct((n_seq * L, D_MODEL), f32)]
    if write_s:
        out_specs.append(pl.BlockSpec((nb, 2, 2, A_HEAD, A_HEAD), lambda s, j: (s, 0, j, 0, 0)))
        out_shape.append(jax.ShapeDtypeStruct((n_seq, 2, A_HEADS, A_HEAD, A_HEAD), f32))
    tok = lambda: pltpu.VMEM((R, LANE), f32)
    tok2 = lambda: pltpu.VMEM((2, R, LANE), f32)
    maps = lambda: pltpu.VMEM((2, 2 * R, LANE), f32)
    scratch = [tok(), tok(), tok(), pltpu.VMEM((R, 4 * LORA), f32), tok2(), tok2(), tok2(), tok2(),
               maps(), maps(), pltpu.VMEM((nb, 2, LANE, LANE), f32)]
    return pl.pallas_call(
        functools.partial(_mixer_kernel, L=L, NB=nb, has_s0=has_s0, write_s=write_s),
        grid=(n_seq // nb, N_PAIR),
        in_specs=in_specs,
        out_specs=out_specs,
        out_shape=out_shape,
        scratch_shapes=scratch,
        compiler_params=pltpu.CompilerParams(vmem_limit_bytes=VMEM_LIMIT),
        name="rwkv_conv_mixer_L%d" % L,
    )(*args)


def _attn_kernel(*refs, LQ, LN, PAST, rope, write_cache):
    it = iter(refs)
    q_ref, k_ref, v_ref, g_ref, qg_ref, kg_ref = (next(it) for _ in range(6))
    if rope:
        cosq, sinq, cosk, sink = (next(it) for _ in range(4))
    if PAST:
        ck, cv = next(it), next(it)
    out_ref = next(it)
    if write_cache:
        nk_ref, nv_ref = next(it), next(it)
    kbuf, vbuf, obuf = it

    bd = _head_block_diag()
    lane = lax.broadcasted_iota(jnp.int32, (1, LANE), 1)
    first_half = (lane & 31) < 16

    def rms(x, g):
        return x * lax.rsqrt(_head_sum(x * x, bd, pieces=1) * (1.0 / C_HEAD) + EPS) * g

    def rotate(x, cos, sin):
        outs = []
        for i in range(x.shape[1] // LANE):
            xb = x[:, i * LANE:(i + 1) * LANE]
            sw = jnp.where(first_half, pltpu.roll(xb, LANE - 16, 1), pltpu.roll(xb, 16, 1))
            outs.append(xb * cos + sw * sin)
        return jnp.concatenate(outs, axis=1)

    @pl.when(pl.program_id(1) == 0)
    def _():
        kn = rms(k_ref[...], kg_ref[...])
        vx = v_ref[...]
        if write_cache:
            for h in range(C_KV_HEADS):
                nk_ref[h] = kn[:, h * C_HEAD:(h + 1) * C_HEAD]
                nv_ref[h] = vx[:, h * C_HEAD:(h + 1) * C_HEAD]
        if rope:
            kn = rotate(kn, cosk[...], sink[...])
        for h in range(C_KV_HEADS):
            if PAST:
                kbuf[h, :PAST, :] = ck[h].astype(bf16)
                vbuf[h, :PAST, :] = cv[h].astype(bf16)
            kbuf[h, PAST:, :] = kn[:, h * C_HEAD:(h + 1) * C_HEAD].astype(bf16)
            vbuf[h, PAST:, :] = vx[:, h * C_HEAD:(h + 1) * C_HEAD].astype(bf16)

    qn = rms(q_ref[...], qg_ref[...])
    if rope:
        qn = rotate(qn, cosq[...], sinq[...])
    qn = (qn * (C_HEAD ** -0.5)).astype(bf16)
    for h in range(C_KV_HEADS):
        heads = [slice((h * C_GROUP + g) * C_HEAD, (h * C_GROUP + g + 1) * C_HEAD) for g in range(C_GROUP)]
        s = _dot_nt(jnp.concatenate([qn[:, sl] for sl in heads], axis=0), kbuf[h])
        p = jnp.exp(s - jnp.max(s, axis=-1, keepdims=True))
        o = _dot(p.astype(bf16), vbuf[h]) / jnp.sum(p, axis=-1, keepdims=True)
        for g, sl in enumerate(heads):
            obuf[:, sl] = o[g * LQ:(g + 1) * LQ]
    out_ref[...] = obuf[...] * _silu(g_ref[...])


def _rope_tables():
    pos = np.arange(DEC_SEQ)
    inv = ROPE_THETA ** (-np.arange(0, ROPE_HALF, 2, dtype=np.float32) / ROPE_HALF)
    ang_r = (pos // GRID_W).astype(np.float32)[:, None] * inv
    ang_c = (pos % GRID_W).astype(np.float32)[:, None] * inv
    cos64 = np.concatenate([np.cos(ang_r), np.cos(ang_r), np.cos(ang_c), np.cos(ang_c)], axis=1)
    sin64 = np.concatenate([-np.sin(ang_r), np.sin(ang_r), -np.sin(ang_c), np.sin(ang_c)], axis=1)
    return jnp.asarray(np.tile(cos64, (1, 2)), f32), jnp.asarray(np.tile(sin64, (1, 2)), f32)


def _attention(proj, q_g, k_g, cache_k, cache_v, pair, tables, *, n_seq, L, sample):
    LQ = ATTN_ROWS
    nq = L // LQ
    past = PAST_LEN if sample else 0
    in_specs = [
        pl.BlockSpec((LQ, D_MODEL), lambda b, i: (b * nq + i, 0)),
        pl.BlockSpec((L, KV_WIDTH), lambda b, i: (b, 2 * D_MODEL // KV_WIDTH)),
        pl.BlockSpec((L, KV_WIDTH), lambda b, i: (b, 2 * D_MODEL // KV_WIDTH + 1)),
        pl.BlockSpec((LQ, D_MODEL), lambda b, i: (b * nq + i, 1)),
        pl.BlockSpec((1, D_MODEL), lambda b, i: (0, 0)),
        pl.BlockSpec((1, KV_WIDTH), lambda b, i: (0, 0)),
    ]
    args = [proj, proj, proj, proj, q_g, k_g]
    if sample:
        cos, sin = tables
        in_specs += [pl.BlockSpec((LQ, LANE), lambda b, i: (i, 0)),
                     pl.BlockSpec((LQ, LANE), lambda b, i: (i, 0)),
                     pl.BlockSpec((L, LANE), lambda b, i: (0, 0)),
                     pl.BlockSpec((L, LANE), lambda b, i: (0, 0))]
        args += [cos, sin, cos, sin]
        cspec = pl.BlockSpec((None, None, C_KV_HEADS, PAST_LEN, C_HEAD), lambda b, i: (b, pair, 0, 0, 0))
        in_specs += [cspec, cspec]
        args += [cache_k, cache_v]
    out_specs = [pl.BlockSpec((LQ, D_MODEL), lambda b, i: (b * nq + i, 0))]
    out_shape = [jax.ShapeDtypeStruct((n_seq * L, D_MODEL), f32)]
    if not sample:
        nspec = pl.BlockSpec((None, C_KV_HEADS, L, C_HEAD), lambda b, i: (b, 0, 0, 0))
        out_specs += [nspec, nspec]
        out_shape += [jax.ShapeDtypeStruct((n_seq, C_KV_HEADS, L, C_HEAD), f32)] * 2
    return pl.pallas_call(
        functools.partial(_attn_kernel, LQ=LQ, LN=L, PAST=past, rope=sample, write_cache=not sample),
        grid=(n_seq, nq),
        in_specs=in_specs,
        out_specs=out_specs,
        out_shape=out_shape,
        scratch_shapes=[pltpu.VMEM((C_KV_HEADS, past + L, C_HEAD), bf16),
                        pltpu.VMEM((C_KV_HEADS, past + L, C_HEAD), bf16),
                        pltpu.VMEM((LQ, D_MODEL), f32)],
        compiler_params=pltpu.CompilerParams(vmem_limit_bytes=VMEM_LIMIT),
        name="gqa_sample" if sample else "gqa_prompt",
    )(*args)


def kernel(x_prompt, x_sample, c, state_rwkv, cache_k, cache_v, c_ctx, w_ada, b_ada, norm_g, final_g,
           w_in_e, mu_shift, lora_w2, w0, lora_a2, a0, k_k, k_a, r_k, lnx_g, lnx_b, conv_w, conv_b,
           w_out_e, w_in_o, q_norm_g, k_norm_g, w_out_o):
    xs = [x_prompt.reshape(STREAM_TOK, D_MODEL), x_sample.reshape(STREAM_TOK, D_MODEL)]
    cond = jnp.concatenate([c_ctx[None, :], c, jnp.zeros((N_MOD - 1 - DEC_BATCH, D_MODEL), f32)], axis=0)
    mod = _adaln(cond, w_ada, b_ada).reshape(DEPTH, N_MOD, 1, 3 * D_MODEL)

    rope = _rope_tables()
    final_g2 = final_g.reshape(1, D_MODEL)

    w_ins, w_outs = [], []
    for layer in range(DEPTH):
        p = layer // 2
        if layer % 2 == 0:
            w_ins.append(_even_in_weight(w_in_e[p]))
            wo = w_out_e[p]
            w_outs.append(jnp.concatenate(
                [wo[h * A_WIDTH + j * LANE:h * A_WIDTH + (j + 1) * LANE] for j in range(N_PAIR) for h in range(2)],
                axis=0).astype(bf16))
        else:
            wi = w_in_o[p]
            w_ins.append(jnp.concatenate([wi[:, :D_MODEL], wi[:, D_MODEL + 2 * KV_WIDTH:],
                                          wi[:, D_MODEL:D_MODEL + 2 * KV_WIDTH]], axis=1).astype(bf16))
            w_outs.append(w_out_o[p].astype(bf16))
    inp = lambda layer: (mod[layer], norm_g[layer].reshape(1, D_MODEL), w_ins[layer])

    new_rwkv, new_k, new_v = [], [], []
    (proj_p,) = _proj(xs[0], sample=False, inp=inp(0))
    (proj_s,) = _proj(xs[1], sample=True, inp=inp(0))
    for layer in range(DEPTH):
        p = layer // 2
        if layer % 2 == 0:
            tables = _mixer_tables(p, mu_shift, lora_w2, w0, lora_a2, a0, k_k, k_a, r_k, lnx_g, lnx_b,
                                   conv_w, conv_b)
            act_p, st_p = _mixer(proj_p, tables, None, p, L=SEQ, n_seq=BATCH, write_s=True)
            (act_s,) = _mixer(proj_s, tables, state_rwkv, p, L=DEC_SEQ, n_seq=DEC_BATCH, write_s=False)
            new_rwkv.append(st_p)
        else:
            q_g = jnp.tile(q_norm_g[p], C_Q_HEADS).reshape(1, D_MODEL)
            k_g = jnp.tile(k_norm_g[p], C_KV_HEADS).reshape(1, KV_WIDTH)
            act_p, nk, nv = _attention(proj_p, q_g, k_g, None, None, p, None, n_seq=BATCH, L=SEQ,
                                       sample=False)
            (act_s,) = _attention(proj_s, q_g, k_g, cache_k, cache_v, p, rope, n_seq=DEC_BATCH,
                                  L=DEC_SEQ, sample=True)
            new_k.append(nk)
            new_v.append(nv)
        if layer < DEPTH - 1:
            xs[0], proj_p = _proj(xs[0], sample=False, out=(act_p, mod[layer], w_outs[layer]), inp=inp(layer + 1))
            xs[1], proj_s = _proj(xs[1], sample=True, out=(act_s, mod[layer], w_outs[layer]), inp=inp(layer + 1))
        else:
            (xs[0],) = _proj(xs[0], sample=False, out=(act_p, mod[layer], w_outs[layer]), final_g=final_g2)
            (xs[1],) = _proj(xs[1], sample=True, out=(act_s, mod[layer], w_outs[layer]), final_g=final_g2)
    return (xs[0].reshape(BATCH, SEQ, D_MODEL), xs[1].reshape(DEC_BATCH, DEC_SEQ, D_MODEL),
            jnp.stack(new_rwkv, axis=1), jnp.stack(new_k, axis=1), jnp.stack(new_v, axis=1))
```

```python
import functools

import numpy as np
import jax
import jax.numpy as jnp
from jax import lax
from jax.experimental import pallas as pl
from jax.experimental.pallas import tpu as pltpu

f32 = jnp.float32
bf16 = jnp.bfloat16
HI = lax.Precision.HIGHEST

D_MODEL = 1024
BATCH = 32
SEQ = 256
DEPTH = 4
DEC_BATCH = 8
DEC_SEQ = 1024
PAST_LEN = 512
GRID_W = 64
EPS = 1e-6
A_WIDTH = 512
A_HEAD = 64
A_HEADS = 8
LORA = 64
GN_EPS = 64e-5
B_WIDTH = 512
C_HEAD = 64
C_Q_HEADS = 16
C_KV_HEADS = 4
C_GROUP = 4
KV_WIDTH = 256
ROPE_THETA = 10000.0
ROPE_HALF = 32
A_SHIFT = 3 * A_WIDTH + 4 * LORA
EVEN_IN = A_SHIFT + A_WIDTH + 4 * B_WIDTH
ODD_IN = 2 * D_MODEL + 2 * KV_WIDTH

STREAM_TOK = BATCH * SEQ
assert STREAM_TOK == DEC_BATCH * DEC_SEQ
PROJ_ROWS = 512
ATTN_ROWS = 256
ATTN_PROMPT_SEQS = 4
LOG2E = 1.4426950408889634
N_MOD = 16
CHUNK = 64
LANE = 128
VMEM_LIMIT = 48 * 1024 * 1024
N_PAIR = A_WIDTH // LANE
PAIR_COLS = 8 * LANE
N_VEC = 16
MIXER_ROWS = 1024


def _sigmoid(x):
    return 1.0 / (1.0 + jnp.exp(-x))


def _silu(x):
    return x * _sigmoid(x)


def _dot(a, b, precision=None):
    return jnp.dot(a, b, precision=precision, preferred_element_type=f32)


def _dot_nt(a, b, precision=None):
    return lax.dot_general(a, b, (((1,), (1,)), ((), ())), precision=precision,
                           preferred_element_type=f32)


def _dot_tn(a, b, precision=None):
    return lax.dot_general(a, b, (((0,), (0,)), ((), ())), precision=precision,
                           preferred_element_type=f32)


def _mod_row(sample):
    if sample:
        return lambda i: 1 + i // (DEC_SEQ // PROJ_ROWS)
    return lambda i: 0


def _head_block_diag():
    ri = lax.broadcasted_iota(jnp.int32, (LANE, LANE), 0) >> 6
    ci = lax.broadcasted_iota(jnp.int32, (LANE, LANE), 1) >> 6
    return (ri == ci).astype(bf16)


def _head_sum(x, bd, pieces=2):
    n = x.shape[1] // LANE
    hi = x.astype(bf16)
    cols = [_dot(hi[:, i * LANE:(i + 1) * LANE], bd) for i in range(n)]
    if pieces == 2:
        lo = (x - hi.astype(f32)).astype(bf16)
        cols = [c + _dot(lo[:, i * LANE:(i + 1) * LANE], bd) for i, c in enumerate(cols)]
    return cols[0] if n == 1 else jnp.concatenate(cols, axis=1)


def _adaln_kernel(cond_ref, w_ref, b_ref, o_ref):
    o_ref[...] = _dot(_silu(cond_ref[...]), w_ref[...], HI) + b_ref[...]


def _adaln(cond, w_ada, b_ada):
    tn = 768
    return pl.pallas_call(
        _adaln_kernel,
        grid=(DEPTH, 3 * D_MODEL // tn),
        in_specs=[
            pl.BlockSpec((N_MOD, D_MODEL), lambda l, n: (0, 0)),
            pl.BlockSpec((None, D_MODEL, tn), lambda l, n: (l, 0, n)),
            pl.BlockSpec((None, 1, tn), lambda l, n: (l, 0, n)),
        ],
        out_specs=pl.BlockSpec((None, N_MOD, tn), lambda l, n: (l, 0, n)),
        out_shape=jax.ShapeDtypeStruct((DEPTH, N_MOD, 3 * D_MODEL), f32),
        compiler_params=pltpu.CompilerParams(vmem_limit_bytes=VMEM_LIMIT),
        name="adaln",
    )(cond, w_ada, b_ada.reshape(DEPTH, 1, 3 * D_MODEL))


def _proj_kernel(*refs, has_out, has_in, final):
    it = iter(refs)
    if has_out:
        a_ref, x_ref, modp_ref, wo_ref = (next(it) for _ in range(4))
        fg_ref = next(it) if final else None
    else:
        x_ref = next(it)
    if has_in:
        modn_ref, g_ref, wi_ref = (next(it) for _ in range(3))
    x = x_ref[...]
    if has_out:
        x = x + modp_ref[:, 2 * D_MODEL:] * _dot(a_ref[...].astype(bf16), wo_ref[...])
        xo_ref = next(it)
        if final:
            xo_ref[...] = x * lax.rsqrt(jnp.mean(x * x, axis=-1, keepdims=True) + EPS) * fg_ref[...]
        else:
            xo_ref[...] = x
    if has_in:
        p_ref = next(it)
        y = x * lax.rsqrt(jnp.mean(x * x, axis=-1, keepdims=True) + EPS) * g_ref[...]
        m = modn_ref[...]
        h = y * (1.0 + m[:, D_MODEL:2 * D_MODEL]) + m[:, :D_MODEL]
        p_ref[...] = _dot(h.astype(bf16), wi_ref[...])


def _proj(x, *, sample, out=None, inp=None, final_g=None):
    mrow = _mod_row(sample)
    tm = PROJ_ROWS
    rows = lambda w: pl.BlockSpec((tm, w), lambda i: (i, 0))
    const = lambda shape: pl.BlockSpec(shape, lambda i: (0,) * len(shape), pipeline_mode=pl.Buffered(1))
    modspec = pl.BlockSpec((None, 1, 3 * D_MODEL), lambda i: (mrow(i), 0, 0))
    in_specs, args, out_specs, out_shape = [], [], [], []
    if out is not None:
        act, mod_prev, w_out = out
        in_specs += [rows(D_MODEL), rows(D_MODEL), modspec, const((D_MODEL, D_MODEL))]
        args += [act, x, mod_prev, w_out]
        if final_g is not None:
            in_specs.append(const((1, D_MODEL)))
            args.append(final_g)
        out_specs.append(rows(D_MODEL))
        out_shape.append(jax.ShapeDtypeStruct((STREAM_TOK, D_MODEL), f32))
    else:
        in_specs.append(rows(D_MODEL))
        args.append(x)
    if inp is not None:
        mod_next, g, w_in = inp
        n = w_in.shape[1]
        in_specs += [modspec, const((1, D_MODEL)), const((D_MODEL, n))]
        args += [mod_next, g, w_in]
        out_specs.append(rows(n))
        out_shape.append(jax.ShapeDtypeStruct((STREAM_TOK, n), f32))
    return pl.pallas_call(
        functools.partial(_proj_kernel, has_out=out is not None, has_in=inp is not None,
                          final=final_g is not None),
        grid=(STREAM_TOK // tm,),
        in_specs=in_specs,
        out_specs=out_specs,
        out_shape=out_shape,
        compiler_params=pltpu.CompilerParams(vmem_limit_bytes=VMEM_LIMIT),
        name="proj_%s%s" % ("o" if out is not None else "", "i" if inp is not None else ""),
    )(*args)


def _even_column_starts():
    bases = [0, A_WIDTH, 2 * A_WIDTH, A_SHIFT, A_SHIFT + A_WIDTH, A_SHIFT + A_WIDTH + B_WIDTH,
             A_SHIFT + A_WIDTH + 2 * B_WIDTH, A_SHIFT + A_WIDTH + 3 * B_WIDTH]
    return [(b + j * LANE, LANE) for j in range(N_PAIR) for b in bases] + [(3 * A_WIDTH, 4 * LORA)]


def _even_column_order():
    return np.concatenate([s + np.arange(w) for s, w in _even_column_starts()])


def _even_in_weight(w):
    return jnp.concatenate([w[:, s:s + n] for s, n in _even_column_starts()], axis=1).astype(bf16)


def _mixer_tables(p, mu_shift, lora_w2, w0, lora_a2, a0, k_k, k_a, r_k, lnx_g, lnx_b, conv_w, conv_b):
    pair = lambda v: v.reshape(-1, N_PAIR, LANE).transpose(1, 0, 2)
    mu = mu_shift[p]
    rows = [k_k[p][None], k_a[p][None], r_k[p].reshape(1, A_WIDTH), lnx_g[p][None], lnx_b[p][None],
            conv_b[p][None], conv_w[p], w0[p], a0[p],
            mu[None, :A_WIDTH], mu[None, A_WIDTH:2 * A_WIDTH], mu[None, 2 * A_WIDTH:3 * A_WIDTH]]
    vecs = pair(jnp.concatenate(rows, axis=0))
    assert vecs.shape == (N_PAIR, N_VEC, LANE)
    mats = pair(jnp.concatenate([lora_w2[p], lora_a2[p]], axis=0).reshape(4 * LORA, A_WIDTH))
    mats = mats.reshape(N_PAIR, 4, LORA, LANE)
    return vecs, mats, mu[None, 3 * A_WIDTH:]


V_KK, V_KA, V_RK, V_LNG, V_LNB, V_CB, V_CW, V_W0, V_A0, V_MU = 0, 1, 2, 3, 4, 5, 6, 9, 11, 13


def _scan_masks():
    ri = lax.broadcasted_iota(jnp.int32, (CHUNK, CHUNK), 0)
    ci = lax.broadcasted_iota(jnp.int32, (CHUNK, CHUNK), 1)
    eye = ri == ci
    ri2 = lax.broadcasted_iota(jnp.int32, (CHUNK, 2 * CHUNK), 0)
    ci2 = lax.broadcasted_iota(jnp.int32, (CHUNK, 2 * CHUNK), 1) & (CHUNK - 1)
    strict2 = (ri2 > ci2, ri2 < ci2)
    incl2 = (ri2 >= ci2, ri2 <= ci2)
    levels = []
    for sh in range(6):
        levels.append(((ri >> (sh + 1)) == (ci >> (sh + 1))) & ((ri >> sh) != (ci >> sh)))
    return eye, strict2, incl2, levels


def _mixer_kernel(*refs, L, NB, has_s0, write_s):
    it = iter(refs)
    x_ref, lo_ref, vec_ref, mat_ref, mulo = (next(it) for _ in range(5))
    s0 = next(it) if has_s0 else None
    mix = next(it)
    sout = next(it) if write_s else None
    r_s, k_s, v_s, lo_s, kd_s, y_s, rp_s, y0_s, mt_s, nt_s, st_s = it

    j = pl.program_id(1)
    vec = vec_ref[j]
    row = lambda i: vec[i:i + 1, :]
    col = lambda i: x_ref[:, i * LANE:(i + 1) * LANE]
    lora_w = [mat_ref[j, i].astype(bf16) for i in range(4)]

    R = NB * L
    nch, nch_seq = R // CHUNK, L // CHUNK
    rows = lax.broadcasted_iota(jnp.int32, (R, 1), 0)
    first, last = (rows & (L - 1)) == 0, (rows & (L - 1)) == L - 1

    def prev(f):
        return jnp.where(first, 0.0, pltpu.roll(f, 1, 0))

    def nxt(f):
        return jnp.where(last, 0.0, pltpu.roll(f, R - 1, 0))

    def tshift(f, mu):
        return f + mu * (0.5 * (prev(f) + nxt(f)) - f)

    bd = _head_block_diag()

    r_s[...] = tshift(col(0), row(V_MU))
    k_s[...] = tshift(col(1), row(V_MU + 1))
    v_s[...] = tshift(col(2), row(V_MU + 2))
    lo_s[...] = tshift(lo_ref[...], mulo[...])
    for q in range(NB):
        for d in range(2):
            st_s[q, d] = jnp.zeros((LANE, LANE), f32)
            if has_s0:
                for hh in range(2):
                    st_s[q, d, hh * A_HEAD:(hh + 1) * A_HEAD, hh * A_HEAD:(hh + 1) * A_HEAD] = s0[q, d, hh]

    eye, strict2, incl2, levels = _scan_masks()
    eye_f = eye.astype(f32)
    lane = lax.broadcasted_iota(jnp.int32, (1, LANE), 1)
    half = (lane < A_HEAD, lane >= A_HEAD)
    ri = lax.broadcasted_iota(jnp.int32, (LANE, LANE), 0)
    ci = lax.broadcasted_iota(jnp.int32, (LANE, LANE), 1)
    eye_pair = ri == ci
    crow = lax.broadcasted_iota(jnp.int32, (CHUNK, 1), 0)

    def setup_dots(c):
        rs = slice(c * CHUNK, (c + 1) * CHUNK)
        k2, lo = k_s[rs, :], lo_s[rs, :]
        kk = k2 * row(V_KK)
        st = dict(c=c, rs=rs, k2=k2, kk=kk, ssq=_head_sum(kk * kk, bd), wl=[], al=[])
        for d in range(2):
            lwd = lo[:, d * LORA:(d + 1) * LORA]
            lad = lo[:, 2 * LORA + d * LORA:2 * LORA + (d + 1) * LORA]
            st["wl"].append(_dot(jnp.tanh(lwd).astype(bf16), lora_w[d]))
            st["al"].append(_dot(lad.astype(bf16), lora_w[2 + d]))
        return st

    def setup_chains(st):
        c, rs, k2 = st["c"], st["rs"], st["k2"]
        r2, v2 = r_s[rs, :], v_s[rs, :]
        kap2 = st["kk"] / jnp.maximum(jnp.sqrt(st["ssq"]), 1e-12)
        groups = []
        for d in range(2):
            wl = row(V_W0 + d) + st["wl"][d]
            softplus = jnp.maximum(-wl, 0.0) + jnp.log(1.0 + jnp.exp(-jnp.abs(wl)))
            lw = -jnp.exp(-softplus - 0.5)
            g = lw
            for sh in range(6):
                stp = 1 << sh
                if d == 0:
                    g = g + jnp.where(crow >= stp, pltpu.roll(g, stp, 0), 0.0)
                else:
                    g = g + jnp.where(crow < CHUNK - stp, pltpu.roll(g, CHUNK - stp, 0), 0.0)
            a = _sigmoid(row(V_A0 + d) + st["al"][d])
            kd2 = k2 * (1.0 + (a - 1.0) * row(V_KA))
            kd_s[d, rs, :] = kd2
            b2 = a * kap2
            tot = g[CHUNK - 1:CHUNK, :] if d == 0 else g[0:1, :]
            e_ng = jnp.exp(-g)
            e_rem = jnp.exp(tot - g)
            kt2 = kap2 * jnp.exp(g - lw)
            rt2 = r2 * jnp.exp(g)
            lhs2 = jnp.concatenate([kt2, rt2], axis=0).astype(bf16)
            grp = dict(
                d=d, c=c, rs=rs, rt2=rt2, v2=v2, v2b=v2.astype(bf16), e_tot=jnp.exp(tot),
                kt2sw=pltpu.roll(kt2, A_HEAD, 1),
                rhs2=jnp.concatenate([kd2 * e_ng, b2 * e_ng], axis=0).astype(bf16),
                kbbb2=jnp.concatenate([kd2 * e_rem, b2 * e_rem], axis=0).astype(bf16), chains=[])
            for hh in range(2):
                ch = dict(grp=grp, d=d, hh=hh)
                lhs = jnp.where(half[hh], lhs2, jnp.zeros_like(lhs2))
                ch["a"] = _dot_nt(lhs, grp["rhs2"])
                grp["chains"].append(ch)
            groups.append(grp)
        return groups

    def chain_stages(groups):
        chains = [ch for grp in groups for ch in grp["chains"]]
        for ch in chains:
            top = jnp.where(strict2[ch["d"]], ch["a"][:CHUNK], 0.0)
            ch["bot"] = jnp.where(incl2[ch["d"]], ch["a"][CHUNK:], 0.0).astype(bf16)
            ch["akk"] = top[:, :CHUNK].astype(bf16)
            akb = top[:, CHUNK:]
            ch["ab"] = akb.astype(bf16)
            ch["t"] = eye_f - jnp.where(levels[0], akb, 0.0)
        yield
        for m in levels[1:]:
            for ch in chains:
                ch["tb"] = ch["t"].astype(bf16)
                off = jnp.where(m, ch["ab"], jnp.zeros_like(ch["ab"]))
                ch["p"] = _dot(off, ch["tb"]).astype(bf16)
            yield
            for ch in chains:
                ch["t"] = ch["t"] - _dot(ch["tb"], ch["p"])
            yield
        for ch in chains:
            ch["w0v"] = _dot(ch["akk"], ch["grp"]["v2b"])
        yield
        for ch in chains:
            rhs = jnp.where(half[ch["hh"]], ch["w0v"], ch["grp"]["kt2sw"])
            ch["x"] = _dot(ch["t"].astype(bf16), rhs.astype(bf16))
        yield
        for ch in chains:
            vtop = jnp.where(half[ch["hh"]], ch["grp"]["v2"], 0.0)
            ch["z"] = jnp.concatenate([vtop, -ch["x"]], axis=0).astype(bf16)
            ch["o"] = _dot(ch["bot"], ch["z"])
        yield
        for ch in chains:
            ch["tn"] = _dot_tn(ch["z"], ch["grp"]["kbbb2"])
        yield
        for grp in groups:
            d, rs = grp["d"], grp["rs"]
            c0, c1 = grp["chains"]
            y0_s[d, rs, :] = jnp.where(half[0], c0["o"], c1["o"])
            rp_s[d, rs, :] = grp["rt2"] + pltpu.roll(jnp.where(half[0], c1["o"], c0["o"]), A_HEAD, 1)
            ps = slice(grp["c"] * LANE, (grp["c"] + 1) * LANE)
            nt_s[d, ps, :] = jnp.concatenate([jnp.where(half[0], c0["tn"][:A_HEAD], 0.0),
                                              jnp.where(half[1], c1["tn"][A_HEAD:], 0.0)], axis=0)
            mt_s[d, ps, :] = jnp.where(eye_pair, grp["e_tot"], 0.0) + jnp.concatenate(
                [jnp.where(half[0], c0["tn"][A_HEAD:], 0.0), jnp.where(half[1], c1["tn"][:A_HEAD], 0.0)],
                axis=0)

    def seq_step(seqs, i):
        steps = []
        for q in seqs:
            for d in range(2):
                c = q * nch_seq + (i if d == 0 else nch_seq - 1 - i)
                steps.append((q, d, slice(c * CHUNK, (c + 1) * CHUNK), slice(c * LANE, (c + 1) * LANE),
                              st_s[q, d].astype(bf16)))
        for q, d, rs, ps, sb in steps:
            st_s[q, d] = _dot(sb, mt_s[d, ps, :].astype(bf16)) + nt_s[d, ps, :]
        for q, d, rs, ps, sb in steps:
            y_s[d, rs, :] = y0_s[d, rs, :] + _dot_nt(rp_s[d, rs, :].astype(bf16), sb)

    def finish_a(c, fin):
        rs = slice(c * CHUNK, (c + 1) * CHUNK)
        y = y_s[0, rs, :] + y_s[1, rs, :]
        kmean = 0.5 * (kd_s[0, rs, :] + kd_s[1, rs, :])
        fin[c] = dict(rs=rs, y=y, ysum=_head_sum(y, bd), bsum=_head_sum(r_s[rs, :] * kmean * row(V_RK), bd))

    def finish_b(c, fin):
        f = fin[c]
        f["yc"] = f["y"] - f["ysum"] * (1.0 / A_HEAD)
        f["vsum"] = _head_sum(f["yc"] * f["yc"], bd)

    def finish_c(c, fin):
        f = fin[c]
        rs = f["rs"]
        yn = f["yc"] * lax.rsqrt(f["vsum"] * (1.0 / A_HEAD) + GN_EPS) * row(V_LNG) + row(V_LNB)
        mix[rs, :LANE] = (yn + f["bsum"] * v_s[rs, :]) * _silu(x_ref[rs, 3 * LANE:4 * LANE])

    def finish_fillers(chunks, fin):
        chunks = list(chunks)
        out = []
        for t in range(len(chunks) + 2):
            slot = []
            if t < len(chunks):
                slot.append(functools.partial(finish_a, chunks[t], fin))
            if 0 <= t - 1 < len(chunks):
                slot.append(functools.partial(finish_b, chunks[t - 1], fin))
            if 0 <= t - 2 < len(chunks):
                slot.append(functools.partial(finish_c, chunks[t - 2], fin))
            out.append(slot)
        return out

    def conv_branch():
        u = col(5) * col(6)
        conv = prev(u) * row(V_CW) + u * row(V_CW + 1) + nxt(u) * row(V_CW + 2) + row(V_CB)
        mix[:, LANE:] = col(4) * conv * _silu(col(7))

    def run(stages, slots):
        slots = list(slots)
        for _ in stages:
            if slots:
                for f in slots.pop(0):
                    f()
        for slot in slots:
            for f in slot:
                f()

    hc = nch // 2
    dots = [setup_dots(c) for c in range(hc)]
    first_half = [grp for st in dots for grp in setup_chains(st)]
    second_half, pending = [], {}
    slots = []
    for t in range(hc + 1):
        slot = []
        if t < hc:
            slot.append(functools.partial(lambda c: pending.__setitem__(c, setup_dots(c)), hc + t))
        if t >= 1:
            slot.append(functools.partial(lambda c: second_half.extend(setup_chains(pending[c])), hc + t - 1))
        slots.append(slot)
    run(chain_stages(first_half), slots)
    fin = {}
    if NB >= 2:
        early = list(range(NB // 2))
        late = list(range(NB // 2, NB))
        slots = [[conv_branch]] + [[functools.partial(seq_step, early, i)] for i in range(nch_seq)]
        slots += finish_fillers(range(hc), fin)
        run(chain_stages(second_half), slots)
        for i in range(nch_seq):
            seq_step(late, i)
        run(iter(()), finish_fillers(range(hc, nch), fin))
    else:
        run(chain_stages(second_half), [[conv_branch]])
        ready = []
        for i in range(nch_seq):
            seq_step([0], i)
            if i >= nch_seq // 2:
                ready += [i, nch_seq - 1 - i]
            for c in list(ready):
                stage = fin.get(c, {}).get("stage", 0)
                (finish_a, finish_b, finish_c)[stage](c, fin)
                fin[c]["stage"] = stage + 1
                if stage == 2:
                    ready.remove(c)
        while ready:
            for c in list(ready):
                stage = fin[c]["stage"]
                (finish_a, finish_b, finish_c)[stage](c, fin)
                fin[c]["stage"] = stage + 1
                if stage == 2:
                    ready.remove(c)

    if write_s:
        for q in range(NB):
            for d in range(2):
                for hh in range(2):
                    sout[q, d, hh] = st_s[q, d, hh * A_HEAD:(hh + 1) * A_HEAD, hh * A_HEAD:(hh + 1) * A_HEAD]


def _mixer(proj, tables, s0, pair, *, L, n_seq, write_s):
    vecs, mats, mulo = tables
    has_s0 = s0 is not None
    R = MIXER_ROWS
    nb = R // L
    in_specs = [
        pl.BlockSpec((R, PAIR_COLS), lambda s, j: (s, j)),
        pl.BlockSpec((R, 4 * LORA), lambda s, j: (s, N_PAIR * PAIR_COLS // (4 * LORA))),
        pl.BlockSpec((N_PAIR, N_VEC, LANE), lambda s, j: (0, 0, 0)),
        pl.BlockSpec((N_PAIR, 4, LORA, LANE), lambda s, j: (0, 0, 0, 0)),
        pl.BlockSpec((1, 4 * LORA), lambda s, j: (0, 0)),
    ]
    args = [proj, proj, vecs, mats, mulo]
    if has_s0:
        in_specs.append(pl.BlockSpec((nb, None, 2, 2, A_HEAD, A_HEAD), lambda s, j: (s, pair, 0, j, 0, 0)))
        args.append(s0)
    out_specs = [pl.BlockSpec((R, 2 * LANE), lambda s, j: (s, j))]
    out_shape = [jax.ShapeDtypeStruct((n_seq * L, D_MODEL), f32)]
    if write_s:
        out_specs.append(pl.BlockSpec((nb, 2, 2, A_HEAD, A_HEAD), lambda s, j: (s, 0, j, 0, 0)))
        out_shape.append(jax.ShapeDtypeStruct((n_seq, 2, A_HEADS, A_HEAD, A_HEAD), f32))
    tok = lambda: pltpu.VMEM((R, LANE), f32)
    tok2 = lambda: pltpu.VMEM((2, R, LANE), f32)
    maps = lambda: pltpu.VMEM((2, 2 * R, LANE), f32)
    scratch = [tok(), tok(), tok(), pltpu.VMEM((R, 4 * LORA), f32), tok2(), tok2(), tok2(), tok2(),
               maps(), maps(), pltpu.VMEM((nb, 2, LANE, LANE), f32)]
    return pl.pallas_call(
        functools.partial(_mixer_kernel, L=L, NB=nb, has_s0=has_s0, write_s=write_s),
        grid=(n_seq // nb, N_PAIR),
        in_specs=in_specs,
        out_specs=out_specs,
        out_shape=out_shape,
        scratch_shapes=scratch,
        compiler_params=pltpu.CompilerParams(vmem_limit_bytes=VMEM_LIMIT),
        name="rwkv_conv_mixer_L%d" % L,
    )(*args)


def _attn_kernel(*refs, LQ, LN, NB, NQ, PAST, rope, write_cache):
    it = iter(refs)
    q_ref, k_ref, v_ref, g_ref, qg_ref, kg_ref = (next(it) for _ in range(6))
    if rope:
        cosq, sinq, cosk, sink = (next(it) for _ in range(4))
    if PAST:
        ck, cv = next(it), next(it)
    out_ref = next(it)
    if write_cache:
        nk_ref, nv_ref = next(it), next(it)
    kbuf, vbuf, obuf = it

    bd = _head_block_diag()
    lane = lax.broadcasted_iota(jnp.int32, (1, LANE), 1)
    first_half = (lane & 31) < 16

    def rms(x, g):
        return x * lax.rsqrt(_head_sum(x * x, bd, pieces=1) * (1.0 / C_HEAD) + EPS) * g

    def rotate(x, cos, sin):
        outs = []
        for i in range(x.shape[1] // LANE):
            xb = x[:, i * LANE:(i + 1) * LANE]
            sw = jnp.where(first_half, pltpu.roll(xb, LANE - 16, 1), pltpu.roll(xb, 16, 1))
            outs.append(xb * cos + sw * sin)
        return jnp.concatenate(outs, axis=1)

    def build_keys():
        kn = rms(k_ref[...], kg_ref[...])
        vx = v_ref[...]
        if write_cache:
            for b in range(NB):
                for h in range(C_KV_HEADS):
                    nk_ref[b, h] = kn[b * LN:(b + 1) * LN, h * C_HEAD:(h + 1) * C_HEAD]
                    nv_ref[b, h] = vx[b * LN:(b + 1) * LN, h * C_HEAD:(h + 1) * C_HEAD]
        if rope:
            kn = rotate(kn, cosk[...], sink[...])
        for b in range(NB):
            for h in range(C_KV_HEADS):
                if PAST:
                    kbuf[b, h, :PAST, :] = ck[b, h].astype(bf16)
                    vbuf[b, h, :PAST, :] = cv[b, h].astype(bf16)
                kbuf[b, h, PAST:, :] = kn[b * LN:(b + 1) * LN, h * C_HEAD:(h + 1) * C_HEAD].astype(bf16)
                vbuf[b, h, PAST:, :] = vx[b * LN:(b + 1) * LN, h * C_HEAD:(h + 1) * C_HEAD].astype(bf16)

    if NQ == 1:
        build_keys()
    else:
        pl.when(pl.program_id(1) == 0)(build_keys)

    qn = rms(q_ref[...], qg_ref[...])
    if rope:
        qn = rotate(qn, cosq[...], sinq[...])
    qn = (qn * (C_HEAD ** -0.5 * LOG2E)).astype(bf16)
    units = [(b, h) for b in range(NB) for h in range(C_KV_HEADS)]
    heads = [[slice((h * C_GROUP + g) * C_HEAD, (h * C_GROUP + g + 1) * C_HEAD) for g in range(C_GROUP)]
             for h in range(C_KV_HEADS)]

    def scores(b, h):
        q4 = jnp.concatenate([qn[b * LQ:(b + 1) * LQ, sl] for sl in heads[h]], axis=0)
        return _dot_nt(q4, kbuf[b, h])

    s_next = scores(*units[0])
    for n, (b, h) in enumerate(units):
        s = s_next
        if n + 1 < len(units):
            s_next = scores(*units[n + 1])
        p = jnp.exp2(s - jnp.max(s, axis=-1, keepdims=True))
        o = _dot(p.astype(bf16), vbuf[b, h]) / jnp.sum(p, axis=-1, keepdims=True)
        for g, sl in enumerate(heads[h]):
            obuf[b * LQ:(b + 1) * LQ, sl] = o[g * LQ:(g + 1) * LQ]
    out_ref[...] = obuf[...] * _silu(g_ref[...])


def _rope_tables():
    pos = np.arange(DEC_SEQ)
    inv = ROPE_THETA ** (-np.arange(0, ROPE_HALF, 2, dtype=np.float32) / ROPE_HALF)
    ang_r = (pos // GRID_W).astype(np.float32)[:, None] * inv
    ang_c = (pos % GRID_W).astype(np.float32)[:, None] * inv
    cos64 = np.concatenate([np.cos(ang_r), np.cos(ang_r), np.cos(ang_c), np.cos(ang_c)], axis=1)
    sin64 = np.concatenate([-np.sin(ang_r), np.sin(ang_r), -np.sin(ang_c), np.sin(ang_c)], axis=1)
    return jnp.asarray(np.tile(cos64, (1, 2)), f32), jnp.asarray(np.tile(sin64, (1, 2)), f32)


def _attention(proj, q_g, k_g, cache_k, cache_v, pair, tables, *, n_seq, L, sample):
    LQ = ATTN_ROWS
    nq = L // LQ
    nb = 1 if sample else ATTN_PROMPT_SEQS
    assert nb == 1 or nq == 1
    past = PAST_LEN if sample else 0
    in_specs = [
        pl.BlockSpec((nb * LQ, D_MODEL), lambda b, i: (b * nq + i, 0)),
        pl.BlockSpec((nb * L, KV_WIDTH), lambda b, i: (b, 2 * D_MODEL // KV_WIDTH)),
        pl.BlockSpec((nb * L, KV_WIDTH), lambda b, i: (b, 2 * D_MODEL // KV_WIDTH + 1)),
        pl.BlockSpec((nb * LQ, D_MODEL), lambda b, i: (b * nq + i, 1)),
        pl.BlockSpec((1, D_MODEL), lambda b, i: (0, 0)),
        pl.BlockSpec((1, KV_WIDTH), lambda b, i: (0, 0)),
    ]
    args = [proj, proj, proj, proj, q_g, k_g]
    if sample:
        cos, sin = tables
        in_specs += [pl.BlockSpec((LQ, LANE), lambda b, i: (i, 0)),
                     pl.BlockSpec((LQ, LANE), lambda b, i: (i, 0)),
                     pl.BlockSpec((L, LANE), lambda b, i: (0, 0)),
                     pl.BlockSpec((L, LANE), lambda b, i: (0, 0))]
        args += [cos, sin, cos, sin]
        cspec = pl.BlockSpec((nb, None, C_KV_HEADS, PAST_LEN, C_HEAD), lambda b, i: (b, pair, 0, 0, 0))
        in_specs += [cspec, cspec]
        args += [cache_k, cache_v]
    out_specs = [pl.BlockSpec((nb * LQ, D_MODEL), lambda b, i: (b * nq + i, 0))]
    out_shape = [jax.ShapeDtypeStruct((n_seq * L, D_MODEL), f32)]
    if not sample:
        nspec = pl.BlockSpec((nb, C_KV_HEADS, L, C_HEAD), lambda b, i: (b, 0, 0, 0))
        out_specs += [nspec, nspec]
        out_shape += [jax.ShapeDtypeStruct((n_seq, C_KV_HEADS, L, C_HEAD), f32)] * 2
    return pl.pallas_call(
        functools.partial(_attn_kernel, LQ=LQ, LN=L, NB=nb, NQ=nq, PAST=past, rope=sample, write_cache=not sample),
        grid=(n_seq // nb, nq),
        in_specs=in_specs,
        out_specs=out_specs,
        out_shape=out_shape,
        scratch_shapes=[pltpu.VMEM((nb, C_KV_HEADS, past + L, C_HEAD), bf16),
                        pltpu.VMEM((nb, C_KV_HEADS, past + L, C_HEAD), bf16),
                        pltpu.VMEM((nb * LQ, D_MODEL), f32)],
        compiler_params=pltpu.CompilerParams(vmem_limit_bytes=VMEM_LIMIT),
        name="gqa_sample" if sample else "gqa_prompt",
    )(*args)


def kernel(x_prompt, x_sample, c, state_rwkv, cache_k, cache_v, c_ctx, w_ada, b_ada, norm_g, final_g,
           w_in_e, mu_shift, lora_w2, w0, lora_a2, a0, k_k, k_a, r_k, lnx_g, lnx_b, conv_w, conv_b,
           w_out_e, w_in_o, q_norm_g, k_norm_g, w_out_o):
    xs = [x_prompt.reshape(STREAM_TOK, D_MODEL), x_sample.reshape(STREAM_TOK, D_MODEL)]
    cond = jnp.concatenate([c_ctx[None, :], c, jnp.zeros((N_MOD - 1 - DEC_BATCH, D_MODEL), f32)], axis=0)
    mod = _adaln(cond, w_ada, b_ada).reshape(DEPTH, N_MOD, 1, 3 * D_MODEL)

    rope = _rope_tables()
    final_g2 = final_g.reshape(1, D_MODEL)

    w_ins, w_outs = [], []
    for layer in range(DEPTH):
        p = layer // 2
        if layer % 2 == 0:
            w_ins.append(_even_in_weight(w_in_e[p]))
            wo = w_out_e[p]
            w_outs.append(jnp.concatenate(
                [wo[h * A_WIDTH + j * LANE:h * A_WIDTH + (j + 1) * LANE] for j in range(N_PAIR) for h in range(2)],
                axis=0).astype(bf16))
        else:
            wi = w_in_o[p]
            w_ins.append(jnp.concatenate([wi[:, :D_MODEL], wi[:, D_MODEL + 2 * KV_WIDTH:],
                                          wi[:, D_MODEL:D_MODEL + 2 * KV_WIDTH]], axis=1).astype(bf16))
            w_outs.append(w_out_o[p].astype(bf16))
    inp = lambda layer: (mod[layer], norm_g[layer].reshape(1, D_MODEL), w_ins[layer])

    new_rwkv, new_k, new_v = [], [], []
    (proj_p,) = _proj(xs[0], sample=False, inp=inp(0))
    (proj_s,) = _proj(xs[1], sample=True, inp=inp(0))
    for layer in range(DEPTH):
        p = layer // 2
        if layer % 2 == 0:
            tables = _mixer_tables(p, mu_shift, lora_w2, w0, lora_a2, a0, k_k, k_a, r_k, lnx_g, lnx_b,
                                   conv_w, conv_b)
            act_p, st_p = _mixer(proj_p, tables, None, p, L=SEQ, n_seq=BATCH, write_s=True)
            (act_s,) = _mixer(proj_s, tables, state_rwkv, p, L=DEC_SEQ, n_seq=DEC_BATCH, write_s=False)
            new_rwkv.append(st_p)
        else:
            q_g = jnp.tile(q_norm_g[p], C_Q_HEADS).reshape(1, D_MODEL)
            k_g = jnp.tile(k_norm_g[p], C_KV_HEADS).reshape(1, KV_WIDTH)
            act_p, nk, nv = _attention(proj_p, q_g, k_g, None, None, p, None, n_seq=BATCH, L=SEQ,
                                       sample=False)
            (act_s,) = _attention(proj_s, q_g, k_g, cache_k, cache_v, p, rope, n_seq=DEC_BATCH,
                                  L=DEC_SEQ, sample=True)
            new_k.append(nk)
            new_v.append(nv)
        if layer < DEPTH - 1:
            xs[0], proj_p = _proj(xs[0], sample=False, out=(act_p, mod[layer], w_outs[layer]), inp=inp(layer + 1))
            xs[1], proj_s = _proj(xs[1], sample=True, out=(act_s, mod[layer], w_outs[layer]), inp=inp(layer + 1))
        else:
            (xs[0],) = _proj(xs[0], sample=False, out=(act_p, mod[layer], w_outs[layer]), final_g=final_g2)
            (xs[1],) = _proj(xs[1], sample=True, out=(act_s, mod[layer], w_outs[layer]), final_g=final_g2)
    return (xs[0].reshape(BATCH, SEQ, D_MODEL), xs[1].reshape(DEC_BATCH, DEC_SEQ, D_MODEL),
            jnp.stack(new_rwkv, axis=1), jnp.stack(new_k, axis=1), jnp.stack(new_v, axis=1))
```

```python
import functools

import numpy as np
import jax
import jax.numpy as jnp
from jax import lax
from jax.experimental import pallas as pl
from jax.experimental.pallas import tpu as pltpu

f32 = jnp.float32
bf16 = jnp.bfloat16
HI = lax.Precision.HIGHEST

D_MODEL = 1024
BATCH = 32
SEQ = 256
DEPTH = 4
DEC_BATCH = 8
DEC_SEQ = 1024
PAST_LEN = 512
GRID_W = 64
EPS = 1e-6
A_WIDTH = 512
A_HEAD = 64
A_HEADS = 8
LORA = 64
GN_EPS = 64e-5
B_WIDTH = 512
C_HEAD = 64
C_Q_HEADS = 16
C_KV_HEADS = 4
C_GROUP = 4
KV_WIDTH = 256
ROPE_THETA = 10000.0
ROPE_HALF = 32
A_SHIFT = 3 * A_WIDTH + 4 * LORA
EVEN_IN = A_SHIFT + A_WIDTH + 4 * B_WIDTH
ODD_IN = 2 * D_MODEL + 2 * KV_WIDTH

STREAM_TOK = BATCH * SEQ
assert STREAM_TOK == DEC_BATCH * DEC_SEQ
PROJ_ROWS = 512
ATTN_ROWS = 256
ATTN_PROMPT_SEQS = 4
LOG2E = 1.4426950408889634
N_MOD = 16
CHUNK = 64
LANE = 128
VMEM_LIMIT = 56 * 1024 * 1024
N_PAIR = A_WIDTH // LANE
PAIR_COLS = 8 * LANE
N_VEC = 16
MIXER_ROWS = 1024
N_PAIRS = DEPTH // 2


def _sigmoid(x):
    return 1.0 / (1.0 + jnp.exp(-x))


def _silu(x):
    return x * _sigmoid(x)


def _dot(a, b, precision=None):
    return jnp.dot(a, b, precision=precision, preferred_element_type=f32)


def _dot_nt(a, b, precision=None):
    return lax.dot_general(a, b, (((1,), (1,)), ((), ())), precision=precision,
                           preferred_element_type=f32)


def _dot_tn(a, b, precision=None):
    return lax.dot_general(a, b, (((0,), (0,)), ((), ())), precision=precision,
                           preferred_element_type=f32)


def _mod_row(sample):
    if sample:
        return lambda i: 1 + i // (DEC_SEQ // PROJ_ROWS)
    return lambda i: 0


def _head_block_diag():
    ri = lax.broadcasted_iota(jnp.int32, (LANE, LANE), 0) >> 6
    ci = lax.broadcasted_iota(jnp.int32, (LANE, LANE), 1) >> 6
    return (ri == ci).astype(bf16)


def _head_sum(x, bd, pieces=2):
    n = x.shape[1] // LANE
    hi = x.astype(bf16)
    cols = [_dot(hi[:, i * LANE:(i + 1) * LANE], bd) for i in range(n)]
    if pieces == 2:
        lo = (x - hi.astype(f32)).astype(bf16)
        cols = [c + _dot(lo[:, i * LANE:(i + 1) * LANE], bd) for i, c in enumerate(cols)]
    return cols[0] if n == 1 else jnp.concatenate(cols, axis=1)


def _adaln_kernel(cond_ref, w_ref, b_ref, o_ref):
    o_ref[...] = _dot(_silu(cond_ref[...]), w_ref[...], HI) + b_ref[...]


def _adaln(cond, w_ada, b_ada):
    tn = 768
    return pl.pallas_call(
        _adaln_kernel,
        grid=(DEPTH, 3 * D_MODEL // tn),
        in_specs=[
            pl.BlockSpec((N_MOD, D_MODEL), lambda l, n: (0, 0)),
            pl.BlockSpec((None, D_MODEL, tn), lambda l, n: (l, 0, n)),
            pl.BlockSpec((None, 1, tn), lambda l, n: (l, 0, n)),
        ],
        out_specs=pl.BlockSpec((None, N_MOD, tn), lambda l, n: (l, 0, n)),
        out_shape=jax.ShapeDtypeStruct((DEPTH, N_MOD, 3 * D_MODEL), f32),
        compiler_params=pltpu.CompilerParams(vmem_limit_bytes=VMEM_LIMIT),
        name="adaln",
    )(cond, w_ada, b_ada.reshape(DEPTH, 1, 3 * D_MODEL))


def _proj_kernel(*refs, has_out, has_in, final):
    it = iter(refs)
    if has_out:
        a_ref, x_ref, modp_ref, wo_ref = (next(it) for _ in range(4))
        fg_ref = next(it) if final else None
    else:
        x_ref = next(it)
    if has_in:
        modn_ref, g_ref, wi_ref = (next(it) for _ in range(3))
    x = x_ref[...]
    if has_out:
        x = x + modp_ref[:, 2 * D_MODEL:] * _dot(a_ref[...].astype(bf16), wo_ref[...])
        xo_ref = next(it)
        if final:
            xo_ref[...] = x * lax.rsqrt(jnp.mean(x * x, axis=-1, keepdims=True) + EPS) * fg_ref[...]
        else:
            xo_ref[...] = x
    if has_in:
        p_ref = next(it)
        y = x * lax.rsqrt(jnp.mean(x * x, axis=-1, keepdims=True) + EPS) * g_ref[...]
        m = modn_ref[...]
        h = y * (1.0 + m[:, D_MODEL:2 * D_MODEL]) + m[:, :D_MODEL]
        p_ref[...] = _dot(h.astype(bf16), wi_ref[...])


def _proj(x, *, sample, out=None, inp=None, final_g=None):
    mrow = _mod_row(sample)
    tm = PROJ_ROWS
    rows = lambda w: pl.BlockSpec((tm, w), lambda i: (i, 0))
    const = lambda shape: pl.BlockSpec(shape, lambda i: (0,) * len(shape), pipeline_mode=pl.Buffered(1))
    modspec = pl.BlockSpec((None, 1, 3 * D_MODEL), lambda i: (mrow(i), 0, 0))
    in_specs, args, out_specs, out_shape = [], [], [], []
    if out is not None:
        act, mod_prev, w_out = out
        in_specs += [rows(D_MODEL), rows(D_MODEL), modspec, const((D_MODEL, D_MODEL))]
        args += [act, x, mod_prev, w_out]
        if final_g is not None:
            in_specs.append(const((1, D_MODEL)))
            args.append(final_g)
        out_specs.append(rows(D_MODEL))
        out_shape.append(jax.ShapeDtypeStruct((STREAM_TOK, D_MODEL), f32))
    else:
        in_specs.append(rows(D_MODEL))
        args.append(x)
    if inp is not None:
        mod_next, g, w_in = inp
        n = w_in.shape[1]
        in_specs += [modspec, const((1, D_MODEL)), const((D_MODEL, n))]
        args += [mod_next, g, w_in]
        out_specs.append(rows(n))
        out_shape.append(jax.ShapeDtypeStruct((STREAM_TOK, n), f32))
    return pl.pallas_call(
        functools.partial(_proj_kernel, has_out=out is not None, has_in=inp is not None,
                          final=final_g is not None),
        grid=(STREAM_TOK // tm,),
        in_specs=in_specs,
        out_specs=out_specs,
        out_shape=out_shape,
        compiler_params=pltpu.CompilerParams(vmem_limit_bytes=VMEM_LIMIT),
        name="proj_%s%s" % ("o" if out is not None else "", "i" if inp is not None else ""),
    )(*args)


def _even_column_starts():
    bases = [0, A_WIDTH, 2 * A_WIDTH, A_SHIFT, A_SHIFT + A_WIDTH, A_SHIFT + A_WIDTH + B_WIDTH,
             A_SHIFT + A_WIDTH + 2 * B_WIDTH, A_SHIFT + A_WIDTH + 3 * B_WIDTH]
    return [(b + j * LANE, LANE) for j in range(N_PAIR) for b in bases] + [(3 * A_WIDTH, 4 * LORA)]


def _even_column_order():
    return np.concatenate([s + np.arange(w) for s, w in _even_column_starts()])


def _even_in_weight(w):
    return jnp.concatenate([w[:, s:s + n] for s, n in _even_column_starts()], axis=1).astype(bf16)


def _mixer_tables(p, mu_shift, lora_w2, w0, lora_a2, a0, k_k, k_a, r_k, lnx_g, lnx_b, conv_w, conv_b):
    pair = lambda v: v.reshape(-1, N_PAIR, LANE).transpose(1, 0, 2)
    mu = mu_shift[p]
    rows = [k_k[p][None], k_a[p][None], r_k[p].reshape(1, A_WIDTH), lnx_g[p][None], lnx_b[p][None],
            conv_b[p][None], conv_w[p], w0[p], a0[p],
            mu[None, :A_WIDTH], mu[None, A_WIDTH:2 * A_WIDTH], mu[None, 2 * A_WIDTH:3 * A_WIDTH]]
    vecs = pair(jnp.concatenate(rows, axis=0))
    assert vecs.shape == (N_PAIR, N_VEC, LANE)
    mats = pair(jnp.concatenate([lora_w2[p], lora_a2[p]], axis=0).reshape(4 * LORA, A_WIDTH))
    mats = mats.reshape(N_PAIR, 4, LORA, LANE)
    return vecs, mats, mu[None, 3 * A_WIDTH:]


V_KK, V_KA, V_RK, V_LNG, V_LNB, V_CB, V_CW, V_W0, V_A0, V_MU = 0, 1, 2, 3, 4, 5, 6, 9, 11, 13


def _scan_masks():
    ri = lax.broadcasted_iota(jnp.int32, (CHUNK, CHUNK), 0)
    ci = lax.broadcasted_iota(jnp.int32, (CHUNK, CHUNK), 1)
    eye = ri == ci
    ri2 = lax.broadcasted_iota(jnp.int32, (CHUNK, 2 * CHUNK), 0)
    ci2 = lax.broadcasted_iota(jnp.int32, (CHUNK, 2 * CHUNK), 1) & (CHUNK - 1)
    strict2 = (ri2 > ci2, ri2 < ci2)
    incl2 = (ri2 >= ci2, ri2 <= ci2)
    levels = []
    for sh in range(6):
        levels.append(((ri >> (sh + 1)) == (ci >> (sh + 1))) & ((ri >> sh) != (ci >> sh)))
    return eye, strict2, incl2, levels


def _mixer_kernel(*refs, L, NB, has_s0, write_s, has_prev, pair):
    it = iter(refs)
    x_ref, lo_ref, vec_ref, mat_ref, mulo = (next(it) for _ in range(5))
    s0 = next(it) if has_s0 else None
    if has_prev:
        next(it)
    mix = next(it)
    sout = next(it) if write_s else None
    r_s, k_s, v_s, lo_s, kd_s, y_s, rp_s, y0_s, mt_s, nt_s, st_s = it

    j = pl.program_id(1)
    vec = vec_ref[j]
    row = lambda i: vec[i:i + 1, :]
    col = lambda i: x_ref[:, i * LANE:(i + 1) * LANE]
    lora_w = [mat_ref[j, i].astype(bf16) for i in range(4)]

    R = NB * L
    nch, nch_seq = R // CHUNK, L // CHUNK
    rows = lax.broadcasted_iota(jnp.int32, (R, 1), 0)
    first, last = (rows & (L - 1)) == 0, (rows & (L - 1)) == L - 1

    def prev(f):
        return jnp.where(first, 0.0, pltpu.roll(f, 1, 0))

    def nxt(f):
        return jnp.where(last, 0.0, pltpu.roll(f, R - 1, 0))

    def tshift(f, mu):
        return f + mu * (0.5 * (prev(f) + nxt(f)) - f)

    bd = _head_block_diag()

    r_s[...] = tshift(col(0), row(V_MU))
    k_s[...] = tshift(col(1), row(V_MU + 1))
    v_s[...] = tshift(col(2), row(V_MU + 2))
    lo_s[...] = tshift(lo_ref[...], mulo[...])
    for q in range(NB):
        for d in range(2):
            st_s[q, d] = jnp.zeros((LANE, LANE), f32)
            if has_s0:
                for hh in range(2):
                    st_s[q, d, hh * A_HEAD:(hh + 1) * A_HEAD, hh * A_HEAD:(hh + 1) * A_HEAD] = s0[q, d, hh]

    eye, strict2, incl2, levels = _scan_masks()
    eye_f = eye.astype(f32)
    lane = lax.broadcasted_iota(jnp.int32, (1, LANE), 1)
    half = (lane < A_HEAD, lane >= A_HEAD)
    ri = lax.broadcasted_iota(jnp.int32, (LANE, LANE), 0)
    ci = lax.broadcasted_iota(jnp.int32, (LANE, LANE), 1)
    eye_pair = ri == ci
    crow = lax.broadcasted_iota(jnp.int32, (CHUNK, 1), 0)

    def setup_dots(c):
        rs = slice(c * CHUNK, (c + 1) * CHUNK)
        k2, lo = k_s[rs, :], lo_s[rs, :]
        kk = k2 * row(V_KK)
        st = dict(c=c, rs=rs, k2=k2, kk=kk, ssq=_head_sum(kk * kk, bd), wl=[], al=[])
        for d in range(2):
            lwd = lo[:, d * LORA:(d + 1) * LORA]
            lad = lo[:, 2 * LORA + d * LORA:2 * LORA + (d + 1) * LORA]
            st["wl"].append(_dot(jnp.tanh(lwd).astype(bf16), lora_w[d]))
            st["al"].append(_dot(lad.astype(bf16), lora_w[2 + d]))
        return st

    def setup_chains(st):
        c, rs, k2 = st["c"], st["rs"], st["k2"]
        r2, v2 = r_s[rs, :], v_s[rs, :]
        kap2 = st["kk"] / jnp.maximum(jnp.sqrt(st["ssq"]), 1e-12)
        groups = []
        for d in range(2):
            wl = row(V_W0 + d) + st["wl"][d]
            softplus = jnp.maximum(-wl, 0.0) + jnp.log(1.0 + jnp.exp(-jnp.abs(wl)))
            lw = -jnp.exp(-softplus - 0.5)
            g = lw
            for sh in range(6):
                stp = 1 << sh
                if d == 0:
                    g = g + jnp.where(crow >= stp, pltpu.roll(g, stp, 0), 0.0)
                else:
                    g = g + jnp.where(crow < CHUNK - stp, pltpu.roll(g, CHUNK - stp, 0), 0.0)
            a = _sigmoid(row(V_A0 + d) + st["al"][d])
            kd2 = k2 * (1.0 + (a - 1.0) * row(V_KA))
            kd_s[d, rs, :] = kd2
            b2 = a * kap2
            tot = g[CHUNK - 1:CHUNK, :] if d == 0 else g[0:1, :]
            e_ng = jnp.exp(-g)
            e_rem = jnp.exp(tot - g)
            kt2 = kap2 * jnp.exp(g - lw)
            rt2 = r2 * jnp.exp(g)
            lhs2 = jnp.concatenate([kt2, rt2], axis=0).astype(bf16)
            grp = dict(
                d=d, c=c, rs=rs, rt2=rt2, v2=v2, v2b=v2.astype(bf16), e_tot=jnp.exp(tot),
                kt2sw=pltpu.roll(kt2, A_HEAD, 1),
                rhs2=jnp.concatenate([kd2 * e_ng, b2 * e_ng], axis=0).astype(bf16),
                kbbb2=jnp.concatenate([kd2 * e_rem, b2 * e_rem], axis=0).astype(bf16), chains=[])
            for hh in range(2):
                ch = dict(grp=grp, d=d, hh=hh)
                lhs = jnp.where(half[hh], lhs2, jnp.zeros_like(lhs2))
                ch["a"] = _dot_nt(lhs, grp["rhs2"])
                grp["chains"].append(ch)
            groups.append(grp)
        return groups

    def chain_stages(groups):
        chains = [ch for grp in groups for ch in grp["chains"]]
        for ch in chains:
            top = jnp.where(strict2[ch["d"]], ch["a"][:CHUNK], 0.0)
            ch["bot"] = jnp.where(incl2[ch["d"]], ch["a"][CHUNK:], 0.0).astype(bf16)
            ch["akk"] = top[:, :CHUNK].astype(bf16)
            akb = top[:, CHUNK:]
            ch["ab"] = akb.astype(bf16)
            ch["t"] = eye_f - jnp.where(levels[0], akb, 0.0)
        yield
        for m in levels[1:]:
            for ch in chains:
                ch["tb"] = ch["t"].astype(bf16)
                off = jnp.where(m, ch["ab"], jnp.zeros_like(ch["ab"]))
                ch["p"] = _dot(off, ch["tb"]).astype(bf16)
            yield
            for ch in chains:
                ch["t"] = ch["t"] - _dot(ch["tb"], ch["p"])
            yield
        for ch in chains:
            ch["w0v"] = _dot(ch["akk"], ch["grp"]["v2b"])
        yield
        for ch in chains:
            rhs = jnp.where(half[ch["hh"]], ch["w0v"], ch["grp"]["kt2sw"])
            ch["x"] = _dot(ch["t"].astype(bf16), rhs.astype(bf16))
        yield
        for ch in chains:
            vtop = jnp.where(half[ch["hh"]], ch["grp"]["v2"], 0.0)
            ch["z"] = jnp.concatenate([vtop, -ch["x"]], axis=0).astype(bf16)
            ch["o"] = _dot(ch["bot"], ch["z"])
        yield
        for ch in chains:
            ch["tn"] = _dot_tn(ch["z"], ch["grp"]["kbbb2"])
        yield
        for grp in groups:
            d, rs = grp["d"], grp["rs"]
            c0, c1 = grp["chains"]
            y0_s[d, rs, :] = jnp.where(half[0], c0["o"], c1["o"])
            rp_s[d, rs, :] = grp["rt2"] + pltpu.roll(jnp.where(half[0], c1["o"], c0["o"]), A_HEAD, 1)
            ps = slice(grp["c"] * LANE, (grp["c"] + 1) * LANE)
            nt_s[d, ps, :] = jnp.concatenate([jnp.where(half[0], c0["tn"][:A_HEAD], 0.0),
                                              jnp.where(half[1], c1["tn"][A_HEAD:], 0.0)], axis=0)
            mt_s[d, ps, :] = jnp.where(eye_pair, grp["e_tot"], 0.0) + jnp.concatenate(
                [jnp.where(half[0], c0["tn"][A_HEAD:], 0.0), jnp.where(half[1], c1["tn"][:A_HEAD], 0.0)],
                axis=0)

    def seq_step(seqs, i):
        steps = []
        for q in seqs:
            for d in range(2):
                c = q * nch_seq + (i if d == 0 else nch_seq - 1 - i)
                steps.append((q, d, slice(c * CHUNK, (c + 1) * CHUNK), slice(c * LANE, (c + 1) * LANE),
                              st_s[q, d].astype(bf16)))
        for q, d, rs, ps, sb in steps:
            st_s[q, d] = _dot(sb, mt_s[d, ps, :].astype(bf16)) + nt_s[d, ps, :]
        for q, d, rs, ps, sb in steps:
            y_s[d, rs, :] = y0_s[d, rs, :] + _dot_nt(rp_s[d, rs, :].astype(bf16), sb)

    def finish_a(c, fin):
        rs = slice(c * CHUNK, (c + 1) * CHUNK)
        y = y_s[0, rs, :] + y_s[1, rs, :]
        kmean = 0.5 * (kd_s[0, rs, :] + kd_s[1, rs, :])
        fin[c] = dict(rs=rs, y=y, ysum=_head_sum(y, bd), bsum=_head_sum(r_s[rs, :] * kmean * row(V_RK), bd))

    def finish_b(c, fin):
        f = fin[c]
        f["yc"] = f["y"] - f["ysum"] * (1.0 / A_HEAD)
        f["vsum"] = _head_sum(f["yc"] * f["yc"], bd)

    def finish_c(c, fin):
        f = fin[c]
        rs = f["rs"]
        yn = f["yc"] * lax.rsqrt(f["vsum"] * (1.0 / A_HEAD) + GN_EPS) * row(V_LNG) + row(V_LNB)
        mix[rs, :LANE] = (yn + f["bsum"] * v_s[rs, :]) * _silu(x_ref[rs, 3 * LANE:4 * LANE])

    def finish_fillers(chunks, fin):
        chunks = list(chunks)
        out = []
        for t in range(len(chunks) + 2):
            slot = []
            if t < len(chunks):
                slot.append(functools.partial(finish_a, chunks[t], fin))
            if 0 <= t - 1 < len(chunks):
                slot.append(functools.partial(finish_b, chunks[t - 1], fin))
            if 0 <= t - 2 < len(chunks):
                slot.append(functools.partial(finish_c, chunks[t - 2], fin))
            out.append(slot)
        return out

    def conv_branch():
        u = col(5) * col(6)
        conv = prev(u) * row(V_CW) + u * row(V_CW + 1) + nxt(u) * row(V_CW + 2) + row(V_CB)
        mix[:, LANE:] = col(4) * conv * _silu(col(7))

    def run(stages, slots):
        slots = list(slots)
        for _ in stages:
            if slots:
                for f in slots.pop(0):
                    f()
        for slot in slots:
            for f in slot:
                f()

    hc = nch // 2
    dots = [setup_dots(c) for c in range(hc)]
    first_half = [grp for st in dots for grp in setup_chains(st)]
    second_half, pending = [], {}
    slots = []
    for t in range(hc + 1):
        slot = []
        if t < hc:
            slot.append(functools.partial(lambda c: pending.__setitem__(c, setup_dots(c)), hc + t))
        if t >= 1:
            slot.append(functools.partial(lambda c: second_half.extend(setup_chains(pending[c])), hc + t - 1))
        slots.append(slot)
    run(chain_stages(first_half), slots)
    fin = {}
    if NB >= 2:
        early = list(range(NB // 2))
        late = list(range(NB // 2, NB))
        slots = [[conv_branch]] + [[functools.partial(seq_step, early, i)] for i in range(nch_seq)]
        slots += finish_fillers(range(hc), fin)
        run(chain_stages(second_half), slots)
        for i in range(nch_seq):
            seq_step(late, i)
        run(iter(()), finish_fillers(range(hc, nch), fin))
    else:
        run(chain_stages(second_half), [[conv_branch]])
        ready = []
        for i in range(nch_seq):
            seq_step([0], i)
            if i >= nch_seq // 2:
                ready += [i, nch_seq - 1 - i]
            for c in list(ready):
                stage = fin.get(c, {}).get("stage", 0)
                (finish_a, finish_b, finish_c)[stage](c, fin)
                fin[c]["stage"] = stage + 1
                if stage == 2:
                    ready.remove(c)
        while ready:
            for c in list(ready):
                stage = fin[c]["stage"]
                (finish_a, finish_b, finish_c)[stage](c, fin)
                fin[c]["stage"] = stage + 1
                if stage == 2:
                    ready.remove(c)

    if write_s:
        if not has_prev:
            for pp in range(N_PAIRS):
                if pp != pair:
                    sout[:, pp] = jnp.zeros((NB, 2, 2, A_HEAD, A_HEAD), f32)
        own = sout if has_prev else sout.at[:, pair]
        for q in range(NB):
            for d in range(2):
                for hh in range(2):
                    own[q, d, hh] = st_s[q, d, hh * A_HEAD:(hh + 1) * A_HEAD, hh * A_HEAD:(hh + 1) * A_HEAD]


def _mixer(proj, tables, s0, pair, *, L, n_seq, write_s, s_prev=None):
    vecs, mats, mulo = tables
    has_s0 = s0 is not None
    R = MIXER_ROWS
    nb = R // L
    in_specs = [
        pl.BlockSpec((R, PAIR_COLS), lambda s, j: (s, j)),
        pl.BlockSpec((R, 4 * LORA), lambda s, j: (s, N_PAIR * PAIR_COLS // (4 * LORA))),
        pl.BlockSpec((N_PAIR, N_VEC, LANE), lambda s, j: (0, 0, 0)),
        pl.BlockSpec((N_PAIR, 4, LORA, LANE), lambda s, j: (0, 0, 0, 0)),
        pl.BlockSpec((1, 4 * LORA), lambda s, j: (0, 0)),
    ]
    args = [proj, proj, vecs, mats, mulo]
    if has_s0:
        in_specs.append(pl.BlockSpec((nb, None, 2, 2, A_HEAD, A_HEAD), lambda s, j: (s, pair, 0, j, 0, 0)))
        args.append(s0)
    out_specs = [pl.BlockSpec((R, 2 * LANE), lambda s, j: (s, j))]
    out_shape = [jax.ShapeDtypeStruct((n_seq * L, D_MODEL), f32)]
    aliases = {}
    if write_s:
        out_shape.append(jax.ShapeDtypeStruct((n_seq, N_PAIRS, 2, A_HEADS, A_HEAD, A_HEAD), f32))
        if s_prev is None:
            out_specs.append(pl.BlockSpec((nb, N_PAIRS, 2, 2, A_HEAD, A_HEAD), lambda s, j: (s, 0, 0, j, 0, 0)))
        else:
            out_specs.append(pl.BlockSpec((nb, None, 2, 2, A_HEAD, A_HEAD), lambda s, j: (s, pair, 0, j, 0, 0)))
            aliases = {len(args): 1}
            in_specs.append(pl.BlockSpec(memory_space=pl.ANY))
            args.append(s_prev)
    tok = lambda: pltpu.VMEM((R, LANE), f32)
    tok2 = lambda: pltpu.VMEM((2, R, LANE), f32)
    maps = lambda: pltpu.VMEM((2, 2 * R, LANE), f32)
    scratch = [tok(), tok(), tok(), pltpu.VMEM((R, 4 * LORA), f32), tok2(), tok2(), tok2(), tok2(),
               maps(), maps(), pltpu.VMEM((nb, 2, LANE, LANE), f32)]
    return pl.pallas_call(
        functools.partial(_mixer_kernel, L=L, NB=nb, has_s0=has_s0, write_s=write_s, has_prev=bool(aliases),
                          pair=pair),
        grid=(n_seq // nb, N_PAIR),
        in_specs=in_specs,
        out_specs=out_specs,
        out_shape=out_shape,
        scratch_shapes=scratch,
        input_output_aliases=aliases,
        compiler_params=pltpu.CompilerParams(vmem_limit_bytes=VMEM_LIMIT),
        name="rwkv_conv_mixer_L%d" % L,
    )(*args)


def _attn_kernel(*refs, LQ, LN, NB, NQ, PAST, rope, write_cache, n_prev, pair):
    it = iter(refs)
    q_ref, k_ref, v_ref, g_ref, qg_ref, kg_ref = (next(it) for _ in range(6))
    if rope:
        cosq, sinq, cosk, sink = (next(it) for _ in range(4))
    if PAST:
        ck, cv = next(it), next(it)
    for _ in range(n_prev):
        next(it)
    out_ref = next(it)
    if write_cache:
        nk_ref, nv_ref = next(it), next(it)
    kbuf, vbuf, obuf = it

    bd = _head_block_diag()
    lane = lax.broadcasted_iota(jnp.int32, (1, LANE), 1)
    first_half = (lane & 31) < 16

    def rms(x, g):
        return x * lax.rsqrt(_head_sum(x * x, bd, pieces=1) * (1.0 / C_HEAD) + EPS) * g

    def rotate(x, cos, sin):
        outs = []
        for i in range(x.shape[1] // LANE):
            xb = x[:, i * LANE:(i + 1) * LANE]
            sw = jnp.where(first_half, pltpu.roll(xb, LANE - 16, 1), pltpu.roll(xb, 16, 1))
            outs.append(xb * cos + sw * sin)
        return jnp.concatenate(outs, axis=1)

    def build_keys():
        kn = rms(k_ref[...], kg_ref[...])
        vx = v_ref[...]
        if write_cache:
            nk_own, nv_own = nk_ref, nv_ref
            if not n_prev:
                for pp in range(N_PAIRS):
                    if pp != pair:
                        nk_ref[:, pp] = jnp.zeros((NB, C_KV_HEADS, LN, C_HEAD), f32)
                        nv_ref[:, pp] = jnp.zeros((NB, C_KV_HEADS, LN, C_HEAD), f32)
                nk_own, nv_own = nk_ref.at[:, pair], nv_ref.at[:, pair]
            for b in range(NB):
                for h in range(C_KV_HEADS):
                    nk_own[b, h] = kn[b * LN:(b + 1) * LN, h * C_HEAD:(h + 1) * C_HEAD]
                    nv_own[b, h] = vx[b * LN:(b + 1) * LN, h * C_HEAD:(h + 1) * C_HEAD]
        if rope:
            kn = rotate(kn, cosk[...], sink[...])
        for b in range(NB):
            for h in range(C_KV_HEADS):
                if PAST:
                    kbuf[b, h, :PAST, :] = ck[b, h].astype(bf16)
                    vbuf[b, h, :PAST, :] = cv[b, h].astype(bf16)
                kbuf[b, h, PAST:, :] = kn[b * LN:(b + 1) * LN, h * C_HEAD:(h + 1) * C_HEAD].astype(bf16)
                vbuf[b, h, PAST:, :] = vx[b * LN:(b + 1) * LN, h * C_HEAD:(h + 1) * C_HEAD].astype(bf16)

    if NQ == 1:
        build_keys()
    else:
        pl.when(pl.program_id(1) == 0)(build_keys)

    qn = rms(q_ref[...], qg_ref[...])
    if rope:
        qn = rotate(qn, cosq[...], sinq[...])
    qn = (qn * (C_HEAD ** -0.5 * LOG2E)).astype(bf16)
    units = [(b, h) for b in range(NB) for h in range(C_KV_HEADS)]
    heads = [[slice((h * C_GROUP + g) * C_HEAD, (h * C_GROUP + g + 1) * C_HEAD) for g in range(C_GROUP)]
             for h in range(C_KV_HEADS)]

    def scores(b, h):
        q4 = jnp.concatenate([qn[b * LQ:(b + 1) * LQ, sl] for sl in heads[h]], axis=0)
        return _dot_nt(q4, kbuf[b, h])

    s_next = scores(*units[0])
    for n, (b, h) in enumerate(units):
        s = s_next
        if n + 1 < len(units):
            s_next = scores(*units[n + 1])
        p = jnp.exp2(s - jnp.max(s, axis=-1, keepdims=True))
        o = _dot(p.astype(bf16), vbuf[b, h]) / jnp.sum(p, axis=-1, keepdims=True)
        for g, sl in enumerate(heads[h]):
            obuf[b * LQ:(b + 1) * LQ, sl] = o[g * LQ:(g + 1) * LQ]
    out_ref[...] = obuf[...] * _silu(g_ref[...])


def _rope_tables():
    pos = np.arange(DEC_SEQ)
    inv = ROPE_THETA ** (-np.arange(0, ROPE_HALF, 2, dtype=np.float32) / ROPE_HALF)
    ang_r = (pos // GRID_W).astype(np.float32)[:, None] * inv
    ang_c = (pos % GRID_W).astype(np.float32)[:, None] * inv
    cos64 = np.concatenate([np.cos(ang_r), np.cos(ang_r), np.cos(ang_c), np.cos(ang_c)], axis=1)
    sin64 = np.concatenate([-np.sin(ang_r), np.sin(ang_r), -np.sin(ang_c), np.sin(ang_c)], axis=1)
    return jnp.asarray(np.tile(cos64, (1, 2)), f32), jnp.asarray(np.tile(sin64, (1, 2)), f32)


def _attention(proj, q_g, k_g, cache_k, cache_v, pair, tables, *, n_seq, L, sample, kv_prev=None):
    LQ = ATTN_ROWS
    nq = L // LQ
    nb = 1 if sample else ATTN_PROMPT_SEQS
    assert nb == 1 or nq == 1
    past = PAST_LEN if sample else 0
    in_specs = [
        pl.BlockSpec((nb * LQ, D_MODEL), lambda b, i: (b * nq + i, 0)),
        pl.BlockSpec((nb * L, KV_WIDTH), lambda b, i: (b, 2 * D_MODEL // KV_WIDTH)),
        pl.BlockSpec((nb * L, KV_WIDTH), lambda b, i: (b, 2 * D_MODEL // KV_WIDTH + 1)),
        pl.BlockSpec((nb * LQ, D_MODEL), lambda b, i: (b * nq + i, 1)),
        pl.BlockSpec((1, D_MODEL), lambda b, i: (0, 0)),
        pl.BlockSpec((1, KV_WIDTH), lambda b, i: (0, 0)),
    ]
    args = [proj, proj, proj, proj, q_g, k_g]
    if sample:
        cos, sin = tables
        in_specs += [pl.BlockSpec((LQ, LANE), lambda b, i: (i, 0)),
                     pl.BlockSpec((LQ, LANE), lambda b, i: (i, 0)),
                     pl.BlockSpec((L, LANE), lambda b, i: (0, 0)),
                     pl.BlockSpec((L, LANE), lambda b, i: (0, 0))]
        args += [cos, sin, cos, sin]
        cspec = pl.BlockSpec((nb, None, C_KV_HEADS, PAST_LEN, C_HEAD), lambda b, i: (b, pair, 0, 0, 0))
        in_specs += [cspec, cspec]
        args += [cache_k, cache_v]
    out_specs = [pl.BlockSpec((nb * LQ, D_MODEL), lambda b, i: (b * nq + i, 0))]
    out_shape = [jax.ShapeDtypeStruct((n_seq * L, D_MODEL), f32)]
    aliases = {}
    if not sample:
        if kv_prev is None:
            nspec = pl.BlockSpec((nb, N_PAIRS, C_KV_HEADS, L, C_HEAD), lambda b, i: (b, 0, 0, 0, 0))
        else:
            nspec = pl.BlockSpec((nb, None, C_KV_HEADS, L, C_HEAD), lambda b, i: (b, pair, 0, 0, 0))
        out_specs += [nspec, nspec]
        out_shape += [jax.ShapeDtypeStruct((n_seq, N_PAIRS, C_KV_HEADS, L, C_HEAD), f32)] * 2
        if kv_prev is not None:
            aliases = {len(args): 1, len(args) + 1: 2}
            in_specs += [pl.BlockSpec(memory_space=pl.ANY)] * 2
            args += list(kv_prev)
    return pl.pallas_call(
        functools.partial(_attn_kernel, LQ=LQ, LN=L, NB=nb, NQ=nq, PAST=past, rope=sample, write_cache=not sample,
                          n_prev=len(aliases), pair=pair),
        grid=(n_seq // nb, nq),
        in_specs=in_specs,
        out_specs=out_specs,
        out_shape=out_shape,
        input_output_aliases=aliases,
        scratch_shapes=[pltpu.VMEM((nb, C_KV_HEADS, past + L, C_HEAD), bf16),
                        pltpu.VMEM((nb, C_KV_HEADS, past + L, C_HEAD), bf16),
                        pltpu.VMEM((nb * LQ, D_MODEL), f32)],
        compiler_params=pltpu.CompilerParams(vmem_limit_bytes=VMEM_LIMIT),
        name="gqa_sample" if sample else "gqa_prompt",
    )(*args)


def kernel(x_prompt, x_sample, c, state_rwkv, cache_k, cache_v, c_ctx, w_ada, b_ada, norm_g, final_g,
           w_in_e, mu_shift, lora_w2, w0, lora_a2, a0, k_k, k_a, r_k, lnx_g, lnx_b, conv_w, conv_b,
           w_out_e, w_in_o, q_norm_g, k_norm_g, w_out_o):
    xs = [x_prompt.reshape(STREAM_TOK, D_MODEL), x_sample.reshape(STREAM_TOK, D_MODEL)]
    cond = jnp.concatenate([c_ctx[None, :], c, jnp.zeros((N_MOD - 1 - DEC_BATCH, D_MODEL), f32)], axis=0)
    mod = _adaln(cond, w_ada, b_ada).reshape(DEPTH, N_MOD, 1, 3 * D_MODEL)

    rope = _rope_tables()
    final_g2 = final_g.reshape(1, D_MODEL)

    w_ins, w_outs = [], []
    for layer in range(DEPTH):
        p = layer // 2
        if layer % 2 == 0:
            w_ins.append(_even_in_weight(w_in_e[p]))
            wo = w_out_e[p]
            w_outs.append(jnp.concatenate(
                [wo[h * A_WIDTH + j * LANE:h * A_WIDTH + (j + 1) * LANE] for j in range(N_PAIR) for h in range(2)],
                axis=0).astype(bf16))
        else:
            wi = w_in_o[p]
            w_ins.append(jnp.concatenate([wi[:, :D_MODEL], wi[:, D_MODEL + 2 * KV_WIDTH:],
                                          wi[:, D_MODEL:D_MODEL + 2 * KV_WIDTH]], axis=1).astype(bf16))
            w_outs.append(w_out_o[p].astype(bf16))
    inp = lambda layer: (mod[layer], norm_g[layer].reshape(1, D_MODEL), w_ins[layer])

    new_rwkv = new_kv = None
    (proj_p,) = _proj(xs[0], sample=False, inp=inp(0))
    (proj_s,) = _proj(xs[1], sample=True, inp=inp(0))
    for layer in range(DEPTH):
        p = layer // 2
        if layer % 2 == 0:
            tables = _mixer_tables(p, mu_shift, lora_w2, w0, lora_a2, a0, k_k, k_a, r_k, lnx_g, lnx_b,
                                   conv_w, conv_b)
            act_p, new_rwkv = _mixer(proj_p, tables, None, p, L=SEQ, n_seq=BATCH, write_s=True, s_prev=new_rwkv)
            (act_s,) = _mixer(proj_s, tables, state_rwkv, p, L=DEC_SEQ, n_seq=DEC_BATCH, write_s=False)
        else:
            q_g = jnp.tile(q_norm_g[p], C_Q_HEADS).reshape(1, D_MODEL)
            k_g = jnp.tile(k_norm_g[p], C_KV_HEADS).reshape(1, KV_WIDTH)
            act_p, *new_kv = _attention(proj_p, q_g, k_g, None, None, p, None, n_seq=BATCH, L=SEQ,
                                        sample=False, kv_prev=new_kv)
            (act_s,) = _attention(proj_s, q_g, k_g, cache_k, cache_v, p, rope, n_seq=DEC_BATCH,
                                  L=DEC_SEQ, sample=True)
        if layer < DEPTH - 1:
            xs[0], proj_p = _proj(xs[0], sample=False, out=(act_p, mod[layer], w_outs[layer]), inp=inp(layer + 1))
            xs[1], proj_s = _proj(xs[1], sample=True, out=(act_s, mod[layer], w_outs[layer]), inp=inp(layer + 1))
        else:
            (xs[0],) = _proj(xs[0], sample=False, out=(act_p, mod[layer], w_outs[layer]), final_g=final_g2)
            (xs[1],) = _proj(xs[1], sample=True, out=(act_s, mod[layer], w_outs[layer]), final_g=final_g2)
    return (xs[0].reshape(BATCH, SEQ, D_MODEL), xs[1].reshape(DEC_BATCH, DEC_SEQ, D_MODEL),
            new_rwkv, new_kv[0], new_kv[1])
```

```python
import functools

import numpy as np
import jax
import jax.numpy as jnp
from jax import lax
from jax.experimental import pallas as pl
from jax.experimental.pallas import tpu as pltpu

f32 = jnp.float32
bf16 = jnp.bfloat16
HI = lax.Precision.HIGHEST

D_MODEL = 1024
BATCH = 32
SEQ = 256
DEPTH = 4
DEC_BATCH = 8
DEC_SEQ = 1024
PAST_LEN = 512
GRID_W = 64
EPS = 1e-6
A_WIDTH = 512
A_HEAD = 64
A_HEADS = 8
LORA = 64
GN_EPS = 64e-5
B_WIDTH = 512
C_HEAD = 64
C_Q_HEADS = 16
C_KV_HEADS = 4
C_GROUP = 4
KV_WIDTH = 256
ROPE_THETA = 10000.0
ROPE_HALF = 32
A_SHIFT = 3 * A_WIDTH + 4 * LORA
EVEN_IN = A_SHIFT + A_WIDTH + 4 * B_WIDTH
ODD_IN = 2 * D_MODEL + 2 * KV_WIDTH

STREAM_TOK = BATCH * SEQ
assert STREAM_TOK == DEC_BATCH * DEC_SEQ
PROJ_ROWS = 512
ATTN_ROWS = 256
ATTN_PROMPT_SEQS = 4
LOG2E = 1.4426950408889634
N_MOD = 16
CHUNK = 64
LANE = 128
VMEM_LIMIT = 56 * 1024 * 1024
N_PAIR = A_WIDTH // LANE
PAIR_COLS = 8 * LANE
N_VEC = 16
MIXER_ROWS = 1024
N_PAIRS = DEPTH // 2


def _sigmoid(x):
    return 1.0 / (1.0 + jnp.exp(-x))


def _silu(x):
    return x * _sigmoid(x)


def _dot(a, b, precision=None):
    return jnp.dot(a, b, precision=precision, preferred_element_type=f32)


def _dot_nt(a, b, precision=None):
    return lax.dot_general(a, b, (((1,), (1,)), ((), ())), precision=precision,
                           preferred_element_type=f32)


def _dot_tn(a, b, precision=None):
    return lax.dot_general(a, b, (((0,), (0,)), ((), ())), precision=precision,
                           preferred_element_type=f32)


def _mod_row(sample):
    if sample:
        return lambda i: 1 + i // (DEC_SEQ // PROJ_ROWS)
    return lambda i: 0


def _head_block_diag():
    ri = lax.broadcasted_iota(jnp.int32, (LANE, LANE), 0) >> 6
    ci = lax.broadcasted_iota(jnp.int32, (LANE, LANE), 1) >> 6
    return (ri == ci).astype(bf16)


def _head_sum(x, bd, pieces=2):
    n = x.shape[1] // LANE
    hi = x.astype(bf16)
    cols = [_dot(hi[:, i * LANE:(i + 1) * LANE], bd) for i in range(n)]
    if pieces == 2:
        lo = (x - hi.astype(f32)).astype(bf16)
        cols = [c + _dot(lo[:, i * LANE:(i + 1) * LANE], bd) for i, c in enumerate(cols)]
    return cols[0] if n == 1 else jnp.concatenate(cols, axis=1)


def _adaln_kernel(cond_ref, w_ref, b_ref, o_ref):
    o_ref[...] = _dot(_silu(cond_ref[...]), w_ref[...], HI) + b_ref[...]


def _adaln(cond, w_ada, b_ada):
    tn = 768
    return pl.pallas_call(
        _adaln_kernel,
        grid=(DEPTH, 3 * D_MODEL // tn),
        in_specs=[
            pl.BlockSpec((N_MOD, D_MODEL), lambda l, n: (0, 0)),
            pl.BlockSpec((None, D_MODEL, tn), lambda l, n: (l, 0, n)),
            pl.BlockSpec((None, 1, tn), lambda l, n: (l, 0, n)),
        ],
        out_specs=pl.BlockSpec((None, N_MOD, tn), lambda l, n: (l, 0, n)),
        out_shape=jax.ShapeDtypeStruct((DEPTH, N_MOD, 3 * D_MODEL), f32),
        compiler_params=pltpu.CompilerParams(vmem_limit_bytes=VMEM_LIMIT),
        name="adaln",
    )(cond, w_ada, b_ada.reshape(DEPTH, 1, 3 * D_MODEL))


def _proj_kernel(*refs, has_out, has_in, final):
    it = iter(refs)
    if has_out:
        a_ref, x_ref, modp_ref, wo_ref = (next(it) for _ in range(4))
        fg_ref = next(it) if final else None
    else:
        x_ref = next(it)
    if has_in:
        modn_ref, g_ref, wi_ref = (next(it) for _ in range(3))
    x = x_ref[...]
    if has_out:
        x = x + modp_ref[:, 2 * D_MODEL:] * _dot(a_ref[...].astype(bf16), wo_ref[...])
        xo_ref = next(it)
        if final:
            xo_ref[...] = x * lax.rsqrt(jnp.mean(x * x, axis=-1, keepdims=True) + EPS) * fg_ref[...]
        else:
            xo_ref[...] = x
    if has_in:
        p_ref = next(it)
        y = x * lax.rsqrt(jnp.mean(x * x, axis=-1, keepdims=True) + EPS) * g_ref[...]
        m = modn_ref[...]
        h = y * (1.0 + m[:, D_MODEL:2 * D_MODEL]) + m[:, :D_MODEL]
        p_ref[...] = _dot(h.astype(bf16), wi_ref[...])


def _proj(x, *, sample, out=None, inp=None, final_g=None):
    mrow = _mod_row(sample)
    tm = PROJ_ROWS
    rows = lambda w: pl.BlockSpec((tm, w), lambda i: (i, 0))
    const = lambda shape: pl.BlockSpec(shape, lambda i: (0,) * len(shape), pipeline_mode=pl.Buffered(1))
    modspec = pl.BlockSpec((None, 1, 3 * D_MODEL), lambda i: (mrow(i), 0, 0))
    in_specs, args, out_specs, out_shape = [], [], [], []
    if out is not None:
        act, mod_prev, w_out = out
        in_specs += [rows(D_MODEL), rows(D_MODEL), modspec, const((D_MODEL, D_MODEL))]
        args += [act, x, mod_prev, w_out]
        if final_g is not None:
            in_specs.append(const((1, D_MODEL)))
            args.append(final_g)
        out_specs.append(rows(D_MODEL))
        out_shape.append(jax.ShapeDtypeStruct((STREAM_TOK, D_MODEL), f32))
    else:
        in_specs.append(rows(D_MODEL))
        args.append(x)
    if inp is not None:
        mod_next, g, w_in = inp
        n = w_in.shape[1]
        in_specs += [modspec, const((1, D_MODEL)), const((D_MODEL, n))]
        args += [mod_next, g, w_in]
        out_specs.append(rows(n))
        out_shape.append(jax.ShapeDtypeStruct((STREAM_TOK, n), f32))
    return pl.pallas_call(
        functools.partial(_proj_kernel, has_out=out is not None, has_in=inp is not None,
                          final=final_g is not None),
        grid=(STREAM_TOK // tm,),
        in_specs=in_specs,
        out_specs=out_specs,
        out_shape=out_shape,
        compiler_params=pltpu.CompilerParams(vmem_limit_bytes=VMEM_LIMIT),
        name="proj_%s%s" % ("o" if out is not None else "", "i" if inp is not None else ""),
    )(*args)


def _even_column_starts():
    bases = [0, A_WIDTH, 2 * A_WIDTH, A_SHIFT, A_SHIFT + A_WIDTH, A_SHIFT + A_WIDTH + B_WIDTH,
             A_SHIFT + A_WIDTH + 2 * B_WIDTH, A_SHIFT + A_WIDTH + 3 * B_WIDTH]
    return [(b + j * LANE, LANE) for j in range(N_PAIR) for b in bases] + [(3 * A_WIDTH, 4 * LORA)]


def _even_column_order():
    return np.concatenate([s + np.arange(w) for s, w in _even_column_starts()])


def _even_in_weight(w):
    return jnp.concatenate([w[:, s:s + n] for s, n in _even_column_starts()], axis=1).astype(bf16)


def _mixer_tables(p, mu_shift, lora_w2, w0, lora_a2, a0, k_k, k_a, r_k, lnx_g, lnx_b, conv_w, conv_b):
    pair = lambda v: v.reshape(-1, N_PAIR, LANE).transpose(1, 0, 2)
    mu = mu_shift[p]
    rows = [k_k[p][None], k_a[p][None], r_k[p].reshape(1, A_WIDTH), lnx_g[p][None], lnx_b[p][None],
            conv_b[p][None], conv_w[p], w0[p], a0[p],
            mu[None, :A_WIDTH], mu[None, A_WIDTH:2 * A_WIDTH], mu[None, 2 * A_WIDTH:3 * A_WIDTH]]
    vecs = pair(jnp.concatenate(rows, axis=0))
    assert vecs.shape == (N_PAIR, N_VEC, LANE)
    mats = pair(jnp.concatenate([lora_w2[p], lora_a2[p]], axis=0).reshape(4 * LORA, A_WIDTH))
    mats = mats.reshape(N_PAIR, 4, LORA, LANE)
    return vecs, mats, mu[None, 3 * A_WIDTH:]


V_KK, V_KA, V_RK, V_LNG, V_LNB, V_CB, V_CW, V_W0, V_A0, V_MU = 0, 1, 2, 3, 4, 5, 6, 9, 11, 13


def _scan_masks():
    ri = lax.broadcasted_iota(jnp.int32, (CHUNK, CHUNK), 0)
    ci = lax.broadcasted_iota(jnp.int32, (CHUNK, CHUNK), 1)
    eye = ri == ci
    ri2 = lax.broadcasted_iota(jnp.int32, (CHUNK, 2 * CHUNK), 0)
    ci2 = lax.broadcasted_iota(jnp.int32, (CHUNK, 2 * CHUNK), 1) & (CHUNK - 1)
    strict2 = (ri2 > ci2, ri2 < ci2)
    incl2 = (ri2 >= ci2, ri2 <= ci2)
    levels = []
    for sh in range(6):
        levels.append(((ri >> (sh + 1)) == (ci >> (sh + 1))) & ((ri >> sh) != (ci >> sh)))
    return eye, strict2, incl2, levels


def _mixer_kernel(*refs, L, NB, has_s0, write_s, has_prev, pair):
    it = iter(refs)
    x_ref, lo_ref, vec_ref, mat_ref, mulo = (next(it) for _ in range(5))
    s0 = next(it) if has_s0 else None
    if has_prev:
        next(it)
    mix = next(it)
    sout = next(it) if write_s else None
    r_s, k_s, v_s, lo_s, kd_s, y_s, rp_s, y0_s, mt_s, nt_s, st_s = it

    j = pl.program_id(1)
    vec = vec_ref[j]
    row = lambda i: vec[i:i + 1, :]
    col = lambda i: x_ref[:, i * LANE:(i + 1) * LANE]
    lora_w = [mat_ref[j, i].astype(bf16) for i in range(4)]

    R = NB * L
    nch, nch_seq = R // CHUNK, L // CHUNK
    rows = lax.broadcasted_iota(jnp.int32, (R, 1), 0)
    first, last = (rows & (L - 1)) == 0, (rows & (L - 1)) == L - 1

    def prev(f):
        return jnp.where(first, 0.0, pltpu.roll(f, 1, 0))

    def nxt(f):
        return jnp.where(last, 0.0, pltpu.roll(f, R - 1, 0))

    def tshift(f, mu):
        return f + mu * (0.5 * (prev(f) + nxt(f)) - f)

    bd = _head_block_diag()

    r_s[...] = tshift(col(0), row(V_MU))
    k_s[...] = tshift(col(1), row(V_MU + 1))
    v_s[...] = tshift(col(2), row(V_MU + 2))
    lo_s[...] = tshift(lo_ref[...], mulo[...])
    for q in range(NB):
        for d in range(2):
            st_s[q, d] = jnp.zeros((LANE, LANE), f32)
            if has_s0:
                for hh in range(2):
                    st_s[q, d, hh * A_HEAD:(hh + 1) * A_HEAD, hh * A_HEAD:(hh + 1) * A_HEAD] = s0[q, d, hh]

    eye, strict2, incl2, levels = _scan_masks()
    eye_f = eye.astype(f32)
    lane = lax.broadcasted_iota(jnp.int32, (1, LANE), 1)
    half = (lane < A_HEAD, lane >= A_HEAD)
    ri = lax.broadcasted_iota(jnp.int32, (LANE, LANE), 0)
    ci = lax.broadcasted_iota(jnp.int32, (LANE, LANE), 1)
    eye_pair = ri == ci
    crow = lax.broadcasted_iota(jnp.int32, (CHUNK, 1), 0)

    def setup_dots(c):
        rs = slice(c * CHUNK, (c + 1) * CHUNK)
        k2, lo = k_s[rs, :], lo_s[rs, :]
        kk = k2 * row(V_KK)
        st = dict(c=c, rs=rs, k2=k2, kk=kk, ssq=_head_sum(kk * kk, bd), wl=[], al=[])
        for d in range(2):
            lwd = lo[:, d * LORA:(d + 1) * LORA]
            lad = lo[:, 2 * LORA + d * LORA:2 * LORA + (d + 1) * LORA]
            st["wl"].append(_dot(jnp.tanh(lwd).astype(bf16), lora_w[d]))
            st["al"].append(_dot(lad.astype(bf16), lora_w[2 + d]))
        return st

    def setup_chains(st):
        c, rs, k2 = st["c"], st["rs"], st["k2"]
        r2, v2 = r_s[rs, :], v_s[rs, :]
        kap2 = st["kk"] / jnp.maximum(jnp.sqrt(st["ssq"]), 1e-12)
        groups = []
        for d in range(2):
            wl = row(V_W0 + d) + st["wl"][d]
            softplus = jnp.maximum(-wl, 0.0) + jnp.log(1.0 + jnp.exp(-jnp.abs(wl)))
            lw = -jnp.exp(-softplus - 0.5)
            g = lw
            for sh in range(6):
                stp = 1 << sh
                if d == 0:
                    g = g + jnp.where(crow >= stp, pltpu.roll(g, stp, 0), 0.0)
                else:
                    g = g + jnp.where(crow < CHUNK - stp, pltpu.roll(g, CHUNK - stp, 0), 0.0)
            a = _sigmoid(row(V_A0 + d) + st["al"][d])
            kd2 = k2 * (1.0 + (a - 1.0) * row(V_KA))
            kd_s[d, rs, :] = kd2
            b2 = a * kap2
            tot = g[CHUNK - 1:CHUNK, :] if d == 0 else g[0:1, :]
            e_ng = jnp.exp(-g)
            e_rem = jnp.exp(tot - g)
            kt2 = kap2 * jnp.exp(g - lw)
            rt2 = r2 * jnp.exp(g)
            lhs2 = jnp.concatenate([kt2, rt2], axis=0).astype(bf16)
            grp = dict(
                d=d, c=c, rs=rs, rt2=rt2, v2=v2, v2b=v2.astype(bf16), e_tot=jnp.exp(tot),
                kt2sw=pltpu.roll(kt2, A_HEAD, 1),
                rhs2=jnp.concatenate([kd2 * e_ng, b2 * e_ng], axis=0).astype(bf16),
                kbbb2=jnp.concatenate([kd2 * e_rem, b2 * e_rem], axis=0).astype(bf16), chains=[])
            for hh in range(2):
                ch = dict(grp=grp, d=d, hh=hh)
                lhs = jnp.where(half[hh], lhs2, jnp.zeros_like(lhs2))
                ch["a"] = _dot_nt(lhs, grp["rhs2"])
                grp["chains"].append(ch)
            groups.append(grp)
        return groups

    def chain_stages(groups):
        chains = [ch for grp in groups for ch in grp["chains"]]
        for ch in chains:
            top = jnp.where(strict2[ch["d"]], ch["a"][:CHUNK], 0.0)
            ch["bot"] = jnp.where(incl2[ch["d"]], ch["a"][CHUNK:], 0.0).astype(bf16)
            ch["akk"] = top[:, :CHUNK].astype(bf16)
            akb = top[:, CHUNK:]
            ch["ab"] = akb.astype(bf16)
            ch["t"] = eye_f - jnp.where(levels[0], akb, 0.0)
        yield
        for m in levels[1:]:
            for ch in chains:
                ch["tb"] = ch["t"].astype(bf16)
                off = jnp.where(m, ch["ab"], jnp.zeros_like(ch["ab"]))
                ch["p"] = _dot(off, ch["tb"]).astype(bf16)
            yield
            for ch in chains:
                ch["t"] = ch["t"] - _dot(ch["tb"], ch["p"])
            yield
        for ch in chains:
            ch["w0v"] = _dot(ch["akk"], ch["grp"]["v2b"])
        yield
        for ch in chains:
            rhs = jnp.where(half[ch["hh"]], ch["w0v"], ch["grp"]["kt2sw"])
            ch["x"] = _dot(ch["t"].astype(bf16), rhs.astype(bf16))
        yield
        for ch in chains:
            vtop = jnp.where(half[ch["hh"]], ch["grp"]["v2"], 0.0)
            ch["z"] = jnp.concatenate([vtop, -ch["x"]], axis=0).astype(bf16)
            ch["o"] = _dot(ch["bot"], ch["z"])
        yield
        for ch in chains:
            ch["tn"] = _dot_tn(ch["z"], ch["grp"]["kbbb2"])
        yield
        for grp in groups:
            d, rs = grp["d"], grp["rs"]
            c0, c1 = grp["chains"]
            y0_s[d, rs, :] = jnp.where(half[0], c0["o"], c1["o"])
            rp_s[d, rs, :] = grp["rt2"] + pltpu.roll(jnp.where(half[0], c1["o"], c0["o"]), A_HEAD, 1)
            ps = slice(grp["c"] * LANE, (grp["c"] + 1) * LANE)
            nt_s[d, ps, :] = jnp.concatenate([jnp.where(half[0], c0["tn"][:A_HEAD], 0.0),
                                              jnp.where(half[1], c1["tn"][A_HEAD:], 0.0)], axis=0)
            mt_s[d, ps, :] = jnp.where(eye_pair, grp["e_tot"], 0.0) + jnp.concatenate(
                [jnp.where(half[0], c0["tn"][A_HEAD:], 0.0), jnp.where(half[1], c1["tn"][:A_HEAD], 0.0)],
                axis=0)

    def seq_step(seqs, i):
        steps = []
        for q in seqs:
            for d in range(2):
                c = q * nch_seq + (i if d == 0 else nch_seq - 1 - i)
                steps.append((q, d, slice(c * CHUNK, (c + 1) * CHUNK), slice(c * LANE, (c + 1) * LANE),
                              st_s[q, d].astype(bf16)))
        for q, d, rs, ps, sb in steps:
            st_s[q, d] = _dot(sb, mt_s[d, ps, :].astype(bf16)) + nt_s[d, ps, :]
        for q, d, rs, ps, sb in steps:
            y_s[d, rs, :] = y0_s[d, rs, :] + _dot_nt(rp_s[d, rs, :].astype(bf16), sb)

    def finish_a(c, fin):
        rs = slice(c * CHUNK, (c + 1) * CHUNK)
        y = y_s[0, rs, :] + y_s[1, rs, :]
        kmean = 0.5 * (kd_s[0, rs, :] + kd_s[1, rs, :])
        fin[c] = dict(rs=rs, y=y, ysum=_head_sum(y, bd), bsum=_head_sum(r_s[rs, :] * kmean * row(V_RK), bd))

    def finish_b(c, fin):
        f = fin[c]
        f["yc"] = f["y"] - f["ysum"] * (1.0 / A_HEAD)
        f["vsum"] = _head_sum(f["yc"] * f["yc"], bd)

    def finish_c(c, fin):
        f = fin[c]
        rs = f["rs"]
        yn = f["yc"] * lax.rsqrt(f["vsum"] * (1.0 / A_HEAD) + GN_EPS) * row(V_LNG) + row(V_LNB)
        mix[rs, :LANE] = (yn + f["bsum"] * v_s[rs, :]) * _silu(x_ref[rs, 3 * LANE:4 * LANE])

    def finish_fillers(chunks, fin):
        chunks = list(chunks)
        out = []
        for t in range(len(chunks) + 2):
            slot = []
            if t < len(chunks):
                slot.append(functools.partial(finish_a, chunks[t], fin))
            if 0 <= t - 1 < len(chunks):
                slot.append(functools.partial(finish_b, chunks[t - 1], fin))
            if 0 <= t - 2 < len(chunks):
                slot.append(functools.partial(finish_c, chunks[t - 2], fin))
            out.append(slot)
        return out

    def conv_branch():
        u = col(5) * col(6)
        conv = prev(u) * row(V_CW) + u * row(V_CW + 1) + nxt(u) * row(V_CW + 2) + row(V_CB)
        mix[:, LANE:] = col(4) * conv * _silu(col(7))

    def run(stages, slots):
        slots = list(slots)
        for _ in stages:
            if slots:
                for f in slots.pop(0):
                    f()
        for slot in slots:
            for f in slot:
                f()

    hc = nch // 2
    dots = [setup_dots(c) for c in range(hc)]
    first_half = [grp for st in dots for grp in setup_chains(st)]
    second_half, pending = [], {}
    slots = []
    for t in range(hc + 1):
        slot = []
        if t < hc:
            slot.append(functools.partial(lambda c: pending.__setitem__(c, setup_dots(c)), hc + t))
        if t >= 1:
            slot.append(functools.partial(lambda c: second_half.extend(setup_chains(pending[c])), hc + t - 1))
        slots.append(slot)
    run(chain_stages(first_half), slots)
    fin = {}
    if NB >= 2:
        early = list(range(NB // 2))
        late = list(range(NB // 2, NB))
        slots = [[conv_branch]] + [[functools.partial(seq_step, early, i)] for i in range(nch_seq)]
        slots += finish_fillers(range(hc), fin)
        run(chain_stages(second_half), slots)
        for i in range(nch_seq):
            seq_step(late, i)
        run(iter(()), finish_fillers(range(hc, nch), fin))
    else:
        run(chain_stages(second_half), [[conv_branch]])
        ready = []
        for i in range(nch_seq):
            seq_step([0], i)
            if i >= nch_seq // 2:
                ready += [i, nch_seq - 1 - i]
            for c in list(ready):
                stage = fin.get(c, {}).get("stage", 0)
                (finish_a, finish_b, finish_c)[stage](c, fin)
                fin[c]["stage"] = stage + 1
                if stage == 2:
                    ready.remove(c)
        while ready:
            for c in list(ready):
                stage = fin[c]["stage"]
                (finish_a, finish_b, finish_c)[stage](c, fin)
                fin[c]["stage"] = stage + 1
                if stage == 2:
                    ready.remove(c)

    if write_s:
        if not has_prev:
            for pp in range(N_PAIRS):
                if pp != pair:
                    sout[:, pp] = jnp.zeros((NB, 2, 2, A_HEAD, A_HEAD), f32)
        own = sout if has_prev else sout.at[:, pair]
        for q in range(NB):
            for d in range(2):
                for hh in range(2):
                    own[q, d, hh] = st_s[q, d, hh * A_HEAD:(hh + 1) * A_HEAD, hh * A_HEAD:(hh + 1) * A_HEAD]


def _mixer(proj, tables, s0, pair, *, L, n_seq, write_s, s_prev=None):
    vecs, mats, mulo = tables
    has_s0 = s0 is not None
    R = MIXER_ROWS
    nb = R // L
    in_specs = [
        pl.BlockSpec((R, PAIR_COLS), lambda s, j: (s, j)),
        pl.BlockSpec((R, 4 * LORA), lambda s, j: (s, N_PAIR * PAIR_COLS // (4 * LORA))),
        pl.BlockSpec((N_PAIR, N_VEC, LANE), lambda s, j: (0, 0, 0)),
        pl.BlockSpec((N_PAIR, 4, LORA, LANE), lambda s, j: (0, 0, 0, 0)),
        pl.BlockSpec((1, 4 * LORA), lambda s, j: (0, 0)),
    ]
    args = [proj, proj, vecs, mats, mulo]
    if has_s0:
        in_specs.append(pl.BlockSpec((nb, None, 2, 2, A_HEAD, A_HEAD), lambda s, j: (s, pair, 0, j, 0, 0)))
        args.append(s0)
    out_specs = [pl.BlockSpec((R, 2 * LANE), lambda s, j: (s, j))]
    out_shape = [jax.ShapeDtypeStruct((n_seq * L, D_MODEL), f32)]
    aliases = {}
    if write_s:
        out_shape.append(jax.ShapeDtypeStruct((n_seq, N_PAIRS, 2, A_HEADS, A_HEAD, A_HEAD), f32))
        if s_prev is None:
            out_specs.append(pl.BlockSpec((nb, N_PAIRS, 2, 2, A_HEAD, A_HEAD), lambda s, j: (s, 0, 0, j, 0, 0)))
        else:
            out_specs.append(pl.BlockSpec((nb, None, 2, 2, A_HEAD, A_HEAD), lambda s, j: (s, pair, 0, j, 0, 0)))
            aliases = {len(args): 1}
            in_specs.append(pl.BlockSpec(memory_space=pl.ANY))
            args.append(s_prev)
    tok = lambda: pltpu.VMEM((R, LANE), f32)
    tok2 = lambda: pltpu.VMEM((2, R, LANE), f32)
    maps = lambda: pltpu.VMEM((2, 2 * R, LANE), f32)
    scratch = [tok(), tok(), tok(), pltpu.VMEM((R, 4 * LORA), f32), tok2(), tok2(), tok2(), tok2(),
               maps(), maps(), pltpu.VMEM((nb, 2, LANE, LANE), f32)]
    return pl.pallas_call(
        functools.partial(_mixer_kernel, L=L, NB=nb, has_s0=has_s0, write_s=write_s, has_prev=bool(aliases),
                          pair=pair),
        grid=(n_seq // nb, N_PAIR),
        in_specs=in_specs,
        out_specs=out_specs,
        out_shape=out_shape,
        scratch_shapes=scratch,
        input_output_aliases=aliases,
        compiler_params=pltpu.CompilerParams(vmem_limit_bytes=VMEM_LIMIT),
        name="rwkv_conv_mixer_L%d" % L,
    )(*args)


def _attn_kernel(*refs, LQ, LN, NB, NQ, PAST, rope, write_cache, n_prev, pair):
    it = iter(refs)
    q_ref, k_ref, v_ref, g_ref, qg_ref, kg_ref = (next(it) for _ in range(6))
    if rope:
        cosq, sinq, cosk, sink = (next(it) for _ in range(4))
    if PAST:
        ck, cv = next(it), next(it)
    for _ in range(n_prev):
        next(it)
    out_ref = next(it)
    if write_cache:
        nk_ref, nv_ref = next(it), next(it)
    kbuf, vbuf, obuf = it

    bd = _head_block_diag()
    lane = lax.broadcasted_iota(jnp.int32, (1, LANE), 1)
    first_half = (lane & 31) < 16

    def rms(x, g):
        return x * lax.rsqrt(_head_sum(x * x, bd, pieces=1) * (1.0 / C_HEAD) + EPS) * g

    def rotate(x, cos, sin):
        outs = []
        for i in range(x.shape[1] // LANE):
            xb = x[:, i * LANE:(i + 1) * LANE]
            sw = jnp.where(first_half, pltpu.roll(xb, LANE - 16, 1), pltpu.roll(xb, 16, 1))
            outs.append(xb * cos + sw * sin)
        return jnp.concatenate(outs, axis=1)

    def build_keys():
        kn = rms(k_ref[...], kg_ref[...])
        vx = v_ref[...]
        if write_cache:
            nk_own, nv_own = nk_ref, nv_ref
            if not n_prev:
                for pp in range(N_PAIRS):
                    if pp != pair:
                        nk_ref[:, pp] = jnp.zeros((NB, C_KV_HEADS, C_HEAD, LN), f32)
                        nv_ref[:, pp] = jnp.zeros((NB, C_KV_HEADS, C_HEAD, LN), f32)
                nk_own, nv_own = nk_ref.at[:, pair], nv_ref.at[:, pair]
            for b in range(NB):
                knt = kn[b * LN:(b + 1) * LN, :].T
                vxt = vx[b * LN:(b + 1) * LN, :].T
                for h in range(C_KV_HEADS):
                    nk_own[b, h] = knt[h * C_HEAD:(h + 1) * C_HEAD, :]
                    nv_own[b, h] = vxt[h * C_HEAD:(h + 1) * C_HEAD, :]
        if rope:
            kn = rotate(kn, cosk[...], sink[...])
        for b in range(NB):
            for h in range(C_KV_HEADS):
                kbuf[b, h] = kn[b * LN:(b + 1) * LN, h * C_HEAD:(h + 1) * C_HEAD].astype(bf16)
                vbuf[b, h] = vx[b * LN:(b + 1) * LN, h * C_HEAD:(h + 1) * C_HEAD].astype(bf16)

    if NQ == 1:
        build_keys()
    else:
        pl.when(pl.program_id(1) == 0)(build_keys)

    qn = rms(q_ref[...], qg_ref[...])
    if rope:
        qn = rotate(qn, cosq[...], sinq[...])
    qn = (qn * (C_HEAD ** -0.5 * LOG2E)).astype(bf16)
    units = [(b, h) for b in range(NB) for h in range(C_KV_HEADS)]
    heads = [[slice((h * C_GROUP + g) * C_HEAD, (h * C_GROUP + g + 1) * C_HEAD) for g in range(C_GROUP)]
             for h in range(C_KV_HEADS)]

    def scores(b, h):
        q4 = jnp.concatenate([qn[b * LQ:(b + 1) * LQ, sl] for sl in heads[h]], axis=0)
        s_new = _dot_nt(q4, kbuf[b, h])
        if not PAST:
            return s_new
        return jnp.concatenate([_dot(q4, ck[b, h].astype(bf16)), s_new], axis=1)

    s_next = scores(*units[0])
    for n, (b, h) in enumerate(units):
        s = s_next
        if n + 1 < len(units):
            s_next = scores(*units[n + 1])
        p = jnp.exp2(s - jnp.max(s, axis=-1, keepdims=True))
        pb = p.astype(bf16)
        o = _dot(pb[:, PAST:], vbuf[b, h])
        if PAST:
            o = o + _dot_nt(pb[:, :PAST], cv[b, h].astype(bf16))
        o = o / jnp.sum(p, axis=-1, keepdims=True)
        for g, sl in enumerate(heads[h]):
            obuf[b * LQ:(b + 1) * LQ, sl] = o[g * LQ:(g + 1) * LQ]
    out_ref[...] = obuf[...] * _silu(g_ref[...])


def _rope_tables():
    pos = np.arange(DEC_SEQ)
    inv = ROPE_THETA ** (-np.arange(0, ROPE_HALF, 2, dtype=np.float32) / ROPE_HALF)
    ang_r = (pos // GRID_W).astype(np.float32)[:, None] * inv
    ang_c = (pos % GRID_W).astype(np.float32)[:, None] * inv
    cos64 = np.concatenate([np.cos(ang_r), np.cos(ang_r), np.cos(ang_c), np.cos(ang_c)], axis=1)
    sin64 = np.concatenate([-np.sin(ang_r), np.sin(ang_r), -np.sin(ang_c), np.sin(ang_c)], axis=1)
    return jnp.asarray(np.tile(cos64, (1, 2)), f32), jnp.asarray(np.tile(sin64, (1, 2)), f32)


def _attention(proj, q_g, k_g, cache_k, cache_v, pair, tables, *, n_seq, L, sample, kv_prev=None):
    LQ = ATTN_ROWS
    nq = L // LQ
    nb = 1 if sample else ATTN_PROMPT_SEQS
    assert nb == 1 or nq == 1
    past = PAST_LEN if sample else 0
    in_specs = [
        pl.BlockSpec((nb * LQ, D_MODEL), lambda b, i: (b * nq + i, 0)),
        pl.BlockSpec((nb * L, KV_WIDTH), lambda b, i: (b, 2 * D_MODEL // KV_WIDTH)),
        pl.BlockSpec((nb * L, KV_WIDTH), lambda b, i: (b, 2 * D_MODEL // KV_WIDTH + 1)),
        pl.BlockSpec((nb * LQ, D_MODEL), lambda b, i: (b * nq + i, 1)),
        pl.BlockSpec((1, D_MODEL), lambda b, i: (0, 0)),
        pl.BlockSpec((1, KV_WIDTH), lambda b, i: (0, 0)),
    ]
    args = [proj, proj, proj, proj, q_g, k_g]
    if sample:
        cos, sin = tables
        in_specs += [pl.BlockSpec((LQ, LANE), lambda b, i: (i, 0)),
                     pl.BlockSpec((LQ, LANE), lambda b, i: (i, 0)),
                     pl.BlockSpec((L, LANE), lambda b, i: (0, 0)),
                     pl.BlockSpec((L, LANE), lambda b, i: (0, 0))]
        args += [cos, sin, cos, sin]
        cspec = pl.BlockSpec((nb, None, C_KV_HEADS, C_HEAD, PAST_LEN), lambda b, i: (b, pair, 0, 0, 0))
        in_specs += [cspec, cspec]
        args += [cache_k.transpose(0, 1, 2, 4, 3), cache_v.transpose(0, 1, 2, 4, 3)]
    out_specs = [pl.BlockSpec((nb * LQ, D_MODEL), lambda b, i: (b * nq + i, 0))]
    out_shape = [jax.ShapeDtypeStruct((n_seq * L, D_MODEL), f32)]
    aliases = {}
    if not sample:
        if kv_prev is None:
            nspec = pl.BlockSpec((nb, N_PAIRS, C_KV_HEADS, C_HEAD, L), lambda b, i: (b, 0, 0, 0, 0))
        else:
            nspec = pl.BlockSpec((nb, None, C_KV_HEADS, C_HEAD, L), lambda b, i: (b, pair, 0, 0, 0))
        out_specs += [nspec, nspec]
        out_shape += [jax.ShapeDtypeStruct((n_seq, N_PAIRS, C_KV_HEADS, C_HEAD, L), f32)] * 2
        if kv_prev is not None:
            aliases = {len(args): 1, len(args) + 1: 2}
            in_specs += [pl.BlockSpec(memory_space=pl.ANY)] * 2
            args += list(kv_prev)
    return pl.pallas_call(
        functools.partial(_attn_kernel, LQ=LQ, LN=L, NB=nb, NQ=nq, PAST=past, rope=sample, write_cache=not sample,
                          n_prev=len(aliases), pair=pair),
        grid=(n_seq // nb, nq),
        in_specs=in_specs,
        out_specs=out_specs,
        out_shape=out_shape,
        input_output_aliases=aliases,
        scratch_shapes=[pltpu.VMEM((nb, C_KV_HEADS, L, C_HEAD), bf16),
                        pltpu.VMEM((nb, C_KV_HEADS, L, C_HEAD), bf16),
                        pltpu.VMEM((nb * LQ, D_MODEL), f32)],
        compiler_params=pltpu.CompilerParams(vmem_limit_bytes=VMEM_LIMIT),
        name="gqa_sample" if sample else "gqa_prompt",
    )(*args)


def kernel(x_prompt, x_sample, c, state_rwkv, cache_k, cache_v, c_ctx, w_ada, b_ada, norm_g, final_g,
           w_in_e, mu_shift, lora_w2, w0, lora_a2, a0, k_k, k_a, r_k, lnx_g, lnx_b, conv_w, conv_b,
           w_out_e, w_in_o, q_norm_g, k_norm_g, w_out_o):
    xs = [x_prompt.reshape(STREAM_TOK, D_MODEL), x_sample.reshape(STREAM_TOK, D_MODEL)]
    cond = jnp.concatenate([c_ctx[None, :], c, jnp.zeros((N_MOD - 1 - DEC_BATCH, D_MODEL), f32)], axis=0)
    mod = _adaln(cond, w_ada, b_ada).reshape(DEPTH, N_MOD, 1, 3 * D_MODEL)

    rope = _rope_tables()
    final_g2 = final_g.reshape(1, D_MODEL)

    w_ins, w_outs = [], []
    for layer in range(DEPTH):
        p = layer // 2
        if layer % 2 == 0:
            w_ins.append(_even_in_weight(w_in_e[p]))
            wo = w_out_e[p]
            w_outs.append(jnp.concatenate(
                [wo[h * A_WIDTH + j * LANE:h * A_WIDTH + (j + 1) * LANE] for j in range(N_PAIR) for h in range(2)],
                axis=0).astype(bf16))
        else:
            wi = w_in_o[p]
            w_ins.append(jnp.concatenate([wi[:, :D_MODEL], wi[:, D_MODEL + 2 * KV_WIDTH:],
                                          wi[:, D_MODEL:D_MODEL + 2 * KV_WIDTH]], axis=1).astype(bf16))
            w_outs.append(w_out_o[p].astype(bf16))
    inp = lambda layer: (mod[layer], norm_g[layer].reshape(1, D_MODEL), w_ins[layer])

    new_rwkv = new_kv = None
    (proj_p,) = _proj(xs[0], sample=False, inp=inp(0))
    (proj_s,) = _proj(xs[1], sample=True, inp=inp(0))
    for layer in range(DEPTH):
        p = layer // 2
        if layer % 2 == 0:
            tables = _mixer_tables(p, mu_shift, lora_w2, w0, lora_a2, a0, k_k, k_a, r_k, lnx_g, lnx_b,
                                   conv_w, conv_b)
            act_p, new_rwkv = _mixer(proj_p, tables, None, p, L=SEQ, n_seq=BATCH, write_s=True, s_prev=new_rwkv)
            (act_s,) = _mixer(proj_s, tables, state_rwkv, p, L=DEC_SEQ, n_seq=DEC_BATCH, write_s=False)
        else:
            q_g = jnp.tile(q_norm_g[p], C_Q_HEADS).reshape(1, D_MODEL)
            k_g = jnp.tile(k_norm_g[p], C_KV_HEADS).reshape(1, KV_WIDTH)
            act_p, *new_kv = _attention(proj_p, q_g, k_g, None, None, p, None, n_seq=BATCH, L=SEQ,
                                        sample=False, kv_prev=new_kv)
            (act_s,) = _attention(proj_s, q_g, k_g, cache_k, cache_v, p, rope, n_seq=DEC_BATCH,
                                  L=DEC_SEQ, sample=True)
        if layer < DEPTH - 1:
            xs[0], proj_p = _proj(xs[0], sample=False, out=(act_p, mod[layer], w_outs[layer]), inp=inp(layer + 1))
            xs[1], proj_s = _proj(xs[1], sample=True, out=(act_s, mod[layer], w_outs[layer]), inp=inp(layer + 1))
        else:
            (xs[0],) = _proj(xs[0], sample=False, out=(act_p, mod[layer], w_outs[layer]), final_g=final_g2)
            (xs[1],) = _proj(xs[1], sample=True, out=(act_s, mod[layer], w_outs[layer]), final_g=final_g2)
    return (xs[0].reshape(BATCH, SEQ, D_MODEL), xs[1].reshape(DEC_BATCH, DEC_SEQ, D_MODEL),
            new_rwkv, new_kv[0].transpose(0, 1, 2, 4, 3), new_kv[1].transpose(0, 1, 2, 4, 3))
```

```python
import functools

import numpy as np
import jax
import jax.numpy as jnp
from jax import lax
from jax.experimental import pallas as pl
from jax.experimental.pallas import tpu as pltpu

f32 = jnp.float32
bf16 = jnp.bfloat16
HI = lax.Precision.HIGHEST

D_MODEL = 1024
BATCH = 32
SEQ = 256
DEPTH = 4
DEC_BATCH = 8
DEC_SEQ = 1024
PAST_LEN = 512
GRID_W = 64
EPS = 1e-6
A_WIDTH = 512
A_HEAD = 64
A_HEADS = 8
LORA = 64
GN_EPS = 64e-5
B_WIDTH = 512
C_HEAD = 64
C_Q_HEADS = 16
C_KV_HEADS = 4
C_GROUP = 4
KV_WIDTH = 256
ROPE_THETA = 10000.0
ROPE_HALF = 32
A_SHIFT = 3 * A_WIDTH + 4 * LORA
EVEN_IN = A_SHIFT + A_WIDTH + 4 * B_WIDTH
ODD_IN = 2 * D_MODEL + 2 * KV_WIDTH

STREAM_TOK = BATCH * SEQ
assert STREAM_TOK == DEC_BATCH * DEC_SEQ
PROJ_ROWS = 512
ATTN_ROWS = 256
ATTN_PROMPT_SEQS = 4
LOG2E = 1.4426950408889634
N_MOD = 16
CHUNK = 64
LANE = 128
VMEM_LIMIT = 56 * 1024 * 1024
N_PAIR = A_WIDTH // LANE
PAIR_COLS = 8 * LANE
N_VEC = 16
MIXER_ROWS = 1024
N_PAIRS = DEPTH // 2


def _sigmoid(x):
    return 1.0 / (1.0 + jnp.exp(-x))


def _silu(x):
    return x * _sigmoid(x)


def _dot(a, b, precision=None):
    return jnp.dot(a, b, precision=precision, preferred_element_type=f32)


def _dot_nt(a, b, precision=None):
    return lax.dot_general(a, b, (((1,), (1,)), ((), ())), precision=precision,
                           preferred_element_type=f32)


def _dot_tn(a, b, precision=None):
    return lax.dot_general(a, b, (((0,), (0,)), ((), ())), precision=precision,
                           preferred_element_type=f32)


def _mod_row(sample):
    if sample:
        return lambda i: 1 + i // (DEC_SEQ // PROJ_ROWS)
    return lambda i: 0


def _head_block_diag():
    ri = lax.broadcasted_iota(jnp.int32, (LANE, LANE), 0) >> 6
    ci = lax.broadcasted_iota(jnp.int32, (LANE, LANE), 1) >> 6
    return (ri == ci).astype(bf16)


def _head_sum(x, bd, pieces=2):
    n = x.shape[1] // LANE
    hi = x.astype(bf16)
    cols = [_dot(hi[:, i * LANE:(i + 1) * LANE], bd) for i in range(n)]
    if pieces == 2:
        lo = (x - hi.astype(f32)).astype(bf16)
        cols = [c + _dot(lo[:, i * LANE:(i + 1) * LANE], bd) for i, c in enumerate(cols)]
    return cols[0] if n == 1 else jnp.concatenate(cols, axis=1)


def _adaln_kernel(cond_ref, w_ref, b_ref, o_ref):
    o_ref[...] = _dot(_silu(cond_ref[...]), w_ref[...], HI) + b_ref[...]


def _adaln(cond, w_ada, b_ada):
    tn = 1536
    return pl.pallas_call(
        _adaln_kernel,
        grid=(DEPTH, 3 * D_MODEL // tn),
        in_specs=[
            pl.BlockSpec((N_MOD, D_MODEL), lambda l, n: (0, 0)),
            pl.BlockSpec((None, D_MODEL, tn), lambda l, n: (l, 0, n)),
            pl.BlockSpec((None, 1, tn), lambda l, n: (l, 0, n)),
        ],
        out_specs=pl.BlockSpec((None, N_MOD, tn), lambda l, n: (l, 0, n)),
        out_shape=jax.ShapeDtypeStruct((DEPTH, N_MOD, 3 * D_MODEL), f32),
        compiler_params=pltpu.CompilerParams(vmem_limit_bytes=VMEM_LIMIT),
        name="adaln",
    )(cond, w_ada, b_ada.reshape(DEPTH, 1, 3 * D_MODEL))


def _proj_kernel(*refs, has_out, has_in, final):
    it = iter(refs)
    if has_out:
        a_ref, x_ref, modp_ref, wo_ref = (next(it) for _ in range(4))
        fg_ref = next(it) if final else None
    else:
        x_ref = next(it)
    if has_in:
        modn_ref, g_ref, wi_ref = (next(it) for _ in range(3))
    x = x_ref[...]
    if has_out:
        x = x + modp_ref[:, 2 * D_MODEL:] * _dot(a_ref[...].astype(bf16), wo_ref[...])
        xo_ref = next(it)
        if final:
            xo_ref[...] = x * lax.rsqrt(jnp.mean(x * x, axis=-1, keepdims=True) + EPS) * fg_ref[...]
        else:
            xo_ref[...] = x
    if has_in:
        p_ref = next(it)
        y = x * lax.rsqrt(jnp.mean(x * x, axis=-1, keepdims=True) + EPS) * g_ref[...]
        m = modn_ref[...]
        h = y * (1.0 + m[:, D_MODEL:2 * D_MODEL]) + m[:, :D_MODEL]
        p_ref[...] = _dot(h.astype(bf16), wi_ref[...])


def _proj(x, *, sample, out=None, inp=None, final_g=None):
    mrow = _mod_row(sample)
    tm = PROJ_ROWS
    rows = lambda w: pl.BlockSpec((tm, w), lambda i: (i, 0))
    const = lambda shape: pl.BlockSpec(shape, lambda i: (0,) * len(shape), pipeline_mode=pl.Buffered(1))
    modspec = pl.BlockSpec((None, 1, 3 * D_MODEL), lambda i: (mrow(i), 0, 0))
    in_specs, args, out_specs, out_shape = [], [], [], []
    if out is not None:
        act, mod_prev, w_out = out
        in_specs += [rows(D_MODEL), rows(D_MODEL), modspec, const((D_MODEL, D_MODEL))]
        args += [act, x, mod_prev, w_out]
        if final_g is not None:
            in_specs.append(const((1, D_MODEL)))
            args.append(final_g)
        out_specs.append(rows(D_MODEL))
        out_shape.append(jax.ShapeDtypeStruct((STREAM_TOK, D_MODEL), f32))
    else:
        in_specs.append(rows(D_MODEL))
        args.append(x)
    if inp is not None:
        mod_next, g, w_in = inp
        n = w_in.shape[1]
        in_specs += [modspec, const((1, D_MODEL)), const((D_MODEL, n))]
        args += [mod_next, g, w_in]
        out_specs.append(rows(n))
        out_shape.append(jax.ShapeDtypeStruct((STREAM_TOK, n), f32))
    return pl.pallas_call(
        functools.partial(_proj_kernel, has_out=out is not None, has_in=inp is not None,
                          final=final_g is not None),
        grid=(STREAM_TOK // tm,),
        in_specs=in_specs,
        out_specs=out_specs,
        out_shape=out_shape,
        compiler_params=pltpu.CompilerParams(vmem_limit_bytes=VMEM_LIMIT),
        name="proj_%s%s" % ("o" if out is not None else "", "i" if inp is not None else ""),
    )(*args)


def _even_column_starts():
    bases = [0, A_WIDTH, 2 * A_WIDTH, A_SHIFT, A_SHIFT + A_WIDTH, A_SHIFT + A_WIDTH + B_WIDTH,
             A_SHIFT + A_WIDTH + 2 * B_WIDTH, A_SHIFT + A_WIDTH + 3 * B_WIDTH]
    return [(b + j * LANE, LANE) for j in range(N_PAIR) for b in bases] + [(3 * A_WIDTH, 4 * LORA)]


def _even_column_order():
    return np.concatenate([s + np.arange(w) for s, w in _even_column_starts()])


def _even_in_weight(w):
    groups = jnp.concatenate([w[:, :3 * A_WIDTH], w[:, A_SHIFT:]], axis=1)
    groups = groups.reshape(D_MODEL, PAIR_COLS // LANE, N_PAIR, LANE).transpose(0, 2, 1, 3)
    return jnp.concatenate([groups.reshape(D_MODEL, N_PAIR * PAIR_COLS), w[:, 3 * A_WIDTH:A_SHIFT]],
                           axis=1).astype(bf16)


def _mixer_tables(p, mu_shift, lora_w2, w0, lora_a2, a0, k_k, k_a, r_k, lnx_g, lnx_b, conv_w, conv_b):
    pair = lambda v: v.reshape(-1, N_PAIR, LANE).transpose(1, 0, 2)
    mu = mu_shift[p]
    rows = [k_k[p][None], k_a[p][None], r_k[p].reshape(1, A_WIDTH), lnx_g[p][None], lnx_b[p][None],
            conv_b[p][None], conv_w[p], w0[p], a0[p],
            mu[None, :A_WIDTH], mu[None, A_WIDTH:2 * A_WIDTH], mu[None, 2 * A_WIDTH:3 * A_WIDTH]]
    vecs = pair(jnp.concatenate(rows, axis=0))
    assert vecs.shape == (N_PAIR, N_VEC, LANE)
    mats = pair(jnp.concatenate([lora_w2[p], lora_a2[p]], axis=0).reshape(4 * LORA, A_WIDTH))
    mats = mats.reshape(N_PAIR, 4, LORA, LANE)
    return vecs, mats, mu[None, 3 * A_WIDTH:]


V_KK, V_KA, V_RK, V_LNG, V_LNB, V_CB, V_CW, V_W0, V_A0, V_MU = 0, 1, 2, 3, 4, 5, 6, 9, 11, 13


def _scan_masks():
    ri = lax.broadcasted_iota(jnp.int32, (CHUNK, CHUNK), 0)
    ci = lax.broadcasted_iota(jnp.int32, (CHUNK, CHUNK), 1)
    eye = ri == ci
    ri2 = lax.broadcasted_iota(jnp.int32, (CHUNK, 2 * CHUNK), 0)
    ci2 = lax.broadcasted_iota(jnp.int32, (CHUNK, 2 * CHUNK), 1) & (CHUNK - 1)
    strict2 = (ri2 > ci2, ri2 < ci2)
    incl2 = (ri2 >= ci2, ri2 <= ci2)
    levels = []
    for sh in range(6):
        levels.append(((ri >> (sh + 1)) == (ci >> (sh + 1))) & ((ri >> sh) != (ci >> sh)))
    return eye, strict2, incl2, levels


def _mixer_kernel(*refs, L, NB, has_s0, write_s, has_prev, pair):
    it = iter(refs)
    x_ref, lo_ref, vec_ref, mat_ref, mulo = (next(it) for _ in range(5))
    s0 = next(it) if has_s0 else None
    if has_prev:
        next(it)
    mix = next(it)
    sout = next(it) if write_s else None
    r_s, k_s, v_s, lo_s, kd_s, y_s, rp_s, y0_s, mt_s, nt_s, st_s = it

    j = pl.program_id(1)
    vec = vec_ref[j]
    row = lambda i: vec[i:i + 1, :]
    col = lambda i: x_ref[:, i * LANE:(i + 1) * LANE]
    lora_w = [mat_ref[j, i].astype(bf16) for i in range(4)]

    R = NB * L
    nch, nch_seq = R // CHUNK, L // CHUNK
    rows = lax.broadcasted_iota(jnp.int32, (R, 1), 0)
    first, last = (rows & (L - 1)) == 0, (rows & (L - 1)) == L - 1

    def prev(f):
        return jnp.where(first, 0.0, pltpu.roll(f, 1, 0))

    def nxt(f):
        return jnp.where(last, 0.0, pltpu.roll(f, R - 1, 0))

    def tshift(f, mu):
        return f + mu * (0.5 * (prev(f) + nxt(f)) - f)

    bd = _head_block_diag()

    r_s[...] = tshift(col(0), row(V_MU))
    k_s[...] = tshift(col(1), row(V_MU + 1))
    v_s[...] = tshift(col(2), row(V_MU + 2))
    lo_s[...] = tshift(lo_ref[...], mulo[...])
    for q in range(NB):
        for d in range(2):
            st_s[q, d] = jnp.zeros((LANE, LANE), f32)
            if has_s0:
                for hh in range(2):
                    st_s[q, d, hh * A_HEAD:(hh + 1) * A_HEAD, hh * A_HEAD:(hh + 1) * A_HEAD] = s0[q, d, hh]

    eye, strict2, incl2, levels = _scan_masks()
    eye_f = eye.astype(f32)
    lane = lax.broadcasted_iota(jnp.int32, (1, LANE), 1)
    half = (lane < A_HEAD, lane >= A_HEAD)
    ri = lax.broadcasted_iota(jnp.int32, (LANE, LANE), 0)
    ci = lax.broadcasted_iota(jnp.int32, (LANE, LANE), 1)
    eye_pair = ri == ci
    crow = lax.broadcasted_iota(jnp.int32, (CHUNK, 1), 0)

    def setup_dots(c):
        rs = slice(c * CHUNK, (c + 1) * CHUNK)
        k2, lo = k_s[rs, :], lo_s[rs, :]
        kk = k2 * row(V_KK)
        st = dict(c=c, rs=rs, k2=k2, kk=kk, ssq=_head_sum(kk * kk, bd), wl=[], al=[])
        for d in range(2):
            lwd = lo[:, d * LORA:(d + 1) * LORA]
            lad = lo[:, 2 * LORA + d * LORA:2 * LORA + (d + 1) * LORA]
            st["wl"].append(_dot(jnp.tanh(lwd).astype(bf16), lora_w[d]))
            st["al"].append(_dot(lad.astype(bf16), lora_w[2 + d]))
        return st

    def setup_chains(st):
        c, rs, k2 = st["c"], st["rs"], st["k2"]
        r2, v2 = r_s[rs, :], v_s[rs, :]
        kap2 = st["kk"] / jnp.maximum(jnp.sqrt(st["ssq"]), 1e-12)
        groups = []
        for d in range(2):
            wl = row(V_W0 + d) + st["wl"][d]
            softplus = jnp.maximum(-wl, 0.0) + jnp.log(1.0 + jnp.exp(-jnp.abs(wl)))
            lw = -jnp.exp(-softplus - 0.5)
            g = lw
            for sh in range(6):
                stp = 1 << sh
                if d == 0:
                    g = g + jnp.where(crow >= stp, pltpu.roll(g, stp, 0), 0.0)
                else:
                    g = g + jnp.where(crow < CHUNK - stp, pltpu.roll(g, CHUNK - stp, 0), 0.0)
            a = _sigmoid(row(V_A0 + d) + st["al"][d])
            kd2 = k2 * (1.0 + (a - 1.0) * row(V_KA))
            kd_s[d, rs, :] = kd2
            b2 = a * kap2
            tot = g[CHUNK - 1:CHUNK, :] if d == 0 else g[0:1, :]
            e_ng = jnp.exp(-g)
            e_rem = jnp.exp(tot - g)
            kt2 = kap2 * jnp.exp(g - lw)
            rt2 = r2 * jnp.exp(g)
            lhs2 = jnp.concatenate([kt2, rt2], axis=0).astype(bf16)
            grp = dict(
                d=d, c=c, rs=rs, rt2=rt2, v2=v2, v2b=v2.astype(bf16), e_tot=jnp.exp(tot),
                kt2sw=pltpu.roll(kt2, A_HEAD, 1),
                rhs2=jnp.concatenate([kd2 * e_ng, b2 * e_ng], axis=0).astype(bf16),
                kbbb2=jnp.concatenate([kd2 * e_rem, b2 * e_rem], axis=0).astype(bf16), chains=[])
            for hh in range(2):
                ch = dict(grp=grp, d=d, hh=hh)
                lhs = jnp.where(half[hh], lhs2, jnp.zeros_like(lhs2))
                ch["a"] = _dot_nt(lhs, grp["rhs2"])
                grp["chains"].append(ch)
            groups.append(grp)
        return groups

    def chain_stages(groups):
        chains = [ch for grp in groups for ch in grp["chains"]]
        for ch in chains:
            top = jnp.where(strict2[ch["d"]], ch["a"][:CHUNK], 0.0)
            ch["bot"] = jnp.where(incl2[ch["d"]], ch["a"][CHUNK:], 0.0).astype(bf16)
            ch["akk"] = top[:, :CHUNK].astype(bf16)
            akb = top[:, CHUNK:]
            ch["ab"] = akb.astype(bf16)
            ch["t"] = eye_f - jnp.where(levels[0], akb, 0.0)
        yield
        for m in levels[1:]:
            for ch in chains:
                ch["tb"] = ch["t"].astype(bf16)
                off = jnp.where(m, ch["ab"], jnp.zeros_like(ch["ab"]))
                ch["p"] = _dot(off, ch["tb"]).astype(bf16)
            yield
            for ch in chains:
                ch["t"] = ch["t"] - _dot(ch["tb"], ch["p"])
            yield
        for ch in chains:
            ch["w0v"] = _dot(ch["akk"], ch["grp"]["v2b"])
        yield
        for ch in chains:
            rhs = jnp.where(half[ch["hh"]], ch["w0v"], ch["grp"]["kt2sw"])
            ch["x"] = _dot(ch["t"].astype(bf16), rhs.astype(bf16))
        yield
        for ch in chains:
            vtop = jnp.where(half[ch["hh"]], ch["grp"]["v2"], 0.0)
            ch["z"] = jnp.concatenate([vtop, -ch["x"]], axis=0).astype(bf16)
            ch["o"] = _dot(ch["bot"], ch["z"])
        yield
        for ch in chains:
            ch["tn"] = _dot_tn(ch["z"], ch["grp"]["kbbb2"])
        yield
        for grp in groups:
            d, rs = grp["d"], grp["rs"]
            c0, c1 = grp["chains"]
            y0_s[d, rs, :] = jnp.where(half[0], c0["o"], c1["o"])
            rp_s[d, rs, :] = grp["rt2"] + pltpu.roll(jnp.where(half[0], c1["o"], c0["o"]), A_HEAD, 1)
            ps = slice(grp["c"] * LANE, (grp["c"] + 1) * LANE)
            nt_s[d, ps, :] = jnp.concatenate([jnp.where(half[0], c0["tn"][:A_HEAD], 0.0),
                                              jnp.where(half[1], c1["tn"][A_HEAD:], 0.0)], axis=0)
            mt_s[d, ps, :] = jnp.where(eye_pair, grp["e_tot"], 0.0) + jnp.concatenate(
                [jnp.where(half[0], c0["tn"][A_HEAD:], 0.0), jnp.where(half[1], c1["tn"][:A_HEAD], 0.0)],
                axis=0)

    def seq_step(seqs, i):
        steps = []
        for q in seqs:
            for d in range(2):
                c = q * nch_seq + (i if d == 0 else nch_seq - 1 - i)
                steps.append((q, d, slice(c * CHUNK, (c + 1) * CHUNK), slice(c * LANE, (c + 1) * LANE),
                              st_s[q, d].astype(bf16)))
        for q, d, rs, ps, sb in steps:
            st_s[q, d] = _dot(sb, mt_s[d, ps, :].astype(bf16)) + nt_s[d, ps, :]
        for q, d, rs, ps, sb in steps:
            y_s[d, rs, :] = y0_s[d, rs, :] + _dot_nt(rp_s[d, rs, :].astype(bf16), sb)

    def finish_a(c, fin):
        rs = slice(c * CHUNK, (c + 1) * CHUNK)
        y = y_s[0, rs, :] + y_s[1, rs, :]
        kmean = 0.5 * (kd_s[0, rs, :] + kd_s[1, rs, :])
        fin[c] = dict(rs=rs, y=y, ysum=_head_sum(y, bd), bsum=_head_sum(r_s[rs, :] * kmean * row(V_RK), bd))

    def finish_b(c, fin):
        f = fin[c]
        f["yc"] = f["y"] - f["ysum"] * (1.0 / A_HEAD)
        f["vsum"] = _head_sum(f["yc"] * f["yc"], bd)

    def finish_c(c, fin):
        f = fin[c]
        rs = f["rs"]
        yn = f["yc"] * lax.rsqrt(f["vsum"] * (1.0 / A_HEAD) + GN_EPS) * row(V_LNG) + row(V_LNB)
        mix[rs, :LANE] = (yn + f["bsum"] * v_s[rs, :]) * _silu(x_ref[rs, 3 * LANE:4 * LANE])

    def finish_fillers(chunks, fin):
        chunks = list(chunks)
        out = []
        for t in range(len(chunks) + 2):
            slot = []
            if t < len(chunks):
                slot.append(functools.partial(finish_a, chunks[t], fin))
            if 0 <= t - 1 < len(chunks):
                slot.append(functools.partial(finish_b, chunks[t - 1], fin))
            if 0 <= t - 2 < len(chunks):
                slot.append(functools.partial(finish_c, chunks[t - 2], fin))
            out.append(slot)
        return out

    def conv_branch():
        u = col(5) * col(6)
        conv = prev(u) * row(V_CW) + u * row(V_CW + 1) + nxt(u) * row(V_CW + 2) + row(V_CB)
        mix[:, LANE:] = col(4) * conv * _silu(col(7))

    def run(stages, slots):
        slots = list(slots)
        for _ in stages:
            if slots:
                for f in slots.pop(0):
                    f()
        for slot in slots:
            for f in slot:
                f()

    hc = nch // 2
    dots = [setup_dots(c) for c in range(hc)]
    first_half = [grp for st in dots for grp in setup_chains(st)]
    second_half, pending = [], {}
    slots = []
    for t in range(hc + 1):
        slot = []
        if t < hc:
            slot.append(functools.partial(lambda c: pending.__setitem__(c, setup_dots(c)), hc + t))
        if t >= 1:
            slot.append(functools.partial(lambda c: second_half.extend(setup_chains(pending[c])), hc + t - 1))
        slots.append(slot)
    run(chain_stages(first_half), slots)
    fin = {}
    if NB >= 2:
        early = list(range(NB // 2))
        late = list(range(NB // 2, NB))
        slots = [[conv_branch]] + [[functools.partial(seq_step, early, i)] for i in range(nch_seq)]
        slots += finish_fillers(range(hc), fin)
        run(chain_stages(second_half), slots)
        for i in range(nch_seq):
            seq_step(late, i)
        run(iter(()), finish_fillers(range(hc, nch), fin))
    else:
        run(chain_stages(second_half), [[conv_branch]])
        ready = []
        for i in range(nch_seq):
            seq_step([0], i)
            if i >= nch_seq // 2:
                ready += [i, nch_seq - 1 - i]
            for c in list(ready):
                stage = fin.get(c, {}).get("stage", 0)
                (finish_a, finish_b, finish_c)[stage](c, fin)
                fin[c]["stage"] = stage + 1
                if stage == 2:
                    ready.remove(c)
        while ready:
            for c in list(ready):
                stage = fin[c]["stage"]
                (finish_a, finish_b, finish_c)[stage](c, fin)
                fin[c]["stage"] = stage + 1
                if stage == 2:
                    ready.remove(c)

    if write_s:
        if not has_prev:
            for pp in range(N_PAIRS):
                if pp != pair:
                    sout[:, pp] = jnp.zeros((NB, 2, 2, A_HEAD, A_HEAD), f32)
        own = sout if has_prev else sout.at[:, pair]
        for q in range(NB):
            for d in range(2):
                for hh in range(2):
                    own[q, d, hh] = st_s[q, d, hh * A_HEAD:(hh + 1) * A_HEAD, hh * A_HEAD:(hh + 1) * A_HEAD]


def _mixer(proj, tables, s0, pair, *, L, n_seq, write_s, s_prev=None):
    vecs, mats, mulo = tables
    has_s0 = s0 is not None
    R = MIXER_ROWS
    nb = R // L
    in_specs = [
        pl.BlockSpec((R, PAIR_COLS), lambda s, j: (s, j)),
        pl.BlockSpec((R, 4 * LORA), lambda s, j: (s, N_PAIR * PAIR_COLS // (4 * LORA))),
        pl.BlockSpec((N_PAIR, N_VEC, LANE), lambda s, j: (0, 0, 0)),
        pl.BlockSpec((N_PAIR, 4, LORA, LANE), lambda s, j: (0, 0, 0, 0)),
        pl.BlockSpec((1, 4 * LORA), lambda s, j: (0, 0)),
    ]
    args = [proj, proj, vecs, mats, mulo]
    if has_s0:
        in_specs.append(pl.BlockSpec((nb, None, 2, 2, A_HEAD, A_HEAD), lambda s, j: (s, pair, 0, j, 0, 0)))
        args.append(s0)
    out_specs = [pl.BlockSpec((R, 2 * LANE), lambda s, j: (s, j))]
    out_shape = [jax.ShapeDtypeStruct((n_seq * L, D_MODEL), f32)]
    aliases = {}
    if write_s:
        out_shape.append(jax.ShapeDtypeStruct((n_seq, N_PAIRS, 2, A_HEADS, A_HEAD, A_HEAD), f32))
        if s_prev is None:
            out_specs.append(pl.BlockSpec((nb, N_PAIRS, 2, 2, A_HEAD, A_HEAD), lambda s, j: (s, 0, 0, j, 0, 0)))
        else:
            out_specs.append(pl.BlockSpec((nb, None, 2, 2, A_HEAD, A_HEAD), lambda s, j: (s, pair, 0, j, 0, 0)))
            aliases = {len(args): 1}
            in_specs.append(pl.BlockSpec(memory_space=pl.ANY))
            args.append(s_prev)
    tok = lambda: pltpu.VMEM((R, LANE), f32)
    tok2 = lambda: pltpu.VMEM((2, R, LANE), f32)
    maps = lambda: pltpu.VMEM((2, 2 * R, LANE), f32)
    scratch = [tok(), tok(), tok(), pltpu.VMEM((R, 4 * LORA), f32), tok2(), tok2(), tok2(), tok2(),
               maps(), maps(), pltpu.VMEM((nb, 2, LANE, LANE), f32)]
    return pl.pallas_call(
        functools.partial(_mixer_kernel, L=L, NB=nb, has_s0=has_s0, write_s=write_s, has_prev=bool(aliases),
                          pair=pair),
        grid=(n_seq // nb, N_PAIR),
        in_specs=in_specs,
        out_specs=out_specs,
        out_shape=out_shape,
        scratch_shapes=scratch,
        input_output_aliases=aliases,
        compiler_params=pltpu.CompilerParams(vmem_limit_bytes=VMEM_LIMIT),
        name="rwkv_conv_mixer_L%d" % L,
    )(*args)


def _attn_kernel(*refs, LQ, LN, NB, NQ, PAST, rope, write_cache, n_prev, pair):
    it = iter(refs)
    q_ref, k_ref, v_ref, g_ref, qg_ref, kg_ref = (next(it) for _ in range(6))
    if rope:
        cosq, sinq, cosk, sink = (next(it) for _ in range(4))
    if PAST:
        ck, cv = next(it), next(it)
    for _ in range(n_prev):
        next(it)
    out_ref = next(it)
    if write_cache:
        nk_ref, nv_ref = next(it), next(it)
    kbuf, vbuf, obuf = it

    bd = _head_block_diag()
    lane = lax.broadcasted_iota(jnp.int32, (1, LANE), 1)
    first_half = (lane & 31) < 16

    def rms(x, g):
        return x * lax.rsqrt(_head_sum(x * x, bd, pieces=1) * (1.0 / C_HEAD) + EPS) * g

    def rotate(x, cos, sin):
        outs = []
        for i in range(x.shape[1] // LANE):
            xb = x[:, i * LANE:(i + 1) * LANE]
            sw = jnp.where(first_half, pltpu.roll(xb, LANE - 16, 1), pltpu.roll(xb, 16, 1))
            outs.append(xb * cos + sw * sin)
        return jnp.concatenate(outs, axis=1)

    def build_keys():
        kn = rms(k_ref[...], kg_ref[...])
        vx = v_ref[...]
        if write_cache:
            nk_own, nv_own = nk_ref, nv_ref
            if not n_prev:
                for pp in range(N_PAIRS):
                    if pp != pair:
                        nk_ref[:, pp] = jnp.zeros((NB, C_KV_HEADS, C_HEAD, LN), f32)
                        nv_ref[:, pp] = jnp.zeros((NB, C_KV_HEADS, C_HEAD, LN), f32)
                nk_own, nv_own = nk_ref.at[:, pair], nv_ref.at[:, pair]
            for b in range(NB):
                knt = kn[b * LN:(b + 1) * LN, :].T
                vxt = vx[b * LN:(b + 1) * LN, :].T
                for h in range(C_KV_HEADS):
                    nk_own[b, h] = knt[h * C_HEAD:(h + 1) * C_HEAD, :]
                    nv_own[b, h] = vxt[h * C_HEAD:(h + 1) * C_HEAD, :]
        if rope:
            kn = rotate(kn, cosk[...], sink[...])
        for b in range(NB):
            for h in range(C_KV_HEADS):
                kbuf[b, h] = kn[b * LN:(b + 1) * LN, h * C_HEAD:(h + 1) * C_HEAD].astype(bf16)
                vbuf[b, h] = vx[b * LN:(b + 1) * LN, h * C_HEAD:(h + 1) * C_HEAD].astype(bf16)

    if NQ == 1:
        build_keys()
    else:
        pl.when(pl.program_id(1) == 0)(build_keys)

    qn = rms(q_ref[...], qg_ref[...])
    if rope:
        qn = rotate(qn, cosq[...], sinq[...])
    qn = (qn * (C_HEAD ** -0.5 * LOG2E)).astype(bf16)
    units = [(b, h) for b in range(NB) for h in range(C_KV_HEADS)]
    heads = [[slice((h * C_GROUP + g) * C_HEAD, (h * C_GROUP + g + 1) * C_HEAD) for g in range(C_GROUP)]
             for h in range(C_KV_HEADS)]

    def scores(b, h):
        q4 = jnp.concatenate([qn[b * LQ:(b + 1) * LQ, sl] for sl in heads[h]], axis=0)
        s_new = _dot_nt(q4, kbuf[b, h])
        if not PAST:
            return s_new
        return jnp.concatenate([_dot(q4, ck[b, h].astype(bf16)), s_new], axis=1)

    s_next = scores(*units[0])
    for n, (b, h) in enumerate(units):
        s = s_next
        if n + 1 < len(units):
            s_next = scores(*units[n + 1])
        p = jnp.exp2(s - jnp.max(s, axis=-1, keepdims=True))
        pb = p.astype(bf16)
        o = _dot(pb[:, PAST:], vbuf[b, h])
        if PAST:
            o = o + _dot_nt(pb[:, :PAST], cv[b, h].astype(bf16))
        o = o / jnp.sum(p, axis=-1, keepdims=True)
        for g, sl in enumerate(heads[h]):
            obuf[b * LQ:(b + 1) * LQ, sl] = o[g * LQ:(g + 1) * LQ]
    out_ref[...] = obuf[...] * _silu(g_ref[...])


def _rope_tables():
    pos = np.arange(DEC_SEQ)
    inv = ROPE_THETA ** (-np.arange(0, ROPE_HALF, 2, dtype=np.float32) / ROPE_HALF)
    ang_r = (pos // GRID_W).astype(np.float32)[:, None] * inv
    ang_c = (pos % GRID_W).astype(np.float32)[:, None] * inv
    cos64 = np.concatenate([np.cos(ang_r), np.cos(ang_r), np.cos(ang_c), np.cos(ang_c)], axis=1)
    sin64 = np.concatenate([-np.sin(ang_r), np.sin(ang_r), -np.sin(ang_c), np.sin(ang_c)], axis=1)
    return jnp.asarray(np.tile(cos64, (1, 2)), f32), jnp.asarray(np.tile(sin64, (1, 2)), f32)


def _attention(proj, q_g, k_g, cache_k, cache_v, pair, tables, *, n_seq, L, sample, kv_prev=None):
    LQ = ATTN_ROWS
    nq = L // LQ
    nb = 1 if sample else ATTN_PROMPT_SEQS
    assert nb == 1 or nq == 1
    past = PAST_LEN if sample else 0
    in_specs = [
        pl.BlockSpec((nb * LQ, D_MODEL), lambda b, i: (b * nq + i, 0)),
        pl.BlockSpec((nb * L, KV_WIDTH), lambda b, i: (b, 2 * D_MODEL // KV_WIDTH)),
        pl.BlockSpec((nb * L, KV_WIDTH), lambda b, i: (b, 2 * D_MODEL // KV_WIDTH + 1)),
        pl.BlockSpec((nb * LQ, D_MODEL), lambda b, i: (b * nq + i, 1)),
        pl.BlockSpec((1, D_MODEL), lambda b, i: (0, 0)),
        pl.BlockSpec((1, KV_WIDTH), lambda b, i: (0, 0)),
    ]
    args = [proj, proj, proj, proj, q_g, k_g]
    if sample:
        cos, sin = tables
        in_specs += [pl.BlockSpec((LQ, LANE), lambda b, i: (i, 0)),
                     pl.BlockSpec((LQ, LANE), lambda b, i: (i, 0)),
                     pl.BlockSpec((L, LANE), lambda b, i: (0, 0)),
                     pl.BlockSpec((L, LANE), lambda b, i: (0, 0))]
        args += [cos, sin, cos, sin]
        cspec = pl.BlockSpec((nb, None, C_KV_HEADS, C_HEAD, PAST_LEN), lambda b, i: (b, pair, 0, 0, 0))
        in_specs += [cspec, cspec]
        args += [cache_k.transpose(0, 1, 2, 4, 3), cache_v.transpose(0, 1, 2, 4, 3)]
    out_specs = [pl.BlockSpec((nb * LQ, D_MODEL), lambda b, i: (b * nq + i, 0))]
    out_shape = [jax.ShapeDtypeStruct((n_seq * L, D_MODEL), f32)]
    aliases = {}
    if not sample:
        if kv_prev is None:
            nspec = pl.BlockSpec((nb, N_PAIRS, C_KV_HEADS, C_HEAD, L), lambda b, i: (b, 0, 0, 0, 0))
        else:
            nspec = pl.BlockSpec((nb, None, C_KV_HEADS, C_HEAD, L), lambda b, i: (b, pair, 0, 0, 0))
        out_specs += [nspec, nspec]
        out_shape += [jax.ShapeDtypeStruct((n_seq, N_PAIRS, C_KV_HEADS, C_HEAD, L), f32)] * 2
        if kv_prev is not None:
            aliases = {len(args): 1, len(args) + 1: 2}
            in_specs += [pl.BlockSpec(memory_space=pl.ANY)] * 2
            args += list(kv_prev)
    return pl.pallas_call(
        functools.partial(_attn_kernel, LQ=LQ, LN=L, NB=nb, NQ=nq, PAST=past, rope=sample, write_cache=not sample,
                          n_prev=len(aliases), pair=pair),
        grid=(n_seq // nb, nq),
        in_specs=in_specs,
        out_specs=out_specs,
        out_shape=out_shape,
        input_output_aliases=aliases,
        scratch_shapes=[pltpu.VMEM((nb, C_KV_HEADS, L, C_HEAD), bf16),
                        pltpu.VMEM((nb, C_KV_HEADS, L, C_HEAD), bf16),
                        pltpu.VMEM((nb * LQ, D_MODEL), f32)],
        compiler_params=pltpu.CompilerParams(vmem_limit_bytes=VMEM_LIMIT),
        name="gqa_sample" if sample else "gqa_prompt",
    )(*args)


def kernel(x_prompt, x_sample, c, state_rwkv, cache_k, cache_v, c_ctx, w_ada, b_ada, norm_g, final_g,
           w_in_e, mu_shift, lora_w2, w0, lora_a2, a0, k_k, k_a, r_k, lnx_g, lnx_b, conv_w, conv_b,
           w_out_e, w_in_o, q_norm_g, k_norm_g, w_out_o):
    xs = [x_prompt.reshape(STREAM_TOK, D_MODEL), x_sample.reshape(STREAM_TOK, D_MODEL)]
    cond = jnp.concatenate([c_ctx[None, :], c, jnp.zeros((N_MOD - 1 - DEC_BATCH, D_MODEL), f32)], axis=0)
    mod = _adaln(cond, w_ada, b_ada).reshape(DEPTH, N_MOD, 1, 3 * D_MODEL)

    rope = _rope_tables()
    final_g2 = final_g.reshape(1, D_MODEL)

    w_ins, w_outs = [], []
    for layer in range(DEPTH):
        p = layer // 2
        if layer % 2 == 0:
            w_ins.append(_even_in_weight(w_in_e[p]))
            w_outs.append(w_out_e[p].reshape(2, N_PAIR, LANE, D_MODEL).transpose(1, 0, 2, 3)
                          .reshape(D_MODEL, D_MODEL).astype(bf16))
        else:
            wi = w_in_o[p]
            w_ins.append(jnp.concatenate([wi[:, :D_MODEL], wi[:, D_MODEL + 2 * KV_WIDTH:],
                                          wi[:, D_MODEL:D_MODEL + 2 * KV_WIDTH]], axis=1).astype(bf16))
            w_outs.append(w_out_o[p].astype(bf16))
    inp = lambda layer: (mod[layer], norm_g[layer].reshape(1, D_MODEL), w_ins[layer])

    new_rwkv = new_kv = None
    (proj_p,) = _proj(xs[0], sample=False, inp=inp(0))
    (proj_s,) = _proj(xs[1], sample=True, inp=inp(0))
    for layer in range(DEPTH):
        p = layer // 2
        if layer % 2 == 0:
            tables = _mixer_tables(p, mu_shift, lora_w2, w0, lora_a2, a0, k_k, k_a, r_k, lnx_g, lnx_b,
                                   conv_w, conv_b)
            act_p, new_rwkv = _mixer(proj_p, tables, None, p, L=SEQ, n_seq=BATCH, write_s=True, s_prev=new_rwkv)
            (act_s,) = _mixer(proj_s, tables, state_rwkv, p, L=DEC_SEQ, n_seq=DEC_BATCH, write_s=False)
        else:
            q_g = jnp.tile(q_norm_g[p], C_Q_HEADS).reshape(1, D_MODEL)
            k_g = jnp.tile(k_norm_g[p], C_KV_HEADS).reshape(1, KV_WIDTH)
            act_p, *new_kv = _attention(proj_p, q_g, k_g, None, None, p, None, n_seq=BATCH, L=SEQ,
                                        sample=False, kv_prev=new_kv)
            (act_s,) = _attention(proj_s, q_g, k_g, cache_k, cache_v, p, rope, n_seq=DEC_BATCH,
                                  L=DEC_SEQ, sample=True)
        if layer < DEPTH - 1:
            xs[0], proj_p = _proj(xs[0], sample=False, out=(act_p, mod[layer], w_outs[layer]), inp=inp(layer + 1))
            xs[1], proj_s = _proj(xs[1], sample=True, out=(act_s, mod[layer], w_outs[layer]), inp=inp(layer + 1))
        else:
            (xs[0],) = _proj(xs[0], sample=False, out=(act_p, mod[layer], w_outs[layer]), final_g=final_g2)
            (xs[1],) = _proj(xs[1], sample=True, out=(act_s, mod[layer], w_outs[layer]), final_g=final_g2)
    return (xs[0].reshape(BATCH, SEQ, D_MODEL), xs[1].reshape(DEC_BATCH, DEC_SEQ, D_MODEL),
            new_rwkv, new_kv[0].transpose(0, 1, 2, 4, 3), new_kv[1].transpose(0, 1, 2, 4, 3))
```

```python
import functools

import numpy as np
import jax
import jax.numpy as jnp
from jax import lax
from jax.experimental import pallas as pl
from jax.experimental.pallas import tpu as pltpu

f32 = jnp.float32
bf16 = jnp.bfloat16
HI = lax.Precision.HIGHEST

D_MODEL = 1024
BATCH = 32
SEQ = 256
DEPTH = 4
DEC_BATCH = 8
DEC_SEQ = 1024
PAST_LEN = 512
GRID_W = 64
EPS = 1e-6
A_WIDTH = 512
A_HEAD = 64
A_HEADS = 8
LORA = 64
GN_EPS = 64e-5
B_WIDTH = 512
C_HEAD = 64
C_Q_HEADS = 16
C_KV_HEADS = 4
C_GROUP = 4
KV_WIDTH = 256
ROPE_THETA = 10000.0
ROPE_HALF = 32
A_SHIFT = 3 * A_WIDTH + 4 * LORA
EVEN_IN = A_SHIFT + A_WIDTH + 4 * B_WIDTH
ODD_IN = 2 * D_MODEL + 2 * KV_WIDTH

STREAM_TOK = BATCH * SEQ
assert STREAM_TOK == DEC_BATCH * DEC_SEQ
PROJ_ROWS = 512
ATTN_ROWS = 256
ATTN_PROMPT_SEQS = 4
LOG2E = 1.4426950408889634
N_MOD = 16
CHUNK = 64
LANE = 128
VMEM_LIMIT = 56 * 1024 * 1024
N_PAIR = A_WIDTH // LANE
PAIR_COLS = 8 * LANE
N_VEC = 16
MIXER_ROWS = 1024
N_PAIRS = DEPTH // 2


def _sigmoid(x):
    return 1.0 / (1.0 + jnp.exp(-x))


def _silu(x):
    return x * _sigmoid(x)


def _dot(a, b, precision=None):
    return jnp.dot(a, b, precision=precision, preferred_element_type=f32)


def _dot_nt(a, b, precision=None):
    return lax.dot_general(a, b, (((1,), (1,)), ((), ())), precision=precision,
                           preferred_element_type=f32)


def _dot_tn(a, b, precision=None):
    return lax.dot_general(a, b, (((0,), (0,)), ((), ())), precision=precision,
                           preferred_element_type=f32)


def _mod_row(sample):
    if sample:
        return lambda i: 1 + i // (DEC_SEQ // PROJ_ROWS)
    return lambda i: 0


def _head_block_diag():
    ri = lax.broadcasted_iota(jnp.int32, (LANE, LANE), 0) >> 6
    ci = lax.broadcasted_iota(jnp.int32, (LANE, LANE), 1) >> 6
    return (ri == ci).astype(bf16)


def _head_sum(x, bd):
    n = x.shape[1] // LANE
    xb = x.astype(bf16)
    cols = [_dot(xb[:, i * LANE:(i + 1) * LANE], bd) for i in range(n)]
    return cols[0] if n == 1 else jnp.concatenate(cols, axis=1)


def _adaln_kernel(cond_ref, w_ref, b_ref, o_ref):
    o_ref[...] = _dot(_silu(cond_ref[...]), w_ref[...], HI) + b_ref[...]


def _adaln(cond, w_ada, b_ada):
    tn = 768
    return pl.pallas_call(
        _adaln_kernel,
        grid=(DEPTH, 3 * D_MODEL // tn),
        in_specs=[
            pl.BlockSpec((N_MOD, D_MODEL), lambda l, n: (0, 0)),
            pl.BlockSpec((None, D_MODEL, tn), lambda l, n: (l, 0, n)),
            pl.BlockSpec((None, 1, tn), lambda l, n: (l, 0, n)),
        ],
        out_specs=pl.BlockSpec((None, N_MOD, tn), lambda l, n: (l, 0, n)),
        out_shape=jax.ShapeDtypeStruct((DEPTH, N_MOD, 3 * D_MODEL), f32),
        compiler_params=pltpu.CompilerParams(vmem_limit_bytes=VMEM_LIMIT),
        name="adaln",
    )(cond, w_ada, b_ada.reshape(DEPTH, 1, 3 * D_MODEL))


def _proj_kernel(*refs, has_out, has_in, final):
    it = iter(refs)
    if has_out:
        a_ref, x_ref, modp_ref, wo_ref = (next(it) for _ in range(4))
        fg_ref = next(it) if final else None
    else:
        x_ref = next(it)
    if has_in:
        modn_ref, g_ref, wi_ref = (next(it) for _ in range(3))
    x = x_ref[...]
    if has_out:
        x = x + modp_ref[:, 2 * D_MODEL:] * _dot(a_ref[...].astype(bf16), wo_ref[...])
        xo_ref = next(it)
        if final:
            xo_ref[...] = x * lax.rsqrt(jnp.mean(x * x, axis=-1, keepdims=True) + EPS) * fg_ref[...]
        else:
            xo_ref[...] = x
    if has_in:
        p_ref = next(it)
        y = x * lax.rsqrt(jnp.mean(x * x, axis=-1, keepdims=True) + EPS) * g_ref[...]
        m = modn_ref[...]
        h = y * (1.0 + m[:, D_MODEL:2 * D_MODEL]) + m[:, :D_MODEL]
        p_ref[...] = _dot(h.astype(bf16), wi_ref[...])


def _proj(x, *, sample, out=None, inp=None, final_g=None):
    mrow = _mod_row(sample)
    tm = PROJ_ROWS
    rows = lambda w: pl.BlockSpec((tm, w), lambda i: (i, 0))
    const = lambda shape: pl.BlockSpec(shape, lambda i: (0,) * len(shape), pipeline_mode=pl.Buffered(1))
    modspec = pl.BlockSpec((None, 1, 3 * D_MODEL), lambda i: (mrow(i), 0, 0))
    in_specs, args, out_specs, out_shape = [], [], [], []
    if out is not None:
        act, mod_prev, w_out = out
        in_specs += [rows(D_MODEL), rows(D_MODEL), modspec, const((D_MODEL, D_MODEL))]
        args += [act, x, mod_prev, w_out]
        if final_g is not None:
            in_specs.append(const((1, D_MODEL)))
            args.append(final_g)
        out_specs.append(rows(D_MODEL))
        out_shape.append(jax.ShapeDtypeStruct((STREAM_TOK, D_MODEL), f32))
    else:
        in_specs.append(rows(D_MODEL))
        args.append(x)
    if inp is not None:
        mod_next, g, w_in = inp
        n = w_in.shape[1]
        in_specs += [modspec, const((1, D_MODEL)), const((D_MODEL, n))]
        args += [mod_next, g, w_in]
        out_specs.append(rows(n))
        out_shape.append(jax.ShapeDtypeStruct((STREAM_TOK, n), f32))
    return pl.pallas_call(
        functools.partial(_proj_kernel, has_out=out is not None, has_in=inp is not None,
                          final=final_g is not None),
        grid=(STREAM_TOK // tm,),
        in_specs=in_specs,
        out_specs=out_specs,
        out_shape=out_shape,
        compiler_params=pltpu.CompilerParams(vmem_limit_bytes=VMEM_LIMIT),
        name="proj_%s%s" % ("o" if out is not None else "", "i" if inp is not None else ""),
    )(*args)


def _even_column_starts():
    bases = [0, A_WIDTH, 2 * A_WIDTH, A_SHIFT, A_SHIFT + A_WIDTH, A_SHIFT + A_WIDTH + B_WIDTH,
             A_SHIFT + A_WIDTH + 2 * B_WIDTH, A_SHIFT + A_WIDTH + 3 * B_WIDTH]
    return [(b + j * LANE, LANE) for j in range(N_PAIR) for b in bases] + [(3 * A_WIDTH, 4 * LORA)]


def _even_column_order():
    return np.concatenate([s + np.arange(w) for s, w in _even_column_starts()])


def _even_in_weight(w):
    return jnp.concatenate([w[:, s:s + n] for s, n in _even_column_starts()], axis=1).astype(bf16)


def _mixer_tables(p, mu_shift, lora_w2, w0, lora_a2, a0, k_k, k_a, r_k, lnx_g, lnx_b, conv_w, conv_b):
    pair = lambda v: v.reshape(-1, N_PAIR, LANE).transpose(1, 0, 2)
    mu = mu_shift[p]
    rows = [k_k[p][None], k_a[p][None], r_k[p].reshape(1, A_WIDTH), lnx_g[p][None], lnx_b[p][None],
            conv_b[p][None], conv_w[p], w0[p], a0[p],
            mu[None, :A_WIDTH], mu[None, A_WIDTH:2 * A_WIDTH], mu[None, 2 * A_WIDTH:3 * A_WIDTH]]
    vecs = pair(jnp.concatenate(rows, axis=0))
    assert vecs.shape == (N_PAIR, N_VEC, LANE)
    mats = pair(jnp.concatenate([lora_w2[p], lora_a2[p]], axis=0).reshape(4 * LORA, A_WIDTH))
    mats = mats.reshape(N_PAIR, 4, LORA, LANE)
    return vecs, mats, mu[None, 3 * A_WIDTH:]


V_KK, V_KA, V_RK, V_LNG, V_LNB, V_CB, V_CW, V_W0, V_A0, V_MU = 0, 1, 2, 3, 4, 5, 6, 9, 11, 13


def _scan_masks():
    ri = lax.broadcasted_iota(jnp.int32, (CHUNK, CHUNK), 0)
    ci = lax.broadcasted_iota(jnp.int32, (CHUNK, CHUNK), 1)
    eye = ri == ci
    ri2 = lax.broadcasted_iota(jnp.int32, (CHUNK, 2 * CHUNK), 0)
    ci2 = lax.broadcasted_iota(jnp.int32, (CHUNK, 2 * CHUNK), 1) & (CHUNK - 1)
    strict2 = (ri2 > ci2, ri2 < ci2)
    incl2 = (ri2 >= ci2, ri2 <= ci2)
    levels = []
    for sh in range(6):
        levels.append(((ri >> (sh + 1)) == (ci >> (sh + 1))) & ((ri >> sh) != (ci >> sh)))
    return eye, strict2, incl2, levels


def _mixer_kernel(*refs, L, NB, has_s0, write_s, has_prev, pair):
    it = iter(refs)
    x_ref, lo_ref, vec_ref, mat_ref, mulo = (next(it) for _ in range(5))
    s0 = next(it) if has_s0 else None
    if has_prev:
        next(it)
    mix = next(it)
    sout = next(it) if write_s else None
    r_s, k_s, v_s, lo_s, kd_s, y_s, rp_s, y0_s, mt_s, nt_s, st_s = it

    j = pl.program_id(1)
    vec = vec_ref[j]
    row = lambda i: vec[i:i + 1, :]
    col = lambda i: x_ref[:, i * LANE:(i + 1) * LANE]
    lora_w = [mat_ref[j, i].astype(bf16) for i in range(4)]

    R = NB * L
    nch, nch_seq = R // CHUNK, L // CHUNK
    rows = lax.broadcasted_iota(jnp.int32, (R, 1), 0)
    first, last = (rows & (L - 1)) == 0, (rows & (L - 1)) == L - 1

    def prev(f):
        return jnp.where(first, 0.0, pltpu.roll(f, 1, 0))

    def nxt(f):
        return jnp.where(last, 0.0, pltpu.roll(f, R - 1, 0))

    def tshift(f, mu):
        return f + mu * (0.5 * (prev(f) + nxt(f)) - f)

    bd = _head_block_diag()

    r_s[...] = tshift(col(0), row(V_MU))
    k_s[...] = tshift(col(1), row(V_MU + 1))
    v_s[...] = tshift(col(2), row(V_MU + 2))
    lo_s[...] = tshift(lo_ref[...], mulo[...])
    for q in range(NB):
        for d in range(2):
            st_s[q, d] = jnp.zeros((LANE, LANE), f32)
            if has_s0:
                for hh in range(2):
                    st_s[q, d, hh * A_HEAD:(hh + 1) * A_HEAD, hh * A_HEAD:(hh + 1) * A_HEAD] = s0[q, d, hh]

    eye, strict2, incl2, levels = _scan_masks()
    eye_f = eye.astype(f32)
    lane = lax.broadcasted_iota(jnp.int32, (1, LANE), 1)
    half = (lane < A_HEAD, lane >= A_HEAD)
    ri = lax.broadcasted_iota(jnp.int32, (LANE, LANE), 0)
    ci = lax.broadcasted_iota(jnp.int32, (LANE, LANE), 1)
    eye_pair = ri == ci
    crow = lax.broadcasted_iota(jnp.int32, (CHUNK, 1), 0)

    def setup_dots(c):
        rs = slice(c * CHUNK, (c + 1) * CHUNK)
        k2, lo = k_s[rs, :], lo_s[rs, :]
        kk = k2 * row(V_KK)
        st = dict(c=c, rs=rs, k2=k2, kk=kk, ssq=_head_sum(kk * kk, bd), wl=[], al=[])
        for d in range(2):
            lwd = lo[:, d * LORA:(d + 1) * LORA]
            lad = lo[:, 2 * LORA + d * LORA:2 * LORA + (d + 1) * LORA]
            st["wl"].append(_dot(jnp.tanh(lwd).astype(bf16), lora_w[d]))
            st["al"].append(_dot(lad.astype(bf16), lora_w[2 + d]))
        return st

    def setup_chains(st):
        c, rs, k2 = st["c"], st["rs"], st["k2"]
        r2, v2 = r_s[rs, :], v_s[rs, :]
        kap2 = st["kk"] / jnp.maximum(jnp.sqrt(st["ssq"]), 1e-12)
        groups = []
        for d in range(2):
            wl = row(V_W0 + d) + st["wl"][d]
            softplus = jnp.maximum(-wl, 0.0) + jnp.log(1.0 + jnp.exp(-jnp.abs(wl)))
            lw = -jnp.exp(-softplus - 0.5)
            g = lw
            for sh in range(6):
                stp = 1 << sh
                if d == 0:
                    g = g + jnp.where(crow >= stp, pltpu.roll(g, stp, 0), 0.0)
                else:
                    g = g + jnp.where(crow < CHUNK - stp, pltpu.roll(g, CHUNK - stp, 0), 0.0)
            a = _sigmoid(row(V_A0 + d) + st["al"][d])
            kd2 = k2 * (1.0 + (a - 1.0) * row(V_KA))
            kd_s[d, rs, :] = kd2
            b2 = a * kap2
            tot = g[CHUNK - 1:CHUNK, :] if d == 0 else g[0:1, :]
            e_ng = jnp.exp(-g)
            e_rem = jnp.exp(tot - g)
            kt2 = kap2 * jnp.exp(g - lw)
            rt2 = r2 * jnp.exp(g)
            lhs2 = jnp.concatenate([kt2, rt2], axis=0).astype(bf16)
            grp = dict(
                d=d, c=c, rs=rs, rt2=rt2, v2=v2, v2b=v2.astype(bf16), e_tot=jnp.exp(tot),
                kt2sw=pltpu.roll(kt2, A_HEAD, 1),
                rhs2=jnp.concatenate([kd2 * e_ng, b2 * e_ng], axis=0).astype(bf16),
                kbbb2=jnp.concatenate([kd2 * e_rem, b2 * e_rem], axis=0).astype(bf16), chains=[])
            for hh in range(2):
                ch = dict(grp=grp, d=d, hh=hh)
                lhs = jnp.where(half[hh], lhs2, jnp.zeros_like(lhs2))
                ch["a"] = _dot_nt(lhs, grp["rhs2"])
                grp["chains"].append(ch)
            groups.append(grp)
        return groups

    def chain_stages(groups):
        chains = [ch for grp in groups for ch in grp["chains"]]
        for ch in chains:
            top = jnp.where(strict2[ch["d"]], ch["a"][:CHUNK], 0.0)
            ch["bot"] = jnp.where(incl2[ch["d"]], ch["a"][CHUNK:], 0.0).astype(bf16)
            ch["akk"] = top[:, :CHUNK].astype(bf16)
            akb = top[:, CHUNK:]
            ch["ab"] = akb.astype(bf16)
            ch["t"] = eye_f - jnp.where(levels[0], akb, 0.0)
        yield
        for m in levels[1:]:
            for ch in chains:
                ch["tb"] = ch["t"].astype(bf16)
                off = jnp.where(m, ch["ab"], jnp.zeros_like(ch["ab"]))
                ch["p"] = _dot(off, ch["tb"]).astype(bf16)
            yield
            for ch in chains:
                ch["t"] = ch["t"] - _dot(ch["tb"], ch["p"])
            yield
        for ch in chains:
            ch["w0v"] = _dot(ch["akk"], ch["grp"]["v2b"])
        yield
        for ch in chains:
            rhs = jnp.where(half[ch["hh"]], ch["w0v"], ch["grp"]["kt2sw"])
            ch["x"] = _dot(ch["t"].astype(bf16), rhs.astype(bf16))
        yield
        for ch in chains:
            vtop = jnp.where(half[ch["hh"]], ch["grp"]["v2"], 0.0)
            ch["z"] = jnp.concatenate([vtop, -ch["x"]], axis=0).astype(bf16)
            ch["o"] = _dot(ch["bot"], ch["z"])
        yield
        for ch in chains:
            ch["tn"] = _dot_tn(ch["z"], ch["grp"]["kbbb2"])
        yield
        for grp in groups:
            d, rs = grp["d"], grp["rs"]
            c0, c1 = grp["chains"]
            y0_s[d, rs, :] = jnp.where(half[0], c0["o"], c1["o"])
            rp_s[d, rs, :] = grp["rt2"] + pltpu.roll(jnp.where(half[0], c1["o"], c0["o"]), A_HEAD, 1)
            ps = slice(grp["c"] * LANE, (grp["c"] + 1) * LANE)
            nt_s[d, ps, :] = jnp.concatenate([jnp.where(half[0], c0["tn"][:A_HEAD], 0.0),
                                              jnp.where(half[1], c1["tn"][A_HEAD:], 0.0)], axis=0)
            mt_s[d, ps, :] = jnp.where(eye_pair, grp["e_tot"], 0.0) + jnp.concatenate(
                [jnp.where(half[0], c0["tn"][A_HEAD:], 0.0), jnp.where(half[1], c1["tn"][:A_HEAD], 0.0)],
                axis=0)

    def seq_step(seqs, i):
        steps = []
        for q in seqs:
            for d in range(2):
                c = q * nch_seq + (i if d == 0 else nch_seq - 1 - i)
                steps.append((q, d, slice(c * CHUNK, (c + 1) * CHUNK), slice(c * LANE, (c + 1) * LANE),
                              st_s[q, d].astype(bf16)))
        for q, d, rs, ps, sb in steps:
            st_s[q, d] = _dot(sb, mt_s[d, ps, :].astype(bf16)) + nt_s[d, ps, :]
        for q, d, rs, ps, sb in steps:
            y_s[d, rs, :] = y0_s[d, rs, :] + _dot_nt(rp_s[d, rs, :].astype(bf16), sb)

    def finish_a(c, fin):
        rs = slice(c * CHUNK, (c + 1) * CHUNK)
        y = y_s[0, rs, :] + y_s[1, rs, :]
        kmean = 0.5 * (kd_s[0, rs, :] + kd_s[1, rs, :])
        fin[c] = dict(rs=rs, y=y, ysum=_head_sum(y, bd), bsum=_head_sum(r_s[rs, :] * kmean * row(V_RK), bd))

    def finish_b(c, fin):
        f = fin[c]
        f["yc"] = f["y"] - f["ysum"] * (1.0 / A_HEAD)
        f["vsum"] = _head_sum(f["yc"] * f["yc"], bd)

    def finish_c(c, fin):
        f = fin[c]
        rs = f["rs"]
        yn = f["yc"] * lax.rsqrt(f["vsum"] * (1.0 / A_HEAD) + GN_EPS) * row(V_LNG) + row(V_LNB)
        mix[rs, :LANE] = (yn + f["bsum"] * v_s[rs, :]) * _silu(x_ref[rs, 3 * LANE:4 * LANE])

    def finish_fillers(chunks, fin):
        chunks = list(chunks)
        out = []
        for t in range(len(chunks) + 2):
            slot = []
            if t < len(chunks):
                slot.append(functools.partial(finish_a, chunks[t], fin))
            if 0 <= t - 1 < len(chunks):
                slot.append(functools.partial(finish_b, chunks[t - 1], fin))
            if 0 <= t - 2 < len(chunks):
                slot.append(functools.partial(finish_c, chunks[t - 2], fin))
            out.append(slot)
        return out

    def conv_branch():
        u = col(5) * col(6)
        conv = prev(u) * row(V_CW) + u * row(V_CW + 1) + nxt(u) * row(V_CW + 2) + row(V_CB)
        mix[:, LANE:] = col(4) * conv * _silu(col(7))

    def run(stages, slots):
        slots = list(slots)
        for _ in stages:
            if slots:
                for f in slots.pop(0):
                    f()
        for slot in slots:
            for f in slot:
                f()

    hc = nch // 2
    dots = [setup_dots(c) for c in range(hc)]
    first_half = [grp for st in dots for grp in setup_chains(st)]
    second_half, pending = [], {}
    slots = []
    for t in range(hc + 1):
        slot = []
        if t < hc:
            slot.append(functools.partial(lambda c: pending.__setitem__(c, setup_dots(c)), hc + t))
        if t >= 1:
            slot.append(functools.partial(lambda c: second_half.extend(setup_chains(pending[c])), hc + t - 1))
        slots.append(slot)
    run(chain_stages(first_half), slots)
    fin = {}
    if NB >= 2:
        early = list(range(NB // 2))
        late = list(range(NB // 2, NB))
        slots = [[conv_branch]] + [[functools.partial(seq_step, early, i)] for i in range(nch_seq)]
        slots += finish_fillers(range(hc), fin)
        run(chain_stages(second_half), slots)
        for i in range(nch_seq):
            seq_step(late, i)
        run(iter(()), finish_fillers(range(hc, nch), fin))
    else:
        run(chain_stages(second_half), [[conv_branch]])
        ready = []
        for i in range(nch_seq):
            seq_step([0], i)
            if i >= nch_seq // 2:
                ready += [i, nch_seq - 1 - i]
            for c in list(ready):
                stage = fin.get(c, {}).get("stage", 0)
                (finish_a, finish_b, finish_c)[stage](c, fin)
                fin[c]["stage"] = stage + 1
                if stage == 2:
                    ready.remove(c)
        while ready:
            for c in list(ready):
                stage = fin[c]["stage"]
                (finish_a, finish_b, finish_c)[stage](c, fin)
                fin[c]["stage"] = stage + 1
                if stage == 2:
                    ready.remove(c)

    if write_s:
        if not has_prev:
            for pp in range(N_PAIRS):
                if pp != pair:
                    sout[:, pp] = jnp.zeros((NB, 2, 2, A_HEAD, A_HEAD), f32)
        own = sout if has_prev else sout.at[:, pair]
        for q in range(NB):
            for d in range(2):
                for hh in range(2):
                    own[q, d, hh] = st_s[q, d, hh * A_HEAD:(hh + 1) * A_HEAD, hh * A_HEAD:(hh + 1) * A_HEAD]


def _mixer(proj, tables, s0, pair, *, L, n_seq, write_s, s_prev=None):
    vecs, mats, mulo = tables
    has_s0 = s0 is not None
    R = MIXER_ROWS
    nb = R // L
    in_specs = [
        pl.BlockSpec((R, PAIR_COLS), lambda s, j: (s, j)),
        pl.BlockSpec((R, 4 * LORA), lambda s, j: (s, N_PAIR * PAIR_COLS // (4 * LORA))),
        pl.BlockSpec((N_PAIR, N_VEC, LANE), lambda s, j: (0, 0, 0)),
        pl.BlockSpec((N_PAIR, 4, LORA, LANE), lambda s, j: (0, 0, 0, 0)),
        pl.BlockSpec((1, 4 * LORA), lambda s, j: (0, 0)),
    ]
    args = [proj, proj, vecs, mats, mulo]
    if has_s0:
        in_specs.append(pl.BlockSpec((nb, None, 2, 2, A_HEAD, A_HEAD), lambda s, j: (s, pair, 0, j, 0, 0)))
        args.append(s0)
    out_specs = [pl.BlockSpec((R, 2 * LANE), lambda s, j: (s, j))]
    out_shape = [jax.ShapeDtypeStruct((n_seq * L, D_MODEL), f32)]
    aliases = {}
    if write_s:
        out_shape.append(jax.ShapeDtypeStruct((n_seq, N_PAIRS, 2, A_HEADS, A_HEAD, A_HEAD), f32))
        if s_prev is None:
            out_specs.append(pl.BlockSpec((nb, N_PAIRS, 2, 2, A_HEAD, A_HEAD), lambda s, j: (s, 0, 0, j, 0, 0)))
        else:
            out_specs.append(pl.BlockSpec((nb, None, 2, 2, A_HEAD, A_HEAD), lambda s, j: (s, pair, 0, j, 0, 0)))
            aliases = {len(args): 1}
            in_specs.append(pl.BlockSpec(memory_space=pl.ANY))
            args.append(s_prev)
    tok = lambda: pltpu.VMEM((R, LANE), f32)
    tok2 = lambda: pltpu.VMEM((2, R, LANE), f32)
    maps = lambda: pltpu.VMEM((2, 2 * R, LANE), f32)
    scratch = [tok(), tok(), tok(), pltpu.VMEM((R, 4 * LORA), f32), tok2(), tok2(), tok2(), tok2(),
               maps(), maps(), pltpu.VMEM((nb, 2, LANE, LANE), f32)]
    return pl.pallas_call(
        functools.partial(_mixer_kernel, L=L, NB=nb, has_s0=has_s0, write_s=write_s, has_prev=bool(aliases),
                          pair=pair),
        grid=(n_seq // nb, N_PAIR),
        in_specs=in_specs,
        out_specs=out_specs,
        out_shape=out_shape,
        scratch_shapes=scratch,
        input_output_aliases=aliases,
        compiler_params=pltpu.CompilerParams(vmem_limit_bytes=VMEM_LIMIT),
        name="rwkv_conv_mixer_L%d" % L,
    )(*args)


def _attn_kernel(*refs, LQ, LN, NB, NQ, PAST, rope, write_cache, n_prev, pair):
    it = iter(refs)
    q_ref, k_ref, v_ref, g_ref, qg_ref, kg_ref = (next(it) for _ in range(6))
    if rope:
        cosq, sinq, cosk, sink = (next(it) for _ in range(4))
    if PAST:
        ck, cv = next(it), next(it)
    for _ in range(n_prev):
        next(it)
    out_ref = next(it)
    if write_cache:
        nk_ref, nv_ref = next(it), next(it)
    kbuf, vbuf, obuf = it

    bd = _head_block_diag()
    lane = lax.broadcasted_iota(jnp.int32, (1, LANE), 1)
    first_half = (lane & 31) < 16

    def rms(x, g):
        return x * lax.rsqrt(_head_sum(x * x, bd) * (1.0 / C_HEAD) + EPS) * g

    def rotate(x, cos, sin):
        outs = []
        for i in range(x.shape[1] // LANE):
            xb = x[:, i * LANE:(i + 1) * LANE]
            sw = jnp.where(first_half, pltpu.roll(xb, LANE - 16, 1), pltpu.roll(xb, 16, 1))
            outs.append(xb * cos + sw * sin)
        return jnp.concatenate(outs, axis=1)

    def build_keys():
        kn = rms(k_ref[...], kg_ref[...])
        vx = v_ref[...]
        if write_cache:
            nk_own, nv_own = nk_ref, nv_ref
            if not n_prev:
                for pp in range(N_PAIRS):
                    if pp != pair:
                        nk_ref[:, pp] = jnp.zeros((NB, C_KV_HEADS, C_HEAD, LN), f32)
                        nv_ref[:, pp] = jnp.zeros((NB, C_KV_HEADS, C_HEAD, LN), f32)
                nk_own, nv_own = nk_ref.at[:, pair], nv_ref.at[:, pair]
            for b in range(NB):
                knt = kn[b * LN:(b + 1) * LN, :].T
                vxt = vx[b * LN:(b + 1) * LN, :].T
                for h in range(C_KV_HEADS):
                    nk_own[b, h] = knt[h * C_HEAD:(h + 1) * C_HEAD, :]
                    nv_own[b, h] = vxt[h * C_HEAD:(h + 1) * C_HEAD, :]
        if rope:
            kn = rotate(kn, cosk[...], sink[...])
        for b in range(NB):
            for h in range(C_KV_HEADS):
                kbuf[b, h] = kn[b * LN:(b + 1) * LN, h * C_HEAD:(h + 1) * C_HEAD].astype(bf16)
                vbuf[b, h] = vx[b * LN:(b + 1) * LN, h * C_HEAD:(h + 1) * C_HEAD].astype(bf16)

    if NQ == 1:
        build_keys()
    else:
        pl.when(pl.program_id(1) == 0)(build_keys)

    qn = rms(q_ref[...], qg_ref[...])
    if rope:
        qn = rotate(qn, cosq[...], sinq[...])
    qn = (qn * (C_HEAD ** -0.5 * LOG2E)).astype(bf16)
    units = [(b, h) for b in range(NB) for h in range(C_KV_HEADS)]
    heads = [[slice((h * C_GROUP + g) * C_HEAD, (h * C_GROUP + g + 1) * C_HEAD) for g in range(C_GROUP)]
             for h in range(C_KV_HEADS)]

    def scores(b, h):
        q4 = jnp.concatenate([qn[b * LQ:(b + 1) * LQ, sl] for sl in heads[h]], axis=0)
        s_new = _dot_nt(q4, kbuf[b, h])
        if not PAST:
            return s_new
        return jnp.concatenate([_dot(q4, ck[b, h].astype(bf16)), s_new], axis=1)

    s_next = scores(*units[0])
    for n, (b, h) in enumerate(units):
        s = s_next
        if n + 1 < len(units):
            s_next = scores(*units[n + 1])
        p = jnp.exp2(s - jnp.max(s, axis=-1, keepdims=True))
        pb = p.astype(bf16)
        o = _dot(pb[:, PAST:], vbuf[b, h])
        if PAST:
            o = o + _dot_nt(pb[:, :PAST], cv[b, h].astype(bf16))
        o = o / jnp.sum(p, axis=-1, keepdims=True)
        for g, sl in enumerate(heads[h]):
            obuf[b * LQ:(b + 1) * LQ, sl] = o[g * LQ:(g + 1) * LQ]
    out_ref[...] = obuf[...] * _silu(g_ref[...])


def _rope_tables():
    pos = np.arange(DEC_SEQ)
    inv = ROPE_THETA ** (-np.arange(0, ROPE_HALF, 2, dtype=np.float32) / ROPE_HALF)
    ang_r = (pos // GRID_W).astype(np.float32)[:, None] * inv
    ang_c = (pos % GRID_W).astype(np.float32)[:, None] * inv
    cos64 = np.concatenate([np.cos(ang_r), np.cos(ang_r), np.cos(ang_c), np.cos(ang_c)], axis=1)
    sin64 = np.concatenate([-np.sin(ang_r), np.sin(ang_r), -np.sin(ang_c), np.sin(ang_c)], axis=1)
    return jnp.asarray(np.tile(cos64, (1, 2)), f32), jnp.asarray(np.tile(sin64, (1, 2)), f32)


def _attention(proj, q_g, k_g, cache_k, cache_v, pair, tables, *, n_seq, L, sample, kv_prev=None):
    LQ = ATTN_ROWS
    nq = L // LQ
    nb = 1 if sample else ATTN_PROMPT_SEQS
    assert nb == 1 or nq == 1
    past = PAST_LEN if sample else 0
    in_specs = [
        pl.BlockSpec((nb * LQ, D_MODEL), lambda b, i: (b * nq + i, 0)),
        pl.BlockSpec((nb * L, KV_WIDTH), lambda b, i: (b, 2 * D_MODEL // KV_WIDTH)),
        pl.BlockSpec((nb * L, KV_WIDTH), lambda b, i: (b, 2 * D_MODEL // KV_WIDTH + 1)),
        pl.BlockSpec((nb * LQ, D_MODEL), lambda b, i: (b * nq + i, 1)),
        pl.BlockSpec((1, D_MODEL), lambda b, i: (0, 0)),
        pl.BlockSpec((1, KV_WIDTH), lambda b, i: (0, 0)),
    ]
    args = [proj, proj, proj, proj, q_g, k_g]
    if sample:
        cos, sin = tables
        in_specs += [pl.BlockSpec((LQ, LANE), lambda b, i: (i, 0)),
                     pl.BlockSpec((LQ, LANE), lambda b, i: (i, 0)),
                     pl.BlockSpec((L, LANE), lambda b, i: (0, 0)),
                     pl.BlockSpec((L, LANE), lambda b, i: (0, 0))]
        args += [cos, sin, cos, sin]
        cspec = pl.BlockSpec((nb, None, C_KV_HEADS, C_HEAD, PAST_LEN), lambda b, i: (b, pair, 0, 0, 0))
        in_specs += [cspec, cspec]
        args += [cache_k.transpose(0, 1, 2, 4, 3), cache_v.transpose(0, 1, 2, 4, 3)]
    out_specs = [pl.BlockSpec((nb * LQ, D_MODEL), lambda b, i: (b * nq + i, 0))]
    out_shape = [jax.ShapeDtypeStruct((n_seq * L, D_MODEL), f32)]
    aliases = {}
    if not sample:
        if kv_prev is None:
            nspec = pl.BlockSpec((nb, N_PAIRS, C_KV_HEADS, C_HEAD, L), lambda b, i: (b, 0, 0, 0, 0))
        else:
            nspec = pl.BlockSpec((nb, None, C_KV_HEADS, C_HEAD, L), lambda b, i: (b, pair, 0, 0, 0))
        out_specs += [nspec, nspec]
        out_shape += [jax.ShapeDtypeStruct((n_seq, N_PAIRS, C_KV_HEADS, C_HEAD, L), f32)] * 2
        if kv_prev is not None:
            aliases = {len(args): 1, len(args) + 1: 2}
            in_specs += [pl.BlockSpec(memory_space=pl.ANY)] * 2
            args += list(kv_prev)
    return pl.pallas_call(
        functools.partial(_attn_kernel, LQ=LQ, LN=L, NB=nb, NQ=nq, PAST=past, rope=sample, write_cache=not sample,
                          n_prev=len(aliases), pair=pair),
        grid=(n_seq // nb, nq),
        in_specs=in_specs,
        out_specs=out_specs,
        out_shape=out_shape,
        input_output_aliases=aliases,
        scratch_shapes=[pltpu.VMEM((nb, C_KV_HEADS, L, C_HEAD), bf16),
                        pltpu.VMEM((nb, C_KV_HEADS, L, C_HEAD), bf16),
                        pltpu.VMEM((nb * LQ, D_MODEL), f32)],
        compiler_params=pltpu.CompilerParams(vmem_limit_bytes=VMEM_LIMIT),
        name="gqa_sample" if sample else "gqa_prompt",
    )(*args)


def kernel(x_prompt, x_sample, c, state_rwkv, cache_k, cache_v, c_ctx, w_ada, b_ada, norm_g, final_g,
           w_in_e, mu_shift, lora_w2, w0, lora_a2, a0, k_k, k_a, r_k, lnx_g, lnx_b, conv_w, conv_b,
           w_out_e, w_in_o, q_norm_g, k_norm_g, w_out_o):
    xs = [x_prompt.reshape(STREAM_TOK, D_MODEL), x_sample.reshape(STREAM_TOK, D_MODEL)]
    cond = jnp.concatenate([c_ctx[None, :], c, jnp.zeros((N_MOD - 1 - DEC_BATCH, D_MODEL), f32)], axis=0)
    mod = _adaln(cond, w_ada, b_ada).reshape(DEPTH, N_MOD, 1, 3 * D_MODEL)

    rope = _rope_tables()
    final_g2 = final_g.reshape(1, D_MODEL)

    w_ins, w_outs = [], []
    for layer in range(DEPTH):
        p = layer // 2
        if layer % 2 == 0:
            w_ins.append(_even_in_weight(w_in_e[p]))
            wo = w_out_e[p]
            w_outs.append(jnp.concatenate(
                [wo[h * A_WIDTH + j * LANE:h * A_WIDTH + (j + 1) * LANE] for j in range(N_PAIR) for h in range(2)],
                axis=0).astype(bf16))
        else:
            wi = w_in_o[p]
            w_ins.append(jnp.concatenate([wi[:, :D_MODEL], wi[:, D_MODEL + 2 * KV_WIDTH:],
                                          wi[:, D_MODEL:D_MODEL + 2 * KV_WIDTH]], axis=1).astype(bf16))
            w_outs.append(w_out_o[p].astype(bf16))
    inp = lambda layer: (mod[layer], norm_g[layer].reshape(1, D_MODEL), w_ins[layer])

    new_rwkv = new_kv = None
    (proj_p,) = _proj(xs[0], sample=False, inp=inp(0))
    (proj_s,) = _proj(xs[1], sample=True, inp=inp(0))
    for layer in range(DEPTH):
        p = layer // 2
        if layer % 2 == 0:
            tables = _mixer_tables(p, mu_shift, lora_w2, w0, lora_a2, a0, k_k, k_a, r_k, lnx_g, lnx_b,
                                   conv_w, conv_b)
            act_p, new_rwkv = _mixer(proj_p, tables, None, p, L=SEQ, n_seq=BATCH, write_s=True, s_prev=new_rwkv)
            (act_s,) = _mixer(proj_s, tables, state_rwkv, p, L=DEC_SEQ, n_seq=DEC_BATCH, write_s=False)
        else:
            q_g = jnp.tile(q_norm_g[p], C_Q_HEADS).reshape(1, D_MODEL)
            k_g = jnp.tile(k_norm_g[p], C_KV_HEADS).reshape(1, KV_WIDTH)
            act_p, *new_kv = _attention(proj_p, q_g, k_g, None, None, p, None, n_seq=BATCH, L=SEQ,
                                        sample=False, kv_prev=new_kv)
            (act_s,) = _attention(proj_s, q_g, k_g, cache_k, cache_v, p, rope, n_seq=DEC_BATCH,
                                  L=DEC_SEQ, sample=True)
        if layer < DEPTH - 1:
            xs[0], proj_p = _proj(xs[0], sample=False, out=(act_p, mod[layer], w_outs[layer]), inp=inp(layer + 1))
            xs[1], proj_s = _proj(xs[1], sample=True, out=(act_s, mod[layer], w_outs[layer]), inp=inp(layer + 1))
        else:
            (xs[0],) = _proj(xs[0], sample=False, out=(act_p, mod[layer], w_outs[layer]), final_g=final_g2)
            (xs[1],) = _proj(xs[1], sample=True, out=(act_s, mod[layer], w_outs[layer]), final_g=final_g2)
    return (xs[0].reshape(BATCH, SEQ, D_MODEL), xs[1].reshape(DEC_BATCH, DEC_SEQ, D_MODEL),
            new_rwkv, new_kv[0].transpose(0, 1, 2, 4, 3), new_kv[1].transpose(0, 1, 2, 4, 3))
```

```python
import functools

import numpy as np
import jax
import jax.numpy as jnp
from jax import lax
from jax.experimental import pallas as pl
from jax.experimental.pallas import tpu as pltpu

f32 = jnp.float32
bf16 = jnp.bfloat16
HI = lax.Precision.HIGHEST

D_MODEL = 1024
BATCH = 32
SEQ = 256
DEPTH = 4
DEC_BATCH = 8
DEC_SEQ = 1024
PAST_LEN = 512
GRID_W = 64
EPS = 1e-6
A_WIDTH = 512
A_HEAD = 64
A_HEADS = 8
LORA = 64
GN_EPS = 64e-5
B_WIDTH = 512
C_HEAD = 64
C_Q_HEADS = 16
C_KV_HEADS = 4
C_GROUP = 4
KV_WIDTH = 256
ROPE_THETA = 10000.0
ROPE_HALF = 32
A_SHIFT = 3 * A_WIDTH + 4 * LORA
EVEN_IN = A_SHIFT + A_WIDTH + 4 * B_WIDTH
ODD_IN = 2 * D_MODEL + 2 * KV_WIDTH

STREAM_TOK = BATCH * SEQ
assert STREAM_TOK == DEC_BATCH * DEC_SEQ
PROJ_ROWS = 512
ATTN_ROWS = 256
ATTN_PROMPT_SEQS = 4
LOG2E = 1.4426950408889634
N_MOD = 16
CHUNK = 64
LANE = 128
VMEM_LIMIT = 56 * 1024 * 1024
N_PAIR = A_WIDTH // LANE
PAIR_COLS = 8 * LANE
N_VEC = 16
MIXER_ROWS = 1024
N_PAIRS = DEPTH // 2


def _sigmoid(x):
    return 1.0 / (1.0 + jnp.exp(-x))


def _silu(x):
    return x * _sigmoid(x)


def _dot(a, b, precision=None):
    return jnp.dot(a, b, precision=precision, preferred_element_type=f32)


def _dot_nt(a, b, precision=None):
    return lax.dot_general(a, b, (((1,), (1,)), ((), ())), precision=precision,
                           preferred_element_type=f32)


def _dot_tn(a, b, precision=None):
    return lax.dot_general(a, b, (((0,), (0,)), ((), ())), precision=precision,
                           preferred_element_type=f32)


def _mod_row(sample):
    if sample:
        return lambda i: 1 + i // (DEC_SEQ // PROJ_ROWS)
    return lambda i: 0


def _head_block_diag():
    ri = lax.broadcasted_iota(jnp.int32, (LANE, LANE), 0) >> 6
    ci = lax.broadcasted_iota(jnp.int32, (LANE, LANE), 1) >> 6
    return (ri == ci).astype(bf16)


def _head_sum(x, bd):
    n = x.shape[1] // LANE
    xb = x.astype(bf16)
    cols = [_dot(xb[:, i * LANE:(i + 1) * LANE], bd) for i in range(n)]
    return cols[0] if n == 1 else jnp.concatenate(cols, axis=1)


def _adaln_kernel(cond_ref, w_ref, b_ref, o_ref):
    o_ref[...] = _dot(_silu(cond_ref[...]), w_ref[...], HI) + b_ref[...]


def _adaln(cond, w_ada, b_ada):
    tn = 768
    return pl.pallas_call(
        _adaln_kernel,
        grid=(DEPTH, 3 * D_MODEL // tn),
        in_specs=[
            pl.BlockSpec((N_MOD, D_MODEL), lambda l, n: (0, 0)),
            pl.BlockSpec((None, D_MODEL, tn), lambda l, n: (l, 0, n)),
            pl.BlockSpec((None, 1, tn), lambda l, n: (l, 0, n)),
        ],
        out_specs=pl.BlockSpec((None, N_MOD, tn), lambda l, n: (l, 0, n)),
        out_shape=jax.ShapeDtypeStruct((DEPTH, N_MOD, 3 * D_MODEL), f32),
        compiler_params=pltpu.CompilerParams(vmem_limit_bytes=VMEM_LIMIT),
        name="adaln",
    )(cond, w_ada, b_ada.reshape(DEPTH, 1, 3 * D_MODEL))


def _proj_kernel(*refs, has_out, has_in, final):
    it = iter(refs)
    if has_out:
        a_ref, x_ref, modp_ref, wo_ref = (next(it) for _ in range(4))
        fg_ref = next(it) if final else None
    else:
        x_ref = next(it)
    if has_in:
        modn_ref, g_ref, wi_ref = (next(it) for _ in range(3))
    x = x_ref[...]
    if has_out:
        x = x + modp_ref[:, 2 * D_MODEL:] * _dot(a_ref[...].astype(bf16), wo_ref[...])
        xo_ref = next(it)
        if final:
            xo_ref[...] = x * lax.rsqrt(jnp.mean(x * x, axis=-1, keepdims=True) + EPS) * fg_ref[...]
        else:
            xo_ref[...] = x
    if has_in:
        p_ref = next(it)
        y = x * lax.rsqrt(jnp.mean(x * x, axis=-1, keepdims=True) + EPS) * g_ref[...]
        m = modn_ref[...]
        h = y * (1.0 + m[:, D_MODEL:2 * D_MODEL]) + m[:, :D_MODEL]
        p_ref[...] = _dot(h.astype(bf16), wi_ref[...])


def _proj(x, *, sample, out=None, inp=None, final_g=None):
    mrow = _mod_row(sample)
    tm = PROJ_ROWS
    rows = lambda w: pl.BlockSpec((tm, w), lambda i: (i, 0))
    const = lambda shape: pl.BlockSpec(shape, lambda i: (0,) * len(shape), pipeline_mode=pl.Buffered(1))
    modspec = pl.BlockSpec((None, 1, 3 * D_MODEL), lambda i: (mrow(i), 0, 0))
    in_specs, args, out_specs, out_shape = [], [], [], []
    if out is not None:
        act, mod_prev, w_out = out
        in_specs += [rows(D_MODEL), rows(D_MODEL), modspec, const((D_MODEL, D_MODEL))]
        args += [act, x, mod_prev, w_out]
        if final_g is not None:
            in_specs.append(const((1, D_MODEL)))
            args.append(final_g)
        out_specs.append(rows(D_MODEL))
        out_shape.append(jax.ShapeDtypeStruct((STREAM_TOK, D_MODEL), f32))
    else:
        in_specs.append(rows(D_MODEL))
        args.append(x)
    if inp is not None:
        mod_next, g, w_in = inp
        n = w_in.shape[1]
        in_specs += [modspec, const((1, D_MODEL)), const((D_MODEL, n))]
        args += [mod_next, g, w_in]
        out_specs.append(rows(n))
        out_shape.append(jax.ShapeDtypeStruct((STREAM_TOK, n), f32))
    return pl.pallas_call(
        functools.partial(_proj_kernel, has_out=out is not None, has_in=inp is not None,
                          final=final_g is not None),
        grid=(STREAM_TOK // tm,),
        in_specs=in_specs,
        out_specs=out_specs,
        out_shape=out_shape,
        compiler_params=pltpu.CompilerParams(vmem_limit_bytes=VMEM_LIMIT),
        name="proj_%s%s" % ("o" if out is not None else "", "i" if inp is not None else ""),
    )(*args)


def _even_column_starts():
    bases = [0, A_WIDTH, 2 * A_WIDTH, A_SHIFT, A_SHIFT + A_WIDTH, A_SHIFT + A_WIDTH + B_WIDTH,
             A_SHIFT + A_WIDTH + 2 * B_WIDTH, A_SHIFT + A_WIDTH + 3 * B_WIDTH]
    return [(b + j * LANE, LANE) for j in range(N_PAIR) for b in bases] + [(3 * A_WIDTH, 4 * LORA)]


def _even_column_order():
    return np.concatenate([s + np.arange(w) for s, w in _even_column_starts()])


def _even_in_weight(w):
    return jnp.concatenate([w[:, s:s + n] for s, n in _even_column_starts()], axis=1).astype(bf16)


def _mixer_tables(p, mu_shift, lora_w2, w0, lora_a2, a0, k_k, k_a, r_k, lnx_g, lnx_b, conv_w, conv_b):
    pair = lambda v: v.reshape(-1, N_PAIR, LANE).transpose(1, 0, 2)
    mu = mu_shift[p]
    rows = [k_k[p][None], k_a[p][None], r_k[p].reshape(1, A_WIDTH), lnx_g[p][None], lnx_b[p][None],
            conv_b[p][None], conv_w[p], w0[p], a0[p],
            mu[None, :A_WIDTH], mu[None, A_WIDTH:2 * A_WIDTH], mu[None, 2 * A_WIDTH:3 * A_WIDTH]]
    vecs = pair(jnp.concatenate(rows, axis=0))
    assert vecs.shape == (N_PAIR, N_VEC, LANE)
    mats = pair(jnp.concatenate([lora_w2[p], lora_a2[p]], axis=0).reshape(4 * LORA, A_WIDTH))
    mats = mats.reshape(N_PAIR, 4, LORA, LANE)
    return vecs, mats, mu[None, 3 * A_WIDTH:]


V_KK, V_KA, V_RK, V_LNG, V_LNB, V_CB, V_CW, V_W0, V_A0, V_MU = 0, 1, 2, 3, 4, 5, 6, 9, 11, 13


def _scan_masks():
    ri = lax.broadcasted_iota(jnp.int32, (CHUNK, CHUNK), 0)
    ci = lax.broadcasted_iota(jnp.int32, (CHUNK, CHUNK), 1)
    eye = ri == ci
    ri2 = lax.broadcasted_iota(jnp.int32, (CHUNK, 2 * CHUNK), 0)
    ci2 = lax.broadcasted_iota(jnp.int32, (CHUNK, 2 * CHUNK), 1) & (CHUNK - 1)
    strict2 = (ri2 > ci2, ri2 < ci2)
    incl2 = (ri2 >= ci2, ri2 <= ci2)
    levels = []
    for sh in range(6):
        levels.append(((ri >> (sh + 1)) == (ci >> (sh + 1))) & ((ri >> sh) != (ci >> sh)))
    return eye, strict2, incl2, levels


def _mixer_kernel(*refs, L, NB, has_s0, write_s, has_prev, pair):
    it = iter(refs)
    x_ref, lo_ref, vec_ref, mat_ref, mulo = (next(it) for _ in range(5))
    s0 = next(it) if has_s0 else None
    if has_prev:
        next(it)
    mix = next(it)
    sout = next(it) if write_s else None
    r_s, k_s, v_s, lo_s, kd_s, y_s, rp_s, y0_s, mt_s, nt_s, st_s = it

    j = pl.program_id(1)
    vec = vec_ref[j]
    row = lambda i: vec[i:i + 1, :]
    col = lambda i: x_ref[:, i * LANE:(i + 1) * LANE]
    lora_w = [mat_ref[j, i].astype(bf16) for i in range(4)]

    R = NB * L
    nch, nch_seq = R // CHUNK, L // CHUNK
    rows = lax.broadcasted_iota(jnp.int32, (R, 1), 0)
    first, last = (rows & (L - 1)) == 0, (rows & (L - 1)) == L - 1

    def prev(f):
        return jnp.where(first, 0.0, pltpu.roll(f, 1, 0))

    def nxt(f):
        return jnp.where(last, 0.0, pltpu.roll(f, R - 1, 0))

    def tshift(f, mu):
        return f + mu * (0.5 * (prev(f) + nxt(f)) - f)

    bd = _head_block_diag()

    r_s[...] = tshift(col(0), row(V_MU))
    k_s[...] = tshift(col(1), row(V_MU + 1))
    v_s[...] = tshift(col(2), row(V_MU + 2))
    lo_s[...] = tshift(lo_ref[...], mulo[...])
    for q in range(NB):
        for d in range(2):
            st_s[q, d] = jnp.zeros((LANE, LANE), f32)
            if has_s0:
                for hh in range(2):
                    st_s[q, d, hh * A_HEAD:(hh + 1) * A_HEAD, hh * A_HEAD:(hh + 1) * A_HEAD] = s0[q, d, hh]

    eye, strict2, incl2, levels = _scan_masks()
    eye_f = eye.astype(f32)
    lane = lax.broadcasted_iota(jnp.int32, (1, LANE), 1)
    half = (lane < A_HEAD, lane >= A_HEAD)
    ri = lax.broadcasted_iota(jnp.int32, (LANE, LANE), 0)
    ci = lax.broadcasted_iota(jnp.int32, (LANE, LANE), 1)
    eye_pair = ri == ci
    crow = lax.broadcasted_iota(jnp.int32, (CHUNK, 1), 0)

    def setup_dots(c):
        rs = slice(c * CHUNK, (c + 1) * CHUNK)
        k2, lo = k_s[rs, :], lo_s[rs, :]
        kk = k2 * row(V_KK)
        st = dict(c=c, rs=rs, k2=k2, kk=kk, ssq=_head_sum(kk * kk, bd), wl=[], al=[])
        for d in range(2):
            lwd = lo[:, d * LORA:(d + 1) * LORA]
            lad = lo[:, 2 * LORA + d * LORA:2 * LORA + (d + 1) * LORA]
            st["wl"].append(_dot(jnp.tanh(lwd).astype(bf16), lora_w[d]))
            st["al"].append(_dot(lad.astype(bf16), lora_w[2 + d]))
        return st

    def setup_chains(st):
        c, rs, k2 = st["c"], st["rs"], st["k2"]
        r2, v2 = r_s[rs, :], v_s[rs, :]
        kap2 = st["kk"] / jnp.maximum(jnp.sqrt(st["ssq"]), 1e-12)
        groups = []
        for d in range(2):
            wl = row(V_W0 + d) + st["wl"][d]
            softplus = jnp.maximum(-wl, 0.0) + jnp.log(1.0 + jnp.exp(-jnp.abs(wl)))
            lw = -jnp.exp(-softplus - 0.5)
            g = lw
            for sh in range(6):
                stp = 1 << sh
                if d == 0:
                    g = g + jnp.where(crow >= stp, pltpu.roll(g, stp, 0), 0.0)
                else:
                    g = g + jnp.where(crow < CHUNK - stp, pltpu.roll(g, CHUNK - stp, 0), 0.0)
            a = _sigmoid(row(V_A0 + d) + st["al"][d])
            kd2 = k2 * (1.0 + (a - 1.0) * row(V_KA))
            kd_s[d, rs, :] = kd2
            b2 = a * kap2
            tot = g[CHUNK - 1:CHUNK, :] if d == 0 else g[0:1, :]
            e_ng = jnp.exp(-g)
            e_rem = jnp.exp(tot - g)
            kt2 = kap2 * jnp.exp(g - lw)
            rt2 = r2 * jnp.exp(g)
            lhs2 = jnp.concatenate([kt2, rt2], axis=0).astype(bf16)
            grp = dict(
                d=d, c=c, rs=rs, rt2=rt2, v2=v2, v2b=v2.astype(bf16), e_tot=jnp.exp(tot),
                kt2sw=pltpu.roll(kt2, A_HEAD, 1),
                rhs2=jnp.concatenate([kd2 * e_ng, b2 * e_ng], axis=0).astype(bf16),
                kbbb2=jnp.concatenate([kd2 * e_rem, b2 * e_rem], axis=0).astype(bf16), chains=[])
            for hh in range(2):
                ch = dict(grp=grp, d=d, hh=hh)
                lhs = jnp.where(half[hh], lhs2, jnp.zeros_like(lhs2))
                ch["a"] = _dot_nt(lhs, grp["rhs2"])
                grp["chains"].append(ch)
            groups.append(grp)
        return groups

    def chain_stages(groups):
        chains = [ch for grp in groups for ch in grp["chains"]]
        for ch in chains:
            top = jnp.where(strict2[ch["d"]], ch["a"][:CHUNK], 0.0)
            ch["bot"] = jnp.where(incl2[ch["d"]], ch["a"][CHUNK:], 0.0).astype(bf16)
            ch["akk"] = top[:, :CHUNK].astype(bf16)
            akb = top[:, CHUNK:]
            ch["akb"] = akb
            ch["ab"] = akb.astype(bf16)
            ch["t"] = eye_f - jnp.where(levels[0], akb, 0.0)
        yield
        for sh in range(1, 6):
            m, blk = levels[sh], 1 << sh
            if blk < 8:
                for ch in chains:
                    ch["tb"] = ch["t"].astype(bf16)
                    off = jnp.where(m, ch["ab"], jnp.zeros_like(ch["ab"]))
                    ch["p"] = _dot(off, ch["tb"]).astype(bf16)
                yield
                for ch in chains:
                    ch["t"] = ch["t"] - _dot(ch["tb"], ch["p"])
                yield
                continue
            for ch in chains:
                ch["upd"] = [k for k in range(CHUNK // blk) if k % 2 == 1 - ch["d"]]
                take = lambda x: jnp.concatenate([x[k * blk:(k + 1) * blk] for k in ch["upd"]], axis=0)
                ch["tb"] = ch["t"].astype(bf16)
                ch["trow"] = take(ch["t"])
                ch["pc"] = _dot(take(jnp.where(m, ch["akb"], 0.0)).astype(bf16), ch["tb"])
            yield
            for ch in chains:
                zero = jnp.zeros((blk, CHUNK), f32)
                pcs = {k: ch["pc"][i * blk:(i + 1) * blk] for i, k in enumerate(ch["upd"])}
                pfull = jnp.concatenate([pcs.get(k, zero) for k in range(CHUNK // blk)], axis=0).astype(bf16)
                new = ch["trow"] - _dot(ch["trow"].astype(bf16), pfull)
                rows = {k: new[i * blk:(i + 1) * blk] for i, k in enumerate(ch["upd"])}
                ch["t"] = jnp.concatenate(
                    [rows[k] if k in rows else ch["t"][k * blk:(k + 1) * blk] for k in range(CHUNK // blk)], axis=0)
            yield
        for ch in chains:
            ch["w0v"] = _dot(ch["akk"], ch["grp"]["v2b"])
        yield
        for ch in chains:
            rhs = jnp.where(half[ch["hh"]], ch["w0v"], ch["grp"]["kt2sw"])
            ch["x"] = _dot(ch["t"].astype(bf16), rhs.astype(bf16))
        yield
        for ch in chains:
            vtop = jnp.where(half[ch["hh"]], ch["grp"]["v2"], 0.0)
            ch["z"] = jnp.concatenate([vtop, -ch["x"]], axis=0).astype(bf16)
            ch["o"] = _dot(ch["bot"], ch["z"])
        yield
        for ch in chains:
            ch["tn"] = _dot_tn(ch["z"], ch["grp"]["kbbb2"])
        yield
        for grp in groups:
            d, rs = grp["d"], grp["rs"]
            c0, c1 = grp["chains"]
            y0_s[d, rs, :] = jnp.where(half[0], c0["o"], c1["o"])
            rp_s[d, rs, :] = grp["rt2"] + pltpu.roll(jnp.where(half[0], c1["o"], c0["o"]), A_HEAD, 1)
            ps = slice(grp["c"] * LANE, (grp["c"] + 1) * LANE)
            nt_s[d, ps, :] = jnp.concatenate([jnp.where(half[0], c0["tn"][:A_HEAD], 0.0),
                                              jnp.where(half[1], c1["tn"][A_HEAD:], 0.0)], axis=0)
            mt_s[d, ps, :] = jnp.where(eye_pair, grp["e_tot"], 0.0) + jnp.concatenate(
                [jnp.where(half[0], c0["tn"][A_HEAD:], 0.0), jnp.where(half[1], c1["tn"][:A_HEAD], 0.0)],
                axis=0)

    def seq_step(seqs, i):
        steps = []
        for q in seqs:
            for d in range(2):
                c = q * nch_seq + (i if d == 0 else nch_seq - 1 - i)
                steps.append((q, d, slice(c * CHUNK, (c + 1) * CHUNK), slice(c * LANE, (c + 1) * LANE),
                              st_s[q, d].astype(bf16)))
        for q, d, rs, ps, sb in steps:
            st_s[q, d] = _dot(sb, mt_s[d, ps, :].astype(bf16)) + nt_s[d, ps, :]
        for q, d, rs, ps, sb in steps:
            y_s[d, rs, :] = y0_s[d, rs, :] + _dot_nt(rp_s[d, rs, :].astype(bf16), sb)

    def finish_a(c, fin):
        rs = slice(c * CHUNK, (c + 1) * CHUNK)
        y = y_s[0, rs, :] + y_s[1, rs, :]
        kmean = 0.5 * (kd_s[0, rs, :] + kd_s[1, rs, :])
        fin[c] = dict(rs=rs, y=y, ysum=_head_sum(y, bd), bsum=_head_sum(r_s[rs, :] * kmean * row(V_RK), bd))

    def finish_b(c, fin):
        f = fin[c]
        f["yc"] = f["y"] - f["ysum"] * (1.0 / A_HEAD)
        f["vsum"] = _head_sum(f["yc"] * f["yc"], bd)

    def finish_c(c, fin):
        f = fin[c]
        rs = f["rs"]
        yn = f["yc"] * lax.rsqrt(f["vsum"] * (1.0 / A_HEAD) + GN_EPS) * row(V_LNG) + row(V_LNB)
        mix[rs, :LANE] = (yn + f["bsum"] * v_s[rs, :]) * _silu(x_ref[rs, 3 * LANE:4 * LANE])

    def finish_fillers(chunks, fin):
        chunks = list(chunks)
        out = []
        for t in range(len(chunks) + 2):
            slot = []
            if t < len(chunks):
                slot.append(functools.partial(finish_a, chunks[t], fin))
            if 0 <= t - 1 < len(chunks):
                slot.append(functools.partial(finish_b, chunks[t - 1], fin))
            if 0 <= t - 2 < len(chunks):
                slot.append(functools.partial(finish_c, chunks[t - 2], fin))
            out.append(slot)
        return out

    def conv_branch():
        u = col(5) * col(6)
        conv = prev(u) * row(V_CW) + u * row(V_CW + 1) + nxt(u) * row(V_CW + 2) + row(V_CB)
        mix[:, LANE:] = col(4) * conv * _silu(col(7))

    def run(stages, slots):
        slots = list(slots)
        for _ in stages:
            if slots:
                for f in slots.pop(0):
                    f()
        for slot in slots:
            for f in slot:
                f()

    hc = nch // 2
    dots = [setup_dots(c) for c in range(hc)]
    first_half = [grp for st in dots for grp in setup_chains(st)]
    second_half, pending = [], {}
    slots = []
    for t in range(hc + 1):
        slot = []
        if t < hc:
            slot.append(functools.partial(lambda c: pending.__setitem__(c, setup_dots(c)), hc + t))
        if t >= 1:
            slot.append(functools.partial(lambda c: second_half.extend(setup_chains(pending[c])), hc + t - 1))
        slots.append(slot)
    run(chain_stages(first_half), slots)
    fin = {}
    if NB >= 2:
        early = list(range(NB // 2))
        late = list(range(NB // 2, NB))
        slots = [[conv_branch]] + [[functools.partial(seq_step, early, i)] for i in range(nch_seq)]
        slots += finish_fillers(range(hc), fin)
        run(chain_stages(second_half), slots)
        for i in range(nch_seq):
            seq_step(late, i)
        run(iter(()), finish_fillers(range(hc, nch), fin))
    else:
        run(chain_stages(second_half), [[conv_branch]])
        ready = []
        for i in range(nch_seq):
            seq_step([0], i)
            if i >= nch_seq // 2:
                ready += [i, nch_seq - 1 - i]
            for c in list(ready):
                stage = fin.get(c, {}).get("stage", 0)
                (finish_a, finish_b, finish_c)[stage](c, fin)
                fin[c]["stage"] = stage + 1
                if stage == 2:
                    ready.remove(c)
        while ready:
            for c in list(ready):
                stage = fin[c]["stage"]
                (finish_a, finish_b, finish_c)[stage](c, fin)
                fin[c]["stage"] = stage + 1
                if stage == 2:
                    ready.remove(c)

    if write_s:
        if not has_prev:
            for pp in range(N_PAIRS):
                if pp != pair:
                    sout[:, pp] = jnp.zeros((NB, 2, 2, A_HEAD, A_HEAD), f32)
        own = sout if has_prev else sout.at[:, pair]
        for q in range(NB):
            for d in range(2):
                for hh in range(2):
                    own[q, d, hh] = st_s[q, d, hh * A_HEAD:(hh + 1) * A_HEAD, hh * A_HEAD:(hh + 1) * A_HEAD]


def _mixer(proj, tables, s0, pair, *, L, n_seq, write_s, s_prev=None):
    vecs, mats, mulo = tables
    has_s0 = s0 is not None
    R = MIXER_ROWS
    nb = R // L
    in_specs = [
        pl.BlockSpec((R, PAIR_COLS), lambda s, j: (s, j)),
        pl.BlockSpec((R, 4 * LORA), lambda s, j: (s, N_PAIR * PAIR_COLS // (4 * LORA))),
        pl.BlockSpec((N_PAIR, N_VEC, LANE), lambda s, j: (0, 0, 0)),
        pl.BlockSpec((N_PAIR, 4, LORA, LANE), lambda s, j: (0, 0, 0, 0)),
        pl.BlockSpec((1, 4 * LORA), lambda s, j: (0, 0)),
    ]
    args = [proj, proj, vecs, mats, mulo]
    if has_s0:
        in_specs.append(pl.BlockSpec((nb, None, 2, 2, A_HEAD, A_HEAD), lambda s, j: (s, pair, 0, j, 0, 0)))
        args.append(s0)
    out_specs = [pl.BlockSpec((R, 2 * LANE), lambda s, j: (s, j))]
    out_shape = [jax.ShapeDtypeStruct((n_seq * L, D_MODEL), f32)]
    aliases = {}
    if write_s:
        out_shape.append(jax.ShapeDtypeStruct((n_seq, N_PAIRS, 2, A_HEADS, A_HEAD, A_HEAD), f32))
        if s_prev is None:
            out_specs.append(pl.BlockSpec((nb, N_PAIRS, 2, 2, A_HEAD, A_HEAD), lambda s, j: (s, 0, 0, j, 0, 0)))
        else:
            out_specs.append(pl.BlockSpec((nb, None, 2, 2, A_HEAD, A_HEAD), lambda s, j: (s, pair, 0, j, 0, 0)))
            aliases = {len(args): 1}
            in_specs.append(pl.BlockSpec(memory_space=pl.ANY))
            args.append(s_prev)
    tok = lambda: pltpu.VMEM((R, LANE), f32)
    tok2 = lambda: pltpu.VMEM((2, R, LANE), f32)
    maps = lambda: pltpu.VMEM((2, 2 * R, LANE), f32)
    scratch = [tok(), tok(), tok(), pltpu.VMEM((R, 4 * LORA), f32), tok2(), tok2(), tok2(), tok2(),
               maps(), maps(), pltpu.VMEM((nb, 2, LANE, LANE), f32)]
    return pl.pallas_call(
        functools.partial(_mixer_kernel, L=L, NB=nb, has_s0=has_s0, write_s=write_s, has_prev=bool(aliases),
                          pair=pair),
        grid=(n_seq // nb, N_PAIR),
        in_specs=in_specs,
        out_specs=out_specs,
        out_shape=out_shape,
        scratch_shapes=scratch,
        input_output_aliases=aliases,
        compiler_params=pltpu.CompilerParams(vmem_limit_bytes=VMEM_LIMIT),
        name="rwkv_conv_mixer_L%d" % L,
    )(*args)


def _attn_kernel(*refs, LQ, LN, NB, NQ, PAST, rope, write_cache, n_prev, pair):
    it = iter(refs)
    q_ref, k_ref, v_ref, g_ref, qg_ref, kg_ref = (next(it) for _ in range(6))
    if rope:
        cosq, sinq, cosk, sink = (next(it) for _ in range(4))
    if PAST:
        ck, cv = next(it), next(it)
    for _ in range(n_prev):
        next(it)
    out_ref = next(it)
    if write_cache:
        nk_ref, nv_ref = next(it), next(it)
    kbuf, vbuf, obuf = it

    bd = _head_block_diag()
    lane = lax.broadcasted_iota(jnp.int32, (1, LANE), 1)
    first_half = (lane & 31) < 16

    def rms(x, g):
        return x * lax.rsqrt(_head_sum(x * x, bd) * (1.0 / C_HEAD) + EPS) * g

    def rotate(x, cos, sin):
        outs = []
        for i in range(x.shape[1] // LANE):
            xb = x[:, i * LANE:(i + 1) * LANE]
            sw = jnp.where(first_half, pltpu.roll(xb, LANE - 16, 1), pltpu.roll(xb, 16, 1))
            outs.append(xb * cos + sw * sin)
        return jnp.concatenate(outs, axis=1)

    def build_keys():
        kn = rms(k_ref[...], kg_ref[...])
        vx = v_ref[...]
        if write_cache:
            nk_own, nv_own = nk_ref, nv_ref
            if not n_prev:
                for pp in range(N_PAIRS):
                    if pp != pair:
                        nk_ref[:, pp] = jnp.zeros((NB, C_KV_HEADS, C_HEAD, LN), f32)
                        nv_ref[:, pp] = jnp.zeros((NB, C_KV_HEADS, C_HEAD, LN), f32)
                nk_own, nv_own = nk_ref.at[:, pair], nv_ref.at[:, pair]
            for b in range(NB):
                knt = kn[b * LN:(b + 1) * LN, :].T
                vxt = vx[b * LN:(b + 1) * LN, :].T
                for h in range(C_KV_HEADS):
                    nk_own[b, h] = knt[h * C_HEAD:(h + 1) * C_HEAD, :]
                    nv_own[b, h] = vxt[h * C_HEAD:(h + 1) * C_HEAD, :]
        if rope:
            kn = rotate(kn, cosk[...], sink[...])
        for b in range(NB):
            for h in range(C_KV_HEADS):
                kbuf[b, h] = kn[b * LN:(b + 1) * LN, h * C_HEAD:(h + 1) * C_HEAD].astype(bf16)
                vbuf[b, h] = vx[b * LN:(b + 1) * LN, h * C_HEAD:(h + 1) * C_HEAD].astype(bf16)

    if NQ == 1:
        build_keys()
    else:
        pl.when(pl.program_id(1) == 0)(build_keys)

    qn = rms(q_ref[...], qg_ref[...])
    if rope:
        qn = rotate(qn, cosq[...], sinq[...])
    qn = (qn * (C_HEAD ** -0.5 * LOG2E)).astype(bf16)
    units = [(b, h) for b in range(NB) for h in range(C_KV_HEADS)]
    heads = [[slice((h * C_GROUP + g) * C_HEAD, (h * C_GROUP + g + 1) * C_HEAD) for g in range(C_GROUP)]
             for h in range(C_KV_HEADS)]

    def scores(b, h):
        q4 = jnp.concatenate([qn[b * LQ:(b + 1) * LQ, sl] for sl in heads[h]], axis=0)
        s_new = _dot_nt(q4, kbuf[b, h])
        if not PAST:
            return s_new
        return jnp.concatenate([_dot(q4, ck[b, h].astype(bf16)), s_new], axis=1)

    s_next = scores(*units[0])
    for n, (b, h) in enumerate(units):
        s = s_next
        if n + 1 < len(units):
            s_next = scores(*units[n + 1])
        p = jnp.exp2(s - jnp.max(s, axis=-1, keepdims=True))
        pb = p.astype(bf16)
        o = _dot(pb[:, PAST:], vbuf[b, h])
        if PAST:
            o = o + _dot_nt(pb[:, :PAST], cv[b, h].astype(bf16))
        o = o / jnp.sum(p, axis=-1, keepdims=True)
        for g, sl in enumerate(heads[h]):
            obuf[b * LQ:(b + 1) * LQ, sl] = o[g * LQ:(g + 1) * LQ]
    out_ref[...] = obuf[...] * _silu(g_ref[...])


def _rope_tables():
    pos = np.arange(DEC_SEQ)
    inv = ROPE_THETA ** (-np.arange(0, ROPE_HALF, 2, dtype=np.float32) / ROPE_HALF)
    ang_r = (pos // GRID_W).astype(np.float32)[:, None] * inv
    ang_c = (pos % GRID_W).astype(np.float32)[:, None] * inv
    cos64 = np.concatenate([np.cos(ang_r), np.cos(ang_r), np.cos(ang_c), np.cos(ang_c)], axis=1)
    sin64 = np.concatenate([-np.sin(ang_r), np.sin(ang_r), -np.sin(ang_c), np.sin(ang_c)], axis=1)
    return jnp.asarray(np.tile(cos64, (1, 2)), f32), jnp.asarray(np.tile(sin64, (1, 2)), f32)


def _attention(proj, q_g, k_g, cache_k, cache_v, pair, tables, *, n_seq, L, sample, kv_prev=None):
    LQ = ATTN_ROWS
    nq = L // LQ
    nb = 1 if sample else ATTN_PROMPT_SEQS
    assert nb == 1 or nq == 1
    past = PAST_LEN if sample else 0
    in_specs = [
        pl.BlockSpec((nb * LQ, D_MODEL), lambda b, i: (b * nq + i, 0)),
        pl.BlockSpec((nb * L, KV_WIDTH), lambda b, i: (b, 2 * D_MODEL // KV_WIDTH)),
        pl.BlockSpec((nb * L, KV_WIDTH), lambda b, i: (b, 2 * D_MODEL // KV_WIDTH + 1)),
        pl.BlockSpec((nb * LQ, D_MODEL), lambda b, i: (b * nq + i, 1)),
        pl.BlockSpec((1, D_MODEL), lambda b, i: (0, 0)),
        pl.BlockSpec((1, KV_WIDTH), lambda b, i: (0, 0)),
    ]
    args = [proj, proj, proj, proj, q_g, k_g]
    if sample:
        cos, sin = tables
        in_specs += [pl.BlockSpec((LQ, LANE), lambda b, i: (i, 0)),
                     pl.BlockSpec((LQ, LANE), lambda b, i: (i, 0)),
                     pl.BlockSpec((L, LANE), lambda b, i: (0, 0)),
                     pl.BlockSpec((L, LANE), lambda b, i: (0, 0))]
        args += [cos, sin, cos, sin]
        cspec = pl.BlockSpec((nb, None, C_KV_HEADS, C_HEAD, PAST_LEN), lambda b, i: (b, pair, 0, 0, 0))
        in_specs += [cspec, cspec]
        args += [cache_k.transpose(0, 1, 2, 4, 3), cache_v.transpose(0, 1, 2, 4, 3)]
    out_specs = [pl.BlockSpec((nb * LQ, D_MODEL), lambda b, i: (b * nq + i, 0))]
    out_shape = [jax.ShapeDtypeStruct((n_seq * L, D_MODEL), f32)]
    aliases = {}
    if not sample:
        if kv_prev is None:
            nspec = pl.BlockSpec((nb, N_PAIRS, C_KV_HEADS, C_HEAD, L), lambda b, i: (b, 0, 0, 0, 0))
        else:
            nspec = pl.BlockSpec((nb, None, C_KV_HEADS, C_HEAD, L), lambda b, i: (b, pair, 0, 0, 0))
        out_specs += [nspec, nspec]
        out_shape += [jax.ShapeDtypeStruct((n_seq, N_PAIRS, C_KV_HEADS, C_HEAD, L), f32)] * 2
        if kv_prev is not None:
            aliases = {len(args): 1, len(args) + 1: 2}
            in_specs += [pl.BlockSpec(memory_space=pl.ANY)] * 2
            args += list(kv_prev)
    return pl.pallas_call(
        functools.partial(_attn_kernel, LQ=LQ, LN=L, NB=nb, NQ=nq, PAST=past, rope=sample, write_cache=not sample,
                          n_prev=len(aliases), pair=pair),
        grid=(n_seq // nb, nq),
        in_specs=in_specs,
        out_specs=out_specs,
        out_shape=out_shape,
        input_output_aliases=aliases,
        scratch_shapes=[pltpu.VMEM((nb, C_KV_HEADS, L, C_HEAD), bf16),
                        pltpu.VMEM((nb, C_KV_HEADS, L, C_HEAD), bf16),
                        pltpu.VMEM((nb * LQ, D_MODEL), f32)],
        compiler_params=pltpu.CompilerParams(vmem_limit_bytes=VMEM_LIMIT),
        name="gqa_sample" if sample else "gqa_prompt",
    )(*args)


def kernel(x_prompt, x_sample, c, state_rwkv, cache_k, cache_v, c_ctx, w_ada, b_ada, norm_g, final_g,
           w_in_e, mu_shift, lora_w2, w0, lora_a2, a0, k_k, k_a, r_k, lnx_g, lnx_b, conv_w, conv_b,
           w_out_e, w_in_o, q_norm_g, k_norm_g, w_out_o):
    xs = [x_prompt.reshape(STREAM_TOK, D_MODEL), x_sample.reshape(STREAM_TOK, D_MODEL)]
    cond = jnp.concatenate([c_ctx[None, :], c, jnp.zeros((N_MOD - 1 - DEC_BATCH, D_MODEL), f32)], axis=0)
    mod = _adaln(cond, w_ada, b_ada).reshape(DEPTH, N_MOD, 1, 3 * D_MODEL)

    rope = _rope_tables()
    final_g2 = final_g.reshape(1, D_MODEL)

    w_ins, w_outs = [], []
    for layer in range(DEPTH):
        p = layer // 2
        if layer % 2 == 0:
            w_ins.append(_even_in_weight(w_in_e[p]))
            wo = w_out_e[p]
            w_outs.append(jnp.concatenate(
                [wo[h * A_WIDTH + j * LANE:h * A_WIDTH + (j + 1) * LANE] for j in range(N_PAIR) for h in range(2)],
                axis=0).astype(bf16))
        else:
            wi = w_in_o[p]
            w_ins.append(jnp.concatenate([wi[:, :D_MODEL], wi[:, D_MODEL + 2 * KV_WIDTH:],
                                          wi[:, D_MODEL:D_MODEL + 2 * KV_WIDTH]], axis=1).astype(bf16))
            w_outs.append(w_out_o[p].astype(bf16))
    inp = lambda layer: (mod[layer], norm_g[layer].reshape(1, D_MODEL), w_ins[layer])

    new_rwkv = new_kv = None
    (proj_p,) = _proj(xs[0], sample=False, inp=inp(0))
    (proj_s,) = _proj(xs[1], sample=True, inp=inp(0))
    for layer in range(DEPTH):
        p = layer // 2
        if layer % 2 == 0:
            tables = _mixer_tables(p, mu_shift, lora_w2, w0, lora_a2, a0, k_k, k_a, r_k, lnx_g, lnx_b,
                                   conv_w, conv_b)
            act_p, new_rwkv = _mixer(proj_p, tables, None, p, L=SEQ, n_seq=BATCH, write_s=True, s_prev=new_rwkv)
            (act_s,) = _mixer(proj_s, tables, state_rwkv, p, L=DEC_SEQ, n_seq=DEC_BATCH, write_s=False)
        else:
            q_g = jnp.tile(q_norm_g[p], C_Q_HEADS).reshape(1, D_MODEL)
            k_g = jnp.tile(k_norm_g[p], C_KV_HEADS).reshape(1, KV_WIDTH)
            act_p, *new_kv = _attention(proj_p, q_g, k_g, None, None, p, None, n_seq=BATCH, L=SEQ,
                                        sample=False, kv_prev=new_kv)
            (act_s,) = _attention(proj_s, q_g, k_g, cache_k, cache_v, p, rope, n_seq=DEC_BATCH,
                                  L=DEC_SEQ, sample=True)
        if layer < DEPTH - 1:
            xs[0], proj_p = _proj(xs[0], sample=False, out=(act_p, mod[layer], w_outs[layer]), inp=inp(layer + 1))
            xs[1], proj_s = _proj(xs[1], sample=True, out=(act_s, mod[layer], w_outs[layer]), inp=inp(layer + 1))
        else:
            (xs[0],) = _proj(xs[0], sample=False, out=(act_p, mod[layer], w_outs[layer]), final_g=final_g2)
            (xs[1],) = _proj(xs[1], sample=True, out=(act_s, mod[layer], w_outs[layer]), final_g=final_g2)
    return (xs[0].reshape(BATCH, SEQ, D_MODEL), xs[1].reshape(DEC_BATCH, DEC_SEQ, D_MODEL),
            new_rwkv, new_kv[0].transpose(0, 1, 2, 4, 3), new_kv[1].transpose(0, 1, 2, 4, 3))
```

```python
import functools

import numpy as np
import jax
import jax.numpy as jnp
from jax import lax
from jax.experimental import pallas as pl
from jax.experimental.pallas import tpu as pltpu

f32 = jnp.float32
bf16 = jnp.bfloat16
HI = lax.Precision.HIGHEST

D_MODEL = 1024
BATCH = 32
SEQ = 256
DEPTH = 4
DEC_BATCH = 8
DEC_SEQ = 1024
PAST_LEN = 512
GRID_W = 64
EPS = 1e-6
A_WIDTH = 512
A_HEAD = 64
A_HEADS = 8
LORA = 64
GN_EPS = 64e-5
B_WIDTH = 512
C_HEAD = 64
C_Q_HEADS = 16
C_KV_HEADS = 4
C_GROUP = 4
KV_WIDTH = 256
ROPE_THETA = 10000.0
ROPE_HALF = 32
A_SHIFT = 3 * A_WIDTH + 4 * LORA
EVEN_IN = A_SHIFT + A_WIDTH + 4 * B_WIDTH
ODD_IN = 2 * D_MODEL + 2 * KV_WIDTH

STREAM_TOK = BATCH * SEQ
assert STREAM_TOK == DEC_BATCH * DEC_SEQ
PROJ_ROWS = 512
ATTN_ROWS = 256
ATTN_PROMPT_SEQS = 4
LOG2E = 1.4426950408889634
N_MOD = 16
CHUNK = 64
LANE = 128
VMEM_LIMIT = 56 * 1024 * 1024
N_PAIR = A_WIDTH // LANE
PAIR_COLS = 8 * LANE
N_VEC = 16
MIXER_ROWS = 1024
N_PAIRS = DEPTH // 2


def _sigmoid(x):
    return 1.0 / (1.0 + jnp.exp(-x))


def _silu(x):
    return x * _sigmoid(x)


def _dot(a, b, precision=None):
    return jnp.dot(a, b, precision=precision, preferred_element_type=f32)


def _dot_nt(a, b, precision=None):
    return lax.dot_general(a, b, (((1,), (1,)), ((), ())), precision=precision,
                           preferred_element_type=f32)


def _dot_tn(a, b, precision=None):
    return lax.dot_general(a, b, (((0,), (0,)), ((), ())), precision=precision,
                           preferred_element_type=f32)


def _mod_row(sample):
    if sample:
        return lambda i: 1 + i // (DEC_SEQ // PROJ_ROWS)
    return lambda i: 0


def _head_block_diag():
    ri = lax.broadcasted_iota(jnp.int32, (LANE, LANE), 0) >> 6
    ci = lax.broadcasted_iota(jnp.int32, (LANE, LANE), 1) >> 6
    return (ri == ci).astype(bf16)


def _head_sum(x, bd):
    n = x.shape[1] // LANE
    xb = x.astype(bf16)
    cols = [_dot(xb[:, i * LANE:(i + 1) * LANE], bd) for i in range(n)]
    return cols[0] if n == 1 else jnp.concatenate(cols, axis=1)


def _adaln_kernel(cond_ref, w_ref, b_ref, o_ref):
    o_ref[...] = _dot(_silu(cond_ref[...]), w_ref[...], HI) + b_ref[...]


def _adaln(cond, w_ada, b_ada):
    tn = 768
    return pl.pallas_call(
        _adaln_kernel,
        grid=(DEPTH, 3 * D_MODEL // tn),
        in_specs=[
            pl.BlockSpec((N_MOD, D_MODEL), lambda l, n: (0, 0)),
            pl.BlockSpec((None, D_MODEL, tn), lambda l, n: (l, 0, n)),
            pl.BlockSpec((None, 1, tn), lambda l, n: (l, 0, n)),
        ],
        out_specs=pl.BlockSpec((None, N_MOD, tn), lambda l, n: (l, 0, n)),
        out_shape=jax.ShapeDtypeStruct((DEPTH, N_MOD, 3 * D_MODEL), f32),
        compiler_params=pltpu.CompilerParams(vmem_limit_bytes=VMEM_LIMIT),
        name="adaln",
    )(cond, w_ada, b_ada.reshape(DEPTH, 1, 3 * D_MODEL))


def _proj_kernel(*refs, has_out, has_in, final):
    it = iter(refs)
    if has_out:
        a_ref, x_ref, modp_ref, wo_ref = (next(it) for _ in range(4))
        fg_ref = next(it) if final else None
    else:
        x_ref = next(it)
    if has_in:
        modn_ref, g_ref, wi_ref = (next(it) for _ in range(3))
    x = x_ref[...]
    if has_out:
        x = x + modp_ref[:, 2 * D_MODEL:] * _dot(a_ref[...].astype(bf16), wo_ref[...])
        xo_ref = next(it)
        if final:
            xo_ref[...] = x * lax.rsqrt(jnp.mean(x * x, axis=-1, keepdims=True) + EPS) * fg_ref[...]
        else:
            xo_ref[...] = x
    if has_in:
        p_ref = next(it)
        y = x * lax.rsqrt(jnp.mean(x * x, axis=-1, keepdims=True) + EPS) * g_ref[...]
        m = modn_ref[...]
        h = y * (1.0 + m[:, D_MODEL:2 * D_MODEL]) + m[:, :D_MODEL]
        p_ref[...] = _dot(h.astype(bf16), wi_ref[...])


def _proj(x, *, sample, out=None, inp=None, final_g=None):
    mrow = _mod_row(sample)
    tm = PROJ_ROWS
    rows = lambda w: pl.BlockSpec((tm, w), lambda i: (i, 0))
    const = lambda shape: pl.BlockSpec(shape, lambda i: (0,) * len(shape), pipeline_mode=pl.Buffered(1))
    modspec = pl.BlockSpec((None, 1, 3 * D_MODEL), lambda i: (mrow(i), 0, 0))
    in_specs, args, out_specs, out_shape = [], [], [], []
    if out is not None:
        act, mod_prev, w_out = out
        in_specs += [rows(D_MODEL), rows(D_MODEL), modspec, const((D_MODEL, D_MODEL))]
        args += [act, x, mod_prev, w_out]
        if final_g is not None:
            in_specs.append(const((1, D_MODEL)))
            args.append(final_g)
        out_specs.append(rows(D_MODEL))
        out_shape.append(jax.ShapeDtypeStruct((STREAM_TOK, D_MODEL), f32))
    else:
        in_specs.append(rows(D_MODEL))
        args.append(x)
    if inp is not None:
        mod_next, g, w_in = inp
        n = w_in.shape[1]
        in_specs += [modspec, const((1, D_MODEL)), const((D_MODEL, n))]
        args += [mod_next, g, w_in]
        out_specs.append(rows(n))
        out_shape.append(jax.ShapeDtypeStruct((STREAM_TOK, n), f32))
    return pl.pallas_call(
        functools.partial(_proj_kernel, has_out=out is not None, has_in=inp is not None,
                          final=final_g is not None),
        grid=(STREAM_TOK // tm,),
        in_specs=in_specs,
        out_specs=out_specs,
        out_shape=out_shape,
        compiler_params=pltpu.CompilerParams(vmem_limit_bytes=VMEM_LIMIT),
        name="proj_%s%s" % ("o" if out is not None else "", "i" if inp is not None else ""),
    )(*args)


def _even_column_starts():
    bases = [0, A_WIDTH, 2 * A_WIDTH, A_SHIFT, A_SHIFT + A_WIDTH, A_SHIFT + A_WIDTH + B_WIDTH,
             A_SHIFT + A_WIDTH + 2 * B_WIDTH, A_SHIFT + A_WIDTH + 3 * B_WIDTH]
    return [(b + j * LANE, LANE) for j in range(N_PAIR) for b in bases] + [(3 * A_WIDTH, 4 * LORA)]


def _even_column_order():
    return np.concatenate([s + np.arange(w) for s, w in _even_column_starts()])


def _even_in_weight(w):
    return jnp.concatenate([w[:, s:s + n] for s, n in _even_column_starts()], axis=1).astype(bf16)


def _mixer_tables(p, mu_shift, lora_w2, w0, lora_a2, a0, k_k, k_a, r_k, lnx_g, lnx_b, conv_w, conv_b):
    pair = lambda v: v.reshape(-1, N_PAIR, LANE).transpose(1, 0, 2)
    mu = mu_shift[p]
    rows = [k_k[p][None], k_a[p][None], r_k[p].reshape(1, A_WIDTH), lnx_g[p][None], lnx_b[p][None],
            conv_b[p][None], conv_w[p], w0[p], a0[p],
            mu[None, :A_WIDTH], mu[None, A_WIDTH:2 * A_WIDTH], mu[None, 2 * A_WIDTH:3 * A_WIDTH]]
    vecs = pair(jnp.concatenate(rows, axis=0))
    assert vecs.shape == (N_PAIR, N_VEC, LANE)
    mats = pair(jnp.concatenate([lora_w2[p], lora_a2[p]], axis=0).reshape(4 * LORA, A_WIDTH))
    mats = mats.reshape(N_PAIR, 4, LORA, LANE)
    return vecs, mats, mu[None, 3 * A_WIDTH:]


V_KK, V_KA, V_RK, V_LNG, V_LNB, V_CB, V_CW, V_W0, V_A0, V_MU = 0, 1, 2, 3, 4, 5, 6, 9, 11, 13


def _scan_masks():
    ri = lax.broadcasted_iota(jnp.int32, (CHUNK, CHUNK), 0)
    ci = lax.broadcasted_iota(jnp.int32, (CHUNK, CHUNK), 1)
    eye = ri == ci
    ri2 = lax.broadcasted_iota(jnp.int32, (CHUNK, 2 * CHUNK), 0)
    ci2 = lax.broadcasted_iota(jnp.int32, (CHUNK, 2 * CHUNK), 1) & (CHUNK - 1)
    strict2 = (ri2 > ci2, ri2 < ci2)
    incl2 = (ri2 >= ci2, ri2 <= ci2)
    levels = []
    for sh in range(6):
        levels.append(((ri >> (sh + 1)) == (ci >> (sh + 1))) & ((ri >> sh) != (ci >> sh)))
    return eye, strict2, incl2, levels


def _mixer_kernel(*refs, L, NB, has_s0, write_s, has_prev, pair):
    it = iter(refs)
    x_ref, lo_ref, vec_ref, mat_ref, mulo = (next(it) for _ in range(5))
    s0 = next(it) if has_s0 else None
    if has_prev:
        next(it)
    mix = next(it)
    sout = next(it) if write_s else None
    r_s, k_s, v_s, lo_s, kd_s, y_s, rp_s, y0_s, mt_s, nt_s, st_s = it

    j = pl.program_id(1)
    vec = vec_ref[j]
    row = lambda i: vec[i:i + 1, :]
    col = lambda i: x_ref[:, i * LANE:(i + 1) * LANE]
    lora_w = [mat_ref[j, i].astype(bf16) for i in range(4)]

    R = NB * L
    nch, nch_seq = R // CHUNK, L // CHUNK
    rows = lax.broadcasted_iota(jnp.int32, (R, 1), 0)
    first, last = (rows & (L - 1)) == 0, (rows & (L - 1)) == L - 1

    def prev(f):
        return jnp.where(first, 0.0, pltpu.roll(f, 1, 0))

    def nxt(f):
        return jnp.where(last, 0.0, pltpu.roll(f, R - 1, 0))

    def tshift(f, mu):
        return f + mu * (0.5 * (prev(f) + nxt(f)) - f)

    bd = _head_block_diag()

    r_s[...] = tshift(col(0), row(V_MU))
    k_s[...] = tshift(col(1), row(V_MU + 1))
    v_s[...] = tshift(col(2), row(V_MU + 2))
    lo_s[...] = tshift(lo_ref[...], mulo[...])
    for q in range(NB):
        for d in range(2):
            st_s[q, d] = jnp.zeros((LANE, LANE), f32)
            if has_s0:
                for hh in range(2):
                    st_s[q, d, hh * A_HEAD:(hh + 1) * A_HEAD, hh * A_HEAD:(hh + 1) * A_HEAD] = s0[q, d, hh]

    eye, strict2, incl2, levels = _scan_masks()
    eye_f = eye.astype(f32)
    lane = lax.broadcasted_iota(jnp.int32, (1, LANE), 1)
    half = (lane < A_HEAD, lane >= A_HEAD)
    ri = lax.broadcasted_iota(jnp.int32, (LANE, LANE), 0)
    ci = lax.broadcasted_iota(jnp.int32, (LANE, LANE), 1)
    eye_pair = ri == ci
    crow = lax.broadcasted_iota(jnp.int32, (CHUNK, 1), 0)

    def setup_dots(c):
        rs = slice(c * CHUNK, (c + 1) * CHUNK)
        k2, lo = k_s[rs, :], lo_s[rs, :]
        kk = k2 * row(V_KK)
        st = dict(c=c, rs=rs, k2=k2, kk=kk, ssq=_head_sum(kk * kk, bd), wl=[], al=[])
        for d in range(2):
            lwd = lo[:, d * LORA:(d + 1) * LORA]
            lad = lo[:, 2 * LORA + d * LORA:2 * LORA + (d + 1) * LORA]
            st["wl"].append(_dot(jnp.tanh(lwd).astype(bf16), lora_w[d]))
            st["al"].append(_dot(lad.astype(bf16), lora_w[2 + d]))
        return st

    def setup_chains(st, dirs=(0, 1)):
        c, rs, k2 = st["c"], st["rs"], st["k2"]
        r2, v2 = r_s[rs, :], v_s[rs, :]
        kap2 = st["kk"] / jnp.maximum(jnp.sqrt(st["ssq"]), 1e-12)
        groups = []
        for d in dirs:
            wl = row(V_W0 + d) + st["wl"][d]
            softplus = jnp.maximum(-wl, 0.0) + jnp.log(1.0 + jnp.exp(-jnp.abs(wl)))
            lw = -jnp.exp(-softplus - 0.5)
            g = lw
            for sh in range(6):
                stp = 1 << sh
                if d == 0:
                    g = g + jnp.where(crow >= stp, pltpu.roll(g, stp, 0), 0.0)
                else:
                    g = g + jnp.where(crow < CHUNK - stp, pltpu.roll(g, CHUNK - stp, 0), 0.0)
            a = _sigmoid(row(V_A0 + d) + st["al"][d])
            kd2 = k2 * (1.0 + (a - 1.0) * row(V_KA))
            kd_s[d, rs, :] = kd2
            b2 = a * kap2
            tot = g[CHUNK - 1:CHUNK, :] if d == 0 else g[0:1, :]
            e_ng = jnp.exp(-g)
            e_rem = jnp.exp(tot - g)
            kt2 = kap2 * jnp.exp(g - lw)
            rt2 = r2 * jnp.exp(g)
            lhs2 = jnp.concatenate([kt2, rt2], axis=0).astype(bf16)
            grp = dict(
                d=d, c=c, rs=rs, rt2=rt2, v2=v2, v2b=v2.astype(bf16), e_tot=jnp.exp(tot),
                kt2sw=pltpu.roll(kt2, A_HEAD, 1),
                rhs2=jnp.concatenate([kd2 * e_ng, b2 * e_ng], axis=0).astype(bf16),
                kbbb2=jnp.concatenate([kd2 * e_rem, b2 * e_rem], axis=0).astype(bf16), chains=[])
            for hh in range(2):
                ch = dict(grp=grp, d=d, hh=hh)
                lhs = jnp.where(half[hh], lhs2, jnp.zeros_like(lhs2))
                ch["a"] = _dot_nt(lhs, grp["rhs2"])
                grp["chains"].append(ch)
            groups.append(grp)
        return groups

    def chain_stages(groups):
        chains = [ch for grp in groups for ch in grp["chains"]]
        for ch in chains:
            top = jnp.where(strict2[ch["d"]], ch["a"][:CHUNK], 0.0)
            ch["bot"] = jnp.where(incl2[ch["d"]], ch["a"][CHUNK:], 0.0).astype(bf16)
            ch["akk"] = top[:, :CHUNK].astype(bf16)
            akb = top[:, CHUNK:]
            ch["akb"] = akb
            ch["ab"] = akb.astype(bf16)
            ch["t"] = eye_f - jnp.where(levels[0], akb, 0.0)
        yield
        for sh in range(1, 6):
            m, blk = levels[sh], 1 << sh
            if blk < 8:
                for ch in chains:
                    ch["tb"] = ch["t"].astype(bf16)
                    off = jnp.where(m, ch["ab"], jnp.zeros_like(ch["ab"]))
                    ch["p"] = _dot(off, ch["tb"]).astype(bf16)
                yield
                for ch in chains:
                    ch["t"] = ch["t"] - _dot(ch["tb"], ch["p"])
                yield
                continue
            for ch in chains:
                ch["upd"] = [k for k in range(CHUNK // blk) if k % 2 == 1 - ch["d"]]
                take = lambda x: jnp.concatenate([x[k * blk:(k + 1) * blk] for k in ch["upd"]], axis=0)
                ch["tb"] = ch["t"].astype(bf16)
                ch["trow"] = take(ch["t"])
                ch["pc"] = _dot(take(jnp.where(m, ch["akb"], 0.0)).astype(bf16), ch["tb"])
            yield
            for ch in chains:
                zero = jnp.zeros((blk, CHUNK), f32)
                pcs = {k: ch["pc"][i * blk:(i + 1) * blk] for i, k in enumerate(ch["upd"])}
                pfull = jnp.concatenate([pcs.get(k, zero) for k in range(CHUNK // blk)], axis=0).astype(bf16)
                new = ch["trow"] - _dot(ch["trow"].astype(bf16), pfull)
                rows = {k: new[i * blk:(i + 1) * blk] for i, k in enumerate(ch["upd"])}
                ch["t"] = jnp.concatenate(
                    [rows[k] if k in rows else ch["t"][k * blk:(k + 1) * blk] for k in range(CHUNK // blk)], axis=0)
            yield
        for ch in chains:
            ch["w0v"] = _dot(ch["akk"], ch["grp"]["v2b"])
        yield
        for ch in chains:
            rhs = jnp.where(half[ch["hh"]], ch["w0v"], ch["grp"]["kt2sw"])
            ch["x"] = _dot(ch["t"].astype(bf16), rhs.astype(bf16))
        yield
        for ch in chains:
            vtop = jnp.where(half[ch["hh"]], ch["grp"]["v2"], 0.0)
            ch["z"] = jnp.concatenate([vtop, -ch["x"]], axis=0).astype(bf16)
            ch["o"] = _dot(ch["bot"], ch["z"])
        yield
        for ch in chains:
            ch["tn"] = _dot_tn(ch["z"], ch["grp"]["kbbb2"])
        yield
        for grp in groups:
            d, rs = grp["d"], grp["rs"]
            c0, c1 = grp["chains"]
            y0_s[d, rs, :] = jnp.where(half[0], c0["o"], c1["o"])
            rp_s[d, rs, :] = grp["rt2"] + pltpu.roll(jnp.where(half[0], c1["o"], c0["o"]), A_HEAD, 1)
            ps = slice(grp["c"] * LANE, (grp["c"] + 1) * LANE)
            nt_s[d, ps, :] = jnp.concatenate([jnp.where(half[0], c0["tn"][:A_HEAD], 0.0),
                                              jnp.where(half[1], c1["tn"][A_HEAD:], 0.0)], axis=0)
            mt_s[d, ps, :] = jnp.where(eye_pair, grp["e_tot"], 0.0) + jnp.concatenate(
                [jnp.where(half[0], c0["tn"][A_HEAD:], 0.0), jnp.where(half[1], c1["tn"][:A_HEAD], 0.0)],
                axis=0)

    def seq_step(seqs, i):
        steps = []
        for q in seqs:
            for d in range(2):
                c = q * nch_seq + (i if d == 0 else nch_seq - 1 - i)
                steps.append((q, d, slice(c * CHUNK, (c + 1) * CHUNK), slice(c * LANE, (c + 1) * LANE),
                              st_s[q, d].astype(bf16)))
        for q, d, rs, ps, sb in steps:
            st_s[q, d] = _dot(sb, mt_s[d, ps, :].astype(bf16)) + nt_s[d, ps, :]
        for q, d, rs, ps, sb in steps:
            y_s[d, rs, :] = y0_s[d, rs, :] + _dot_nt(rp_s[d, rs, :].astype(bf16), sb)

    def finish_a(c, fin):
        rs = slice(c * CHUNK, (c + 1) * CHUNK)
        y = y_s[0, rs, :] + y_s[1, rs, :]
        kmean = 0.5 * (kd_s[0, rs, :] + kd_s[1, rs, :])
        fin[c] = dict(rs=rs, y=y, ysum=_head_sum(y, bd), bsum=_head_sum(r_s[rs, :] * kmean * row(V_RK), bd))

    def finish_b(c, fin):
        f = fin[c]
        f["yc"] = f["y"] - f["ysum"] * (1.0 / A_HEAD)
        f["vsum"] = _head_sum(f["yc"] * f["yc"], bd)

    def finish_c(c, fin):
        f = fin[c]
        rs = f["rs"]
        yn = f["yc"] * lax.rsqrt(f["vsum"] * (1.0 / A_HEAD) + GN_EPS) * row(V_LNG) + row(V_LNB)
        mix[rs, :LANE] = (yn + f["bsum"] * v_s[rs, :]) * _silu(x_ref[rs, 3 * LANE:4 * LANE])

    def finish_fillers(chunks, fin):
        chunks = list(chunks)
        out = []
        for t in range(len(chunks) + 2):
            slot = []
            if t < len(chunks):
                slot.append(functools.partial(finish_a, chunks[t], fin))
            if 0 <= t - 1 < len(chunks):
                slot.append(functools.partial(finish_b, chunks[t - 1], fin))
            if 0 <= t - 2 < len(chunks):
                slot.append(functools.partial(finish_c, chunks[t - 2], fin))
            out.append(slot)
        return out

    def conv_branch():
        u = col(5) * col(6)
        conv = prev(u) * row(V_CW) + u * row(V_CW + 1) + nxt(u) * row(V_CW + 2) + row(V_CB)
        mix[:, LANE:] = col(4) * conv * _silu(col(7))

    def run(stages, slots):
        slots = list(slots)
        for _ in stages:
            if slots:
                for f in slots.pop(0):
                    f()
        for slot in slots:
            for f in slot:
                f()

    hc = nch // 2
    fin = {}
    if NB >= 2:
        dots = [setup_dots(c) for c in range(hc)]
        first_half = [grp for st in dots for grp in setup_chains(st)]
        second_half, pending = [], {}
        slots = []
        for t in range(hc + 1):
            slot = []
            if t < hc:
                slot.append(functools.partial(lambda c: pending.__setitem__(c, setup_dots(c)), hc + t))
            if t >= 1:
                slot.append(functools.partial(lambda c: second_half.extend(setup_chains(pending[c])), hc + t - 1))
            slots.append(slot)
        run(chain_stages(first_half), slots)
        early = list(range(NB // 2))
        late = list(range(NB // 2, NB))
        slots = [[conv_branch]] + [[functools.partial(seq_step, early, i)] for i in range(nch_seq)]
        slots += finish_fillers(range(hc), fin)
        run(chain_stages(second_half), slots)
        for i in range(nch_seq):
            seq_step(late, i)
        run(iter(()), finish_fillers(range(hc, nch), fin))
    else:
        dots = [setup_dots(c) for c in range(nch)]
        lead = lambda c: 0 if c < hc else 1
        group1 = [grp for st in dots for grp in setup_chains(st, (lead(st["c"]),))]
        group2 = []
        run(chain_stages(group1),
            [[functools.partial(lambda st: group2.extend(setup_chains(st, (1 - lead(st["c"]),))), st)]
             for st in dots])
        run(chain_stages(group2),
            [[conv_branch]] + [[functools.partial(seq_step, [0], i)] for i in range(hc)])
        ready = []
        for i in range(hc, nch_seq):
            seq_step([0], i)
            ready += [i, nch_seq - 1 - i]
            for c in list(ready):
                stage = fin.get(c, {}).get("stage", 0)
                (finish_a, finish_b, finish_c)[stage](c, fin)
                fin[c]["stage"] = stage + 1
                if stage == 2:
                    ready.remove(c)
        while ready:
            for c in list(ready):
                stage = fin[c]["stage"]
                (finish_a, finish_b, finish_c)[stage](c, fin)
                fin[c]["stage"] = stage + 1
                if stage == 2:
                    ready.remove(c)

    if write_s:
        if not has_prev:
            for pp in range(N_PAIRS):
                if pp != pair:
                    sout[:, pp] = jnp.zeros((NB, 2, 2, A_HEAD, A_HEAD), f32)
        own = sout if has_prev else sout.at[:, pair]
        for q in range(NB):
            for d in range(2):
                for hh in range(2):
                    own[q, d, hh] = st_s[q, d, hh * A_HEAD:(hh + 1) * A_HEAD, hh * A_HEAD:(hh + 1) * A_HEAD]


def _mixer(proj, tables, s0, pair, *, L, n_seq, write_s, s_prev=None):
    vecs, mats, mulo = tables
    has_s0 = s0 is not None
    R = MIXER_ROWS
    nb = R // L
    in_specs = [
        pl.BlockSpec((R, PAIR_COLS), lambda s, j: (s, j)),
        pl.BlockSpec((R, 4 * LORA), lambda s, j: (s, N_PAIR * PAIR_COLS // (4 * LORA))),
        pl.BlockSpec((N_PAIR, N_VEC, LANE), lambda s, j: (0, 0, 0)),
        pl.BlockSpec((N_PAIR, 4, LORA, LANE), lambda s, j: (0, 0, 0, 0)),
        pl.BlockSpec((1, 4 * LORA), lambda s, j: (0, 0)),
    ]
    args = [proj, proj, vecs, mats, mulo]
    if has_s0:
        in_specs.append(pl.BlockSpec((nb, None, 2, 2, A_HEAD, A_HEAD), lambda s, j: (s, pair, 0, j, 0, 0)))
        args.append(s0)
    out_specs = [pl.BlockSpec((R, 2 * LANE), lambda s, j: (s, j))]
    out_shape = [jax.ShapeDtypeStruct((n_seq * L, D_MODEL), f32)]
    aliases = {}
    if write_s:
        out_shape.append(jax.ShapeDtypeStruct((n_seq, N_PAIRS, 2, A_HEADS, A_HEAD, A_HEAD), f32))
        if s_prev is None:
            out_specs.append(pl.BlockSpec((nb, N_PAIRS, 2, 2, A_HEAD, A_HEAD), lambda s, j: (s, 0, 0, j, 0, 0)))
        else:
            out_specs.append(pl.BlockSpec((nb, None, 2, 2, A_HEAD, A_HEAD), lambda s, j: (s, pair, 0, j, 0, 0)))
            aliases = {len(args): 1}
            in_specs.append(pl.BlockSpec(memory_space=pl.ANY))
            args.append(s_prev)
    tok = lambda: pltpu.VMEM((R, LANE), f32)
    tok2 = lambda: pltpu.VMEM((2, R, LANE), f32)
    maps = lambda: pltpu.VMEM((2, 2 * R, LANE), f32)
    scratch = [tok(), tok(), tok(), pltpu.VMEM((R, 4 * LORA), f32), tok2(), tok2(), tok2(), tok2(),
               maps(), maps(), pltpu.VMEM((nb, 2, LANE, LANE), f32)]
    return pl.pallas_call(
        functools.partial(_mixer_kernel, L=L, NB=nb, has_s0=has_s0, write_s=write_s, has_prev=bool(aliases),
                          pair=pair),
        grid=(n_seq // nb, N_PAIR),
        in_specs=in_specs,
        out_specs=out_specs,
        out_shape=out_shape,
        scratch_shapes=scratch,
        input_output_aliases=aliases,
        compiler_params=pltpu.CompilerParams(vmem_limit_bytes=VMEM_LIMIT),
        name="rwkv_conv_mixer_L%d" % L,
    )(*args)


def _attn_kernel(*refs, LQ, LN, NB, NQ, PAST, rope, write_cache, n_prev, pair):
    it = iter(refs)
    q_ref, k_ref, v_ref, g_ref, qg_ref, kg_ref = (next(it) for _ in range(6))
    if rope:
        cosq, sinq, cosk, sink = (next(it) for _ in range(4))
    if PAST:
        ck, cv = next(it), next(it)
    for _ in range(n_prev):
        next(it)
    out_ref = next(it)
    if write_cache:
        nk_ref, nv_ref = next(it), next(it)
    kbuf, vbuf, obuf = it

    bd = _head_block_diag()
    lane = lax.broadcasted_iota(jnp.int32, (1, LANE), 1)
    first_half = (lane & 31) < 16

    def rms(x, g):
        return x * lax.rsqrt(_head_sum(x * x, bd) * (1.0 / C_HEAD) + EPS) * g

    def rotate(x, cos, sin):
        outs = []
        for i in range(x.shape[1] // LANE):
            xb = x[:, i * LANE:(i + 1) * LANE]
            sw = jnp.where(first_half, pltpu.roll(xb, LANE - 16, 1), pltpu.roll(xb, 16, 1))
            outs.append(xb * cos + sw * sin)
        return jnp.concatenate(outs, axis=1)

    def build_keys():
        kn = rms(k_ref[...], kg_ref[...])
        vx = v_ref[...]
        if write_cache:
            nk_own, nv_own = nk_ref, nv_ref
            if not n_prev:
                for pp in range(N_PAIRS):
                    if pp != pair:
                        nk_ref[:, pp] = jnp.zeros((NB, C_KV_HEADS, C_HEAD, LN), f32)
                        nv_ref[:, pp] = jnp.zeros((NB, C_KV_HEADS, C_HEAD, LN), f32)
                nk_own, nv_own = nk_ref.at[:, pair], nv_ref.at[:, pair]
            for b in range(NB):
                knt = kn[b * LN:(b + 1) * LN, :].T
                vxt = vx[b * LN:(b + 1) * LN, :].T
                for h in range(C_KV_HEADS):
                    nk_own[b, h] = knt[h * C_HEAD:(h + 1) * C_HEAD, :]
                    nv_own[b, h] = vxt[h * C_HEAD:(h + 1) * C_HEAD, :]
        if rope:
            kn = rotate(kn, cosk[...], sink[...])
        for b in range(NB):
            for h in range(C_KV_HEADS):
                kbuf[b, h] = kn[b * LN:(b + 1) * LN, h * C_HEAD:(h + 1) * C_HEAD].astype(bf16)
                vbuf[b, h] = vx[b * LN:(b + 1) * LN, h * C_HEAD:(h + 1) * C_HEAD].astype(bf16)

    if NQ == 1:
        build_keys()
    else:
        pl.when(pl.program_id(1) == 0)(build_keys)

    qn = rms(q_ref[...], qg_ref[...])
    if rope:
        qn = rotate(qn, cosq[...], sinq[...])
    qn = (qn * (C_HEAD ** -0.5 * LOG2E)).astype(bf16)
    units = [(b, h) for b in range(NB) for h in range(C_KV_HEADS)]
    heads = [[slice((h * C_GROUP + g) * C_HEAD, (h * C_GROUP + g + 1) * C_HEAD) for g in range(C_GROUP)]
             for h in range(C_KV_HEADS)]

    def scores(b, h):
        q4 = jnp.concatenate([qn[b * LQ:(b + 1) * LQ, sl] for sl in heads[h]], axis=0)
        s_new = _dot_nt(q4, kbuf[b, h])
        if not PAST:
            return s_new
        return jnp.concatenate([_dot(q4, ck[b, h].astype(bf16)), s_new], axis=1)

    s_next = scores(*units[0])
    for n, (b, h) in enumerate(units):
        s = s_next
        if n + 1 < len(units):
            s_next = scores(*units[n + 1])
        p = jnp.exp2(s - jnp.max(s, axis=-1, keepdims=True))
        pb = p.astype(bf16)
        o = _dot(pb[:, PAST:], vbuf[b, h])
        if PAST:
            o = o + _dot_nt(pb[:, :PAST], cv[b, h].astype(bf16))
        o = o / jnp.sum(p, axis=-1, keepdims=True)
        for g, sl in enumerate(heads[h]):
            obuf[b * LQ:(b + 1) * LQ, sl] = o[g * LQ:(g + 1) * LQ]
    out_ref[...] = obuf[...] * _silu(g_ref[...])


def _rope_tables():
    pos = np.arange(DEC_SEQ)
    inv = ROPE_THETA ** (-np.arange(0, ROPE_HALF, 2, dtype=np.float32) / ROPE_HALF)
    ang_r = (pos // GRID_W).astype(np.float32)[:, None] * inv
    ang_c = (pos % GRID_W).astype(np.float32)[:, None] * inv
    cos64 = np.concatenate([np.cos(ang_r), np.cos(ang_r), np.cos(ang_c), np.cos(ang_c)], axis=1)
    sin64 = np.concatenate([-np.sin(ang_r), np.sin(ang_r), -np.sin(ang_c), np.sin(ang_c)], axis=1)
    return jnp.asarray(np.tile(cos64, (1, 2)), f32), jnp.asarray(np.tile(sin64, (1, 2)), f32)


def _attention(proj, q_g, k_g, cache_k, cache_v, pair, tables, *, n_seq, L, sample, kv_prev=None):
    LQ = ATTN_ROWS
    nq = L // LQ
    nb = 1 if sample else ATTN_PROMPT_SEQS
    assert nb == 1 or nq == 1
    past = PAST_LEN if sample else 0
    in_specs = [
        pl.BlockSpec((nb * LQ, D_MODEL), lambda b, i: (b * nq + i, 0)),
        pl.BlockSpec((nb * L, KV_WIDTH), lambda b, i: (b, 2 * D_MODEL // KV_WIDTH)),
        pl.BlockSpec((nb * L, KV_WIDTH), lambda b, i: (b, 2 * D_MODEL // KV_WIDTH + 1)),
        pl.BlockSpec((nb * LQ, D_MODEL), lambda b, i: (b * nq + i, 1)),
        pl.BlockSpec((1, D_MODEL), lambda b, i: (0, 0)),
        pl.BlockSpec((1, KV_WIDTH), lambda b, i: (0, 0)),
    ]
    args = [proj, proj, proj, proj, q_g, k_g]
    if sample:
        cos, sin = tables
        in_specs += [pl.BlockSpec((LQ, LANE), lambda b, i: (i, 0)),
                     pl.BlockSpec((LQ, LANE), lambda b, i: (i, 0)),
                     pl.BlockSpec((L, LANE), lambda b, i: (0, 0)),
                     pl.BlockSpec((L, LANE), lambda b, i: (0, 0))]
        args += [cos, sin, cos, sin]
        cspec = pl.BlockSpec((nb, None, C_KV_HEADS, C_HEAD, PAST_LEN), lambda b, i: (b, pair, 0, 0, 0))
        in_specs += [cspec, cspec]
        args += [cache_k.transpose(0, 1, 2, 4, 3), cache_v.transpose(0, 1, 2, 4, 3)]
    out_specs = [pl.BlockSpec((nb * LQ, D_MODEL), lambda b, i: (b * nq + i, 0))]
    out_shape = [jax.ShapeDtypeStruct((n_seq * L, D_MODEL), f32)]
    aliases = {}
    if not sample:
        if kv_prev is None:
            nspec = pl.BlockSpec((nb, N_PAIRS, C_KV_HEADS, C_HEAD, L), lambda b, i: (b, 0, 0, 0, 0))
        else:
            nspec = pl.BlockSpec((nb, None, C_KV_HEADS, C_HEAD, L), lambda b, i: (b, pair, 0, 0, 0))
        out_specs += [nspec, nspec]
        out_shape += [jax.ShapeDtypeStruct((n_seq, N_PAIRS, C_KV_HEADS, C_HEAD, L), f32)] * 2
        if kv_prev is not None:
            aliases = {len(args): 1, len(args) + 1: 2}
            in_specs += [pl.BlockSpec(memory_space=pl.ANY)] * 2
            args += list(kv_prev)
    return pl.pallas_call(
        functools.partial(_attn_kernel, LQ=LQ, LN=L, NB=nb, NQ=nq, PAST=past, rope=sample, write_cache=not sample,
                          n_prev=len(aliases), pair=pair),
        grid=(n_seq // nb, nq),
        in_specs=in_specs,
        out_specs=out_specs,
        out_shape=out_shape,
        input_output_aliases=aliases,
        scratch_shapes=[pltpu.VMEM((nb, C_KV_HEADS, L, C_HEAD), bf16),
                        pltpu.VMEM((nb, C_KV_HEADS, L, C_HEAD), bf16),
                        pltpu.VMEM((nb * LQ, D_MODEL), f32)],
        compiler_params=pltpu.CompilerParams(vmem_limit_bytes=VMEM_LIMIT),
        name="gqa_sample" if sample else "gqa_prompt",
    )(*args)


def kernel(x_prompt, x_sample, c, state_rwkv, cache_k, cache_v, c_ctx, w_ada, b_ada, norm_g, final_g,
           w_in_e, mu_shift, lora_w2, w0, lora_a2, a0, k_k, k_a, r_k, lnx_g, lnx_b, conv_w, conv_b,
           w_out_e, w_in_o, q_norm_g, k_norm_g, w_out_o):
    xs = [x_prompt.reshape(STREAM_TOK, D_MODEL), x_sample.reshape(STREAM_TOK, D_MODEL)]
    cond = jnp.concatenate([c_ctx[None, :], c, jnp.zeros((N_MOD - 1 - DEC_BATCH, D_MODEL), f32)], axis=0)
    mod = _adaln(cond, w_ada, b_ada).reshape(DEPTH, N_MOD, 1, 3 * D_MODEL)

    rope = _rope_tables()
    final_g2 = final_g.reshape(1, D_MODEL)

    w_ins, w_outs = [], []
    for layer in range(DEPTH):
        p = layer // 2
        if layer % 2 == 0:
            w_ins.append(_even_in_weight(w_in_e[p]))
            wo = w_out_e[p]
            w_outs.append(jnp.concatenate(
                [wo[h * A_WIDTH + j * LANE:h * A_WIDTH + (j + 1) * LANE] for j in range(N_PAIR) for h in range(2)],
                axis=0).astype(bf16))
        else:
            wi = w_in_o[p]
            w_ins.append(jnp.concatenate([wi[:, :D_MODEL], wi[:, D_MODEL + 2 * KV_WIDTH:],
                                          wi[:, D_MODEL:D_MODEL + 2 * KV_WIDTH]], axis=1).astype(bf16))
            w_outs.append(w_out_o[p].astype(bf16))
    inp = lambda layer: (mod[layer], norm_g[layer].reshape(1, D_MODEL), w_ins[layer])

    new_rwkv = new_kv = None
    (proj_p,) = _proj(xs[0], sample=False, inp=inp(0))
    (proj_s,) = _proj(xs[1], sample=True, inp=inp(0))
    for layer in range(DEPTH):
        p = layer // 2
        if layer % 2 == 0:
            tables = _mixer_tables(p, mu_shift, lora_w2, w0, lora_a2, a0, k_k, k_a, r_k, lnx_g, lnx_b,
                                   conv_w, conv_b)
            act_p, new_rwkv = _mixer(proj_p, tables, None, p, L=SEQ, n_seq=BATCH, write_s=True, s_prev=new_rwkv)
            (act_s,) = _mixer(proj_s, tables, state_rwkv, p, L=DEC_SEQ, n_seq=DEC_BATCH, write_s=False)
        else:
            q_g = jnp.tile(q_norm_g[p], C_Q_HEADS).reshape(1, D_MODEL)
            k_g = jnp.tile(k_norm_g[p], C_KV_HEADS).reshape(1, KV_WIDTH)
            act_p, *new_kv = _attention(proj_p, q_g, k_g, None, None, p, None, n_seq=BATCH, L=SEQ,
                                        sample=False, kv_prev=new_kv)
            (act_s,) = _attention(proj_s, q_g, k_g, cache_k, cache_v, p, rope, n_seq=DEC_BATCH,
                                  L=DEC_SEQ, sample=True)
        if layer < DEPTH - 1:
            xs[0], proj_p = _proj(xs[0], sample=False, out=(act_p, mod[layer], w_outs[layer]), inp=inp(layer + 1))
            xs[1], proj_s = _proj(xs[1], sample=True, out=(act_s, mod[layer], w_outs[layer]), inp=inp(layer + 1))
        else:
            (xs[0],) = _proj(xs[0], sample=False, out=(act_p, mod[layer], w_outs[layer]), final_g=final_g2)
            (xs[1],) = _proj(xs[1], sample=True, out=(act_s, mod[layer], w_outs[layer]), final_g=final_g2)
    return (xs[0].reshape(BATCH, SEQ, D_MODEL), xs[1].reshape(DEC_BATCH, DEC_SEQ, D_MODEL),
            new_rwkv, new_kv[0].transpose(0, 1, 2, 4, 3), new_kv[1].transpose(0, 1, 2, 4, 3))
```
